```python
import jax, jax.numpy as jnp
from jax import lax
import numpy as np

D_MODEL = 1024
BATCH = 2
SEQ = 8192
DEPTH = 2
DEC_BATCH = 32
DEC_SEQ = 1
PAST_LEN = 16384
PAGE_SIZE = 128

MIX_W = D_MODEL
HG_W = MIX_W // 2
HG_DK = 128
HG_DV = 128
HG_HEADS = HG_W // HG_DV
HG_CHUNK = 64
HEAD_DIM = 64
NSA_W = MIX_W - HG_W
NSA_HEADS = NSA_W // HEAD_DIM
NSA_KV_HEADS = 2
GQA_REP = NSA_HEADS // NSA_KV_HEADS
KV_W = NSA_KV_HEADS * 2 * HEAD_DIM
N_GATES = NSA_HEADS * 3
CMP_STRIDE = 16
CMP_RATIO = 2
CMP_BLOCK = CMP_STRIDE * CMP_RATIO
SEL_BLOCK = 64
N_SELECT = 16
WINDOW = 512
Q_BLOCK = 128
ROPE_THETA = 500000.0
ROT_DIM = HEAD_DIM // 4
D_FF = -(-8 * D_MODEL // (3 * 256)) * 256
SPLIT_SIZES = (HG_W, HG_W, HG_W, HG_W, NSA_W, KV_W, KV_W, KV_W, N_GATES)
IN_W = sum(SPLIT_SIZES)
EPS = 1e-6
FORCED_SCORE = 1e6
NEG_BIG = -1e30
LB_FLOOR = 1e-30
F32 = jnp.float32

kernel_name = 'hymba_hgrn2_nsa_decode_step'


def rmsnorm(x, g):
    xf = x.astype(F32)
    y = xf * lax.rsqrt(jnp.mean(xf * xf, axis=-1, keepdims=True) + EPS)
    return (y * g.astype(F32)).astype(x.dtype)


def rope(x, pos):
    half = ROT_DIM // 2
    inv = ROPE_THETA ** (-2.0 * jnp.arange(half, dtype=F32) / ROT_DIM)
    ang = pos.astype(F32)[:, None] * inv[None, :]
    cos, sin = jnp.cos(ang)[:, None, :], jnp.sin(ang)[:, None, :]
    xf = x.astype(F32)
    x1, x2 = xf[..., :half], xf[..., half:ROT_DIM]
    out = jnp.concatenate([x1 * cos - x2 * sin, x2 * cos + x1 * sin, xf[..., ROT_DIM:]], axis=-1)
    return out.astype(x.dtype)


def masked_softmax(s, mask):
    s = jnp.where(mask, s.astype(F32), NEG_BIG)
    m = jnp.max(s, axis=-1, keepdims=True)
    p = jnp.where(mask, jnp.exp(s - m), 0.0)
    return p / jnp.maximum(jnp.sum(p, axis=-1, keepdims=True), 1e-30)


def heads(a, n):
    return a.reshape(a.shape[:-1] + (n, a.shape[-1] // n))


def project_mixer(hn, w_in, lb, pos):
    b, t = hn.shape[:2]
    z = hn @ w_in
    cuts = [int(v) for v in np.cumsum(SPLIT_SIZES)[:-1]]
    qh, fh, ih, gh, qn, kvc, kvs, kvw, gl = jnp.split(z, cuts, axis=-1)
    fl = heads(fh, HG_HEADS).astype(F32)
    logf = jnp.logaddexp(jnp.log(jnp.maximum(lb, LB_FLOOR)), jnp.log1p(-lb) + jax.nn.log_sigmoid(fl))
    kh = (1.0 - lb) * jax.nn.sigmoid(-fl)
    kv_shape = (b, t, NSA_KV_HEADS, 2, HEAD_DIM)
    kvc = kvc.reshape(kv_shape)
    kvs = kvs.reshape(kv_shape)
    kvw = kvw.reshape(kv_shape)
    kvs = jnp.stack([rope(kvs[..., 0, :], pos), kvs[..., 1, :]], axis=-2)
    kvw = jnp.stack([rope(kvw[..., 0, :], pos), kvw[..., 1, :]], axis=-2)
    qn = heads(qn, NSA_HEADS)
    return (heads(qh, HG_HEADS), logf, kh, heads(ih, HG_HEADS), gh, qn, rope(qn, pos), kvc, kvs, kvw, gl)


def hgrn_chunked(q, logf, k, v):
    b, t, h, _ = q.shape
    n_chunks = t // HG_CHUNK

    def to_chunks(a):
        return a.astype(F32).reshape(b, n_chunks, HG_CHUNK, h, a.shape[-1]).swapaxes(0, 1)

    causal = jnp.tril(jnp.ones((HG_CHUNK, HG_CHUNK), bool))[None, :, :, None, None]

    def step(S, inp):
        qc, lfc, kc, vc = inp
        cum = jnp.cumsum(lfc, axis=1)
        o_inter = jnp.einsum('bthk,bhkv->bthv', qc * jnp.exp(cum), S)
        diff = cum[:, :, None] - cum[:, None, :]
        decay = jnp.exp(jnp.where(causal, diff, NEG_BIG))
        attn = jnp.einsum('bthk,bshk,btshk->btsh', qc, kc, decay)
        o = o_inter + jnp.einsum('btsh,bshv->bthv', attn, vc)
        last = cum[:, -1]
        S = jnp.exp(last)[..., None] * S + jnp.einsum('bshk,bshv->bhkv', kc * jnp.exp(last[:, None] - cum), vc)
        return S, o

    S0 = jnp.zeros((b, h, HG_DK, HG_DV), F32)
    S, o = lax.scan(step, S0, (to_chunks(q), to_chunks(logf), to_chunks(k), to_chunks(v)))
    return o.swapaxes(0, 1).reshape(b, t, h, HG_DV), S


def hgrn_recurrent(S0, q, logf, k, v):
    def step(S, inp):
        qt, lft, kt, vt = inp
        S = jnp.exp(lft)[..., None] * S + kt[..., None] * vt[..., None, :]
        return S, jnp.einsum('bhk,bhkv->bhv', qt, S)

    xs = tuple(a.astype(F32).swapaxes(0, 1) for a in (q, logf, k, v))
    S, o = lax.scan(step, S0.astype(F32), xs)
    return o.swapaxes(0, 1), S


def hgrn_readout(o, g, gain):
    b, t = o.shape[:2]
    on = o * lax.rsqrt(jnp.mean(o * o, axis=-1, keepdims=True) + EPS) * gain.astype(F32)
    gate = jax.nn.silu(g.astype(F32)).reshape(b, t, HG_HEADS, HG_DV)
    return (on * gate).reshape(b, t, HG_W).astype(g.dtype)


def compress(kv, pe, w):
    b, L = kv.shape[:2]
    n_chunk = L // CMP_STRIDE
    n_cmp = n_chunk - CMP_RATIO + 1
    x = kv[:, :n_chunk * CMP_STRIDE].reshape(b, n_chunk, CMP_STRIDE, NSA_KV_HEADS, 2, HEAD_DIM).astype(F32)
    pe_r = pe.astype(F32).reshape(2, CMP_RATIO, CMP_STRIDE, HEAD_DIM)
    w_r = w.astype(F32).reshape(2, CMP_RATIO, CMP_STRIDE, HEAD_DIM, HEAD_DIM)
    out = jnp.zeros((b, n_cmp, NSA_KV_HEADS, 2, HEAD_DIM), F32)
    for m in range(CMP_RATIO):
        xm = x[:, m:m + n_cmp] + jnp.transpose(pe_r[:, m], (1, 0, 2))[:, None]
        out = out + jnp.einsum('bncgkd,kcde->bngke', xm, w_r[:, m])
    return out.astype(kv.dtype)


def sel_overlap(n_cmp, n_sel):
    cs = jnp.arange(n_cmp) * CMP_STRIDE
    ss = jnp.arange(n_sel) * SEL_BLOCK
    return ((cs[:, None] < ss[None, :] + SEL_BLOCK) & (cs[:, None] + CMP_BLOCK > ss[None, :])).astype(F32)


def nsa_core(q, q_rot, pos_q, kv_cmp, cmp_end, overlap, sel_gather, kv_win, pos_win, gate_logits):
    b, nq = q.shape[:2]
    scale = HEAD_DIM ** -0.5
    qc = q.reshape(b, nq, NSA_KV_HEADS, GQA_REP, HEAD_DIM)
    qr = q_rot.reshape(b, nq, NSA_KV_HEADS, GQA_REP, HEAD_DIM)
    s = jnp.einsum('bqgrd,bngd->bqgrn', qc, kv_cmp[..., 0, :]) * scale
    p_c = masked_softmax(s, (cmp_end[None, :] <= pos_q[:, None])[None, :, None, None, :])
    o_c = jnp.einsum('bqgrn,bngd->bqgrd', p_c.astype(kv_cmp.dtype), kv_cmp[..., 1, :]).astype(F32)
    n_sel = overlap.shape[1]
    imp = jnp.einsum('bqgrn,nj->bqgj', p_c, overlap)
    blk = jnp.arange(n_sel)[None, :]
    cur = (pos_q // SEL_BLOCK)[:, None]
    valid = (blk * SEL_BLOCK <= pos_q[:, None])[None, :, None, :]
    forced = ((blk == 0) | (blk == cur) | (blk == cur - 1))[None, :, None, :]
    score = jnp.where(valid, jnp.where(forced, FORCED_SCORE, imp), NEG_BIG)
    n_top = min(N_SELECT, n_sel)
    top_val, top_idx = lax.top_k(score, n_top)
    tok = (top_idx[..., None] * SEL_BLOCK + jnp.arange(SEL_BLOCK)).reshape(b, nq, NSA_KV_HEADS, n_top * SEL_BLOCK)
    blk_ok = top_val > 0.5 * NEG_BIG
    tok_ok = jnp.repeat(blk_ok, SEL_BLOCK, axis=-1) & (tok <= pos_q[None, :, None, None])
    kv_sel = sel_gather(tok)
    s = jnp.einsum('bqgrd,bqgnd->bqgrn', qr, kv_sel[..., 0, :]) * scale
    p_s = masked_softmax(s, tok_ok[:, :, :, None, :])
    o_s = jnp.einsum('bqgrn,bqgnd->bqgrd', p_s.astype(kv_sel.dtype), kv_sel[..., 1, :]).astype(F32)
    s = jnp.einsum('bqgrd,bsgd->bqgrs', qr, kv_win[..., 0, :]) * scale
    dist = pos_q[:, None] - pos_win[None, :]
    m_w = ((dist >= 0) & (dist < WINDOW) & (pos_win[None, :] >= 0))[None, :, None, None, :]
    p_w = masked_softmax(s, m_w)
    o_w = jnp.einsum('bqgrs,bsgd->bqgrd', p_w.astype(kv_win.dtype), kv_win[..., 1, :]).astype(F32)
    gate = jax.nn.sigmoid(gate_logits.astype(F32)).reshape(b, nq, NSA_KV_HEADS, GQA_REP, 3)
    o = gate[..., 0:1] * o_c + gate[..., 1:2] * o_s + gate[..., 2:3] * o_w
    return o.reshape(b, nq, NSA_W).astype(q.dtype)


def nsa_prompt(q, q_rot, kv_c, kv_s, kv_w, gate_logits, pe, w_cmp):
    b, t = q.shape[:2]
    kv_cmp = compress(kv_c, pe, w_cmp)
    n_cmp = kv_cmp.shape[1]
    cmp_end = jnp.arange(n_cmp) * CMP_STRIDE + CMP_BLOCK - 1
    overlap = sel_overlap(n_cmp, -(-t // SEL_BLOCK))
    kv_w_pad = jnp.pad(kv_w, ((0, 0), (WINDOW, 0), (0, 0), (0, 0), (0, 0)))
    b_idx = jnp.arange(b)[:, None, None, None]
    g_idx = jnp.arange(NSA_KV_HEADS)[None, None, :, None]

    def sel_gather(tok):
        return kv_s[b_idx, jnp.minimum(tok, t - 1), g_idx]

    def query_block(qb):
        start = qb * Q_BLOCK
        sl = lambda a: lax.dynamic_slice_in_dim(a, start, Q_BLOCK, axis=1)
        pos_q = start + jnp.arange(Q_BLOCK)
        kv_win = lax.dynamic_slice_in_dim(kv_w_pad, start, WINDOW + Q_BLOCK, axis=1)
        pos_win = start - WINDOW + jnp.arange(WINDOW + Q_BLOCK)
        return nsa_core(sl(q), sl(q_rot), pos_q, kv_cmp, cmp_end, overlap, sel_gather, kv_win, pos_win, sl(gate_logits))

    out = lax.map(query_block, jnp.arange(t // Q_BLOCK))
    return out.swapaxes(0, 1).reshape(b, t, NSA_W)


def nsa_sample(q, q_rot, kv_c_new, kv_s_new, kv_w_new, gate_logits, cmp_pool, sel_pool, win_buf, page_table, pe, w_cmp):
    b, t = q.shape[:2]
    past = page_table.shape[1] * PAGE_SIZE
    past_c = cmp_pool[page_table].reshape(b, past, NSA_KV_HEADS, 2, HEAD_DIM)
    kv_cmp = compress(jnp.concatenate([past_c, kv_c_new], axis=1), pe, w_cmp)
    n_cmp = kv_cmp.shape[1]
    cmp_end = jnp.arange(n_cmp) * CMP_STRIDE + CMP_BLOCK - 1
    overlap = sel_overlap(n_cmp, -(-(past + t) // SEL_BLOCK))
    pos_q = past + jnp.arange(t)
    b_idx = jnp.arange(b)[:, None, None, None]
    g_idx = jnp.arange(NSA_KV_HEADS)[None, None, :, None]

    def sel_gather(tok):
        in_past = tok < past
        tp = jnp.minimum(tok, past - 1)
        phys = page_table[b_idx, tp // PAGE_SIZE]
        old = sel_pool[phys, tp % PAGE_SIZE, g_idx]
        new = kv_s_new[b_idx, jnp.clip(tok - past, 0, t - 1), g_idx]
        return jnp.where(in_past[..., None, None], old, new)

    w_buf = win_buf.shape[1]
    kv_win = jnp.concatenate([win_buf, kv_w_new], axis=1)
    pos_win = past - w_buf + jnp.arange(w_buf + t)
    o = nsa_core(q, q_rot, pos_q, kv_cmp, cmp_end, overlap, sel_gather, kv_win, pos_win, gate_logits)
    return o, kv_win[:, -w_buf:]


def ffn(h, w_in, w_out):
    a, g = jnp.split(h @ w_in, 2, axis=-1)
    return (jax.nn.silu(a) * g) @ w_out


def finish_layer(x, mix_heads, w_out, g_mix_post, g_ffn_pre, g_ffn_post, w_ffn_in, w_ffn_out):
    x = x + rmsnorm(mix_heads @ w_out, g_mix_post)
    return x + rmsnorm(ffn(rmsnorm(x, g_ffn_pre), w_ffn_in, w_ffn_out), g_ffn_post)


def setup_inputs(seed: int = 0) -> dict:
    key = jax.random.key(seed)
    ks = jax.random.split(key, 20)
    n_pages = PAST_LEN // PAGE_SIZE
    n_used = DEC_BATCH * n_pages
    n_pool = n_used + max(1, n_used // 4)
    w_buf = min(WINDOW, PAST_LEN)
    nrm = lambda k, s: jax.random.normal(k, s, F32)
    gain = lambda k: 1.0 + 0.05 * nrm(k, (DEPTH, D_MODEL))
    page_table = jax.random.permutation(ks[6], n_pool)[:n_used].reshape(DEC_BATCH, n_pages).astype(jnp.int32)
    return {
        'x_prompt': nrm(ks[0], (BATCH, SEQ, D_MODEL)),
        'x_sample': nrm(ks[1], (DEC_BATCH, DEC_SEQ, D_MODEL)),
        'state_hgrn': nrm(ks[2], (DEPTH, DEC_BATCH, HG_HEADS, HG_DK, HG_DV)),
        'cache_cmp_kv': nrm(ks[3], (DEPTH, n_pool, PAGE_SIZE, NSA_KV_HEADS, 2, HEAD_DIM)),
        'cache_sel_kv': nrm(ks[4], (DEPTH, n_pool, PAGE_SIZE, NSA_KV_HEADS, 2, HEAD_DIM)),
        'cache_win_kv': nrm(ks[5], (DEPTH, DEC_BATCH, w_buf, NSA_KV_HEADS, 2, HEAD_DIM)),
        'page_table': page_table,
        'norm_mix_pre': gain(ks[7]),
        'norm_mix_post': gain(ks[8]),
        'norm_ffn_pre': gain(ks[9]),
        'norm_ffn_post': gain(ks[10]),
        'w_in': nrm(ks[11], (DEPTH, D_MODEL, IN_W)) * D_MODEL ** -0.5,
        'w_out': nrm(ks[12], (DEPTH, MIX_W, D_MODEL)) * MIX_W ** -0.5,
        'hg_lb_logits': nrm(ks[13], (DEPTH, HG_HEADS * HG_DK)),
        'hg_out_norm': 1.0 + 0.05 * nrm(ks[14], (DEPTH, HG_DV)),
        'cmp_pe': 0.1 * nrm(ks[15], (DEPTH, 2, CMP_BLOCK, HEAD_DIM)),
        'w_cmp': nrm(ks[16], (DEPTH, 2, CMP_BLOCK * HEAD_DIM, HEAD_DIM)) * (CMP_BLOCK * HEAD_DIM) ** -0.5,
        'w_ffn_in': nrm(ks[17], (DEPTH, D_MODEL, 2 * D_FF)) * D_MODEL ** -0.5,
        'w_ffn_out': nrm(ks[18], (DEPTH, D_FF, D_MODEL)) * D_FF ** -0.5,
    }


def reference(x_prompt, x_sample, state_hgrn, cache_cmp_kv, cache_sel_kv, cache_win_kv, page_table,
              norm_mix_pre, norm_mix_post, norm_ffn_pre, norm_ffn_post, w_in, w_out, hg_lb_logits,
              hg_out_norm, cmp_pe, w_cmp, w_ffn_in, w_ffn_out):
    t_p = x_prompt.shape[1]
    t_s = x_sample.shape[1]
    pos_p = jnp.arange(t_p)
    pos_s = page_table.shape[1] * PAGE_SIZE + jnp.arange(t_s)
    sm = jax.nn.softmax(hg_lb_logits.astype(F32), axis=0)
    lower = jnp.cumsum(sm, axis=0) - sm[0]
    xp, xs = x_prompt, x_sample
    hgp, hgs, cmpp, cmps, selp, sels, winp, wins = [], [], [], [], [], [], [], []
    for l in range(DEPTH):
        lb = lower[l].reshape(HG_HEADS, HG_DK)
        hq, lf, hk, hv, hg, nq, nqr, kvc, kvs, kvw, gl = project_mixer(rmsnorm(xp, norm_mix_pre[l]), w_in[l], lb, pos_p)
        ho, s_fin = hgrn_chunked(hq, lf, hk, hv)
        no = nsa_prompt(nq, nqr, kvc, kvs, kvw, gl, cmp_pe[l], w_cmp[l])
        mix = jnp.concatenate([hgrn_readout(ho, hg, hg_out_norm[l]), no], axis=-1)
        xp = finish_layer(xp, mix, w_out[l], norm_mix_post[l], norm_ffn_pre[l], norm_ffn_post[l], w_ffn_in[l], w_ffn_out[l])
        hgp.append(s_fin)
        cmpp.append(kvc)
        selp.append(kvs)
        winp.append(kvw[:, t_p - min(WINDOW, t_p):])
        hq, lf, hk, hv, hg, nq, nqr, kvc, kvs, kvw, gl = project_mixer(rmsnorm(xs, norm_mix_pre[l]), w_in[l], lb, pos_s)
        ho, s_new = hgrn_recurrent(state_hgrn[l], hq, lf, hk, hv)
        no, win_new = nsa_sample(nq, nqr, kvc, kvs, kvw, gl, cache_cmp_kv[l], cache_sel_kv[l], cache_win_kv[l],
                                 page_table, cmp_pe[l], w_cmp[l])
        mix = jnp.concatenate([hgrn_readout(ho, hg, hg_out_norm[l]), no], axis=-1)
        xs = finish_layer(xs, mix, w_out[l], norm_mix_post[l], norm_ffn_pre[l], norm_ffn_post[l], w_ffn_in[l], w_ffn_out[l])
        hgs.append(s_new)
        cmps.append(kvc)
        sels.append(kvs)
        wins.append(win_new)
    return (xp, xs, jnp.stack(hgp), jnp.stack(hgs), jnp.stack(cmpp), jnp.stack(cmps),
            jnp.stack(selp), jnp.stack(sels), jnp.stack(winp), jnp.stack(wins))
```

```python
import functools

import jax
import jax.numpy as jnp
import numpy as np
from jax import lax
from jax.experimental import pallas as pl
from jax.experimental.pallas import tpu as pltpu

F32 = jnp.float32
BF16 = jnp.bfloat16

HG_HEADS = 4
HG_DK = 128
HG_DV = 128
HG_W = HG_HEADS * HG_DV
HEAD_DIM = 64
NSA_HEADS = 8
NSA_KV_HEADS = 2
GQA_REP = NSA_HEADS // NSA_KV_HEADS
NSA_W = NSA_HEADS * HEAD_DIM
KV_W = NSA_KV_HEADS * 2 * HEAD_DIM
N_GATES = NSA_HEADS * 3
CMP_STRIDE = 16
CMP_RATIO = 2
CMP_BLOCK = CMP_STRIDE * CMP_RATIO
SEL_BLOCK = 64
N_SELECT = 16
WINDOW = 512
PAGE_SIZE = 128
ROPE_THETA = 500000.0
ROT_DIM = HEAD_DIM // 4
EPS = 1e-6
FORCED_SCORE = 1e6
NEG_BIG = -1e30
LB_FLOOR = 1e-30
REMOVED = -3e38

LANES = 128
HG_CHUNK = 16
GATE_PAD = 2 * LANES
COL_QN = 4 * HG_W
COL_KVC = COL_QN + NSA_W
COL_KVS = COL_KVC + KV_W
COL_KVW = COL_KVS + KV_W
COL_GL = COL_KVW + KV_W
IN_W_PAD = COL_GL + GATE_PAD

VMEM_LIMIT = 56 * 1024 * 1024


def _cparams(sem):
    return pltpu.CompilerParams(dimension_semantics=sem, vmem_limit_bytes=VMEM_LIMIT)


def _rms(x, g):
    return x * lax.rsqrt(jnp.mean(x * x, axis=-1, keepdims=True) + EPS) * g


def _sigmoid(x):
    return 1.0 / (1.0 + jnp.exp(-x))


def _dot(a, b):
    return jnp.dot(a, b, preferred_element_type=F32)


def _dot_nt(a, b):
    return lax.dot_general(a, b, (((1,), (1,)), ((), ())), preferred_element_type=F32)


def _dot_tn(a, b):
    return lax.dot_general(a, b, (((0,), (0,)), ((), ())), preferred_element_type=F32)


def _div_pow2(x, n):
    assert n & (n - 1) == 0
    return x >> (n.bit_length() - 1)


def _split3(x):
    hi = x.astype(BF16)
    r1 = x - hi.astype(F32)
    mid = r1.astype(BF16)
    lo = (r1 - mid.astype(F32)).astype(BF16)
    return hi, mid, lo


def _rope128(x, c, s1, s2):
    return x * c + pltpu.roll(x, LANES - ROT_DIM // 2, 1) * s1 + pltpu.roll(x, ROT_DIM // 2, 1) * s2


def _inproj_body(x_ref, g_ref, w_ref, cq_ref, s1q_ref, s2q_ref, ck_ref, s1k_ref, s2k_ref,
                 hq_ref, fl_ref, hi_ref, hg_ref, qn_ref, qr_ref, kvc_ref, kvs_ref, kvw_ref, gl_ref):
    hn = _rms(x_ref[...], g_ref[...])
    z = _dot(hn.astype(BF16), w_ref[...])
    hq_ref[...] = z[:, 0:HG_W]
    fl_ref[...] = z[:, HG_W:2 * HG_W]
    hi_ref[...] = z[:, 2 * HG_W:3 * HG_W]
    hg_ref[...] = z[:, 3 * HG_W:4 * HG_W]
    scale = HEAD_DIM ** -0.5
    cq, s1q, s2q = cq_ref[...], s1q_ref[...], s2q_ref[...]
    for j in range(NSA_W // LANES):
        blk = z[:, COL_QN + j * LANES:COL_QN + (j + 1) * LANES]
        qn_ref[:, j * LANES:(j + 1) * LANES] = (blk * scale).astype(BF16)
        qr_ref[:, j * LANES:(j + 1) * LANES] = (_rope128(blk, cq, s1q, s2q) * scale).astype(BF16)
    kvc_ref[...] = z[:, COL_KVC:COL_KVS]
    ck, s1k, s2k = ck_ref[...], s1k_ref[...], s2k_ref[...]
    for j in range(KV_W // LANES):
        kvs_ref[:, j * LANES:(j + 1) * LANES] = _rope128(
            z[:, COL_KVS + j * LANES:COL_KVS + (j + 1) * LANES], ck, s1k, s2k)
        kvw_ref[:, j * LANES:(j + 1) * LANES] = _rope128(
            z[:, COL_KVW + j * LANES:COL_KVW + (j + 1) * LANES], ck, s1k, s2k)
    gl_ref[...] = z[:, COL_GL:COL_GL + GATE_PAD]


def _inproj(x2d, gain, w_bf, tabs, tm):
    m, d = x2d.shape
    n_pos_blocks = tabs[0].shape[0] // tm
    row = lambda i: (i, 0)
    tab = lambda i: (i % n_pos_blocks, 0)
    const = lambda i: (0, 0)
    widths = (HG_W, HG_W, HG_W, HG_W, NSA_W, NSA_W, KV_W, KV_W, KV_W, GATE_PAD)
    dtypes = (F32, F32, F32, F32, BF16, BF16, F32, F32, F32, F32)
    return pl.pallas_call(
        _inproj_body,
        grid=(m // tm,),
        in_specs=[pl.BlockSpec((tm, d), row), pl.BlockSpec((1, d), const),
                  pl.BlockSpec((d, IN_W_PAD), const)] + [pl.BlockSpec((tm, LANES), tab)] * 6,
        out_specs=[pl.BlockSpec((tm, w), row) for w in widths],
        out_shape=[jax.ShapeDtypeStruct((m, w), dt) for w, dt in zip(widths, dtypes)],
        compiler_params=_cparams(("arbitrary",)),
        name="inproj",
    )(x2d, gain, w_bf, *tabs)


def _rope_tables(pos):
    half = ROT_DIM // 2
    inv = ROPE_THETA ** (-2.0 * jnp.arange(half, dtype=F32) / ROT_DIM)
    ang = pos.astype(F32)[:, None] * inv[None, :]
    cos, sin = jnp.cos(ang), jnp.sin(ang)
    p = pos.shape[0]
    one = jnp.ones((p, HEAD_DIM - ROT_DIM), F32)
    zero = jnp.zeros((p, HEAD_DIM - ROT_DIM), F32)
    zh = jnp.zeros((p, half), F32)
    c64 = jnp.concatenate([cos, cos, one], axis=1)
    s1_64 = jnp.concatenate([-sin, zh, zero], axis=1)
    s2_64 = jnp.concatenate([zh, sin, zero], axis=1)
    ones64 = jnp.ones((p, HEAD_DIM), F32)
    zeros64 = jnp.zeros((p, HEAD_DIM), F32)
    q = tuple(jnp.concatenate([t, t], axis=1) for t in (c64, s1_64, s2_64))
    k = (jnp.concatenate([c64, ones64], axis=1), jnp.concatenate([s1_64, zeros64], axis=1),
         jnp.concatenate([s2_64, zeros64], axis=1))
    return q + k


def _pad_w_in(w):
    d = w.shape[0]
    per_g = GQA_REP * 3
    pad = jnp.zeros((d, LANES - per_g), w.dtype)
    gates = [jnp.concatenate([w[:, COL_GL + g * per_g:COL_GL + (g + 1) * per_g], pad], axis=1)
             for g in range(NSA_KV_HEADS)]
    return jnp.concatenate([w[:, :COL_GL]] + gates, axis=1).astype(BF16)


def _lower_bound(lbl_ref, layer):
    rows = [lbl_ref[l:l + 1, :] for l in range(lbl_ref.shape[0])]
    mx = functools.reduce(jnp.maximum, rows)
    ex = [jnp.exp(r - mx) for r in rows]
    den = functools.reduce(lambda a, b: a + b, ex)
    sm = [e / den for e in ex]
    cum = sm[0]
    for l in range(1, layer + 1):
        cum = cum + sm[l]
    return cum - sm[0]


def _forget_gate(fl, lb):
    logsig = jnp.minimum(fl, 0.0) - jnp.log1p(jnp.exp(-jnp.abs(fl)))
    a = jnp.log(jnp.maximum(lb, LB_FLOOR))
    b = jnp.log1p(-lb) + logsig
    logf = jnp.maximum(a, b) + jnp.log1p(jnp.exp(-jnp.abs(a - b)))
    kh = (1.0 - lb) * _sigmoid(-fl)
    return logf, kh


def _readout(o, g, gain):
    on = o * lax.rsqrt(jnp.mean(o * o, axis=-1, keepdims=True) + EPS) * gain
    return on * (g * _sigmoid(g))


def _hgrn_prompt_body(layer, lbl_ref, q_ref, f_ref, i_ref, g_ref, gain_ref, o_ref, s_ref,
                      st_sc, k_sc, b_sc, o_sc):
    t = pl.program_id(1)
    ct = q_ref.shape[1]

    @pl.when(t == 0)
    def _():
        st_sc[...] = jnp.zeros_like(st_sc)

    lb = _lower_bound(lbl_ref, layer)
    logf, kh = _forget_gate(f_ref[0], lb)
    k_sc[...] = kh
    ri = lax.broadcasted_iota(jnp.int32, (ct, ct), 0)
    ci = lax.broadcasted_iota(jnp.int32, (ct, ct), 1)
    tri = jnp.where((ci <= ri) & (_div_pow2(ci, HG_CHUNK) == _div_pow2(ri, HG_CHUNK)), 1.0, 0.0).astype(BF16)
    hi, mid, lo = _split3(logf)
    b_sc[...] = _dot(tri, hi) + _dot(tri, mid) + _dot(tri, lo)

    row = lax.broadcasted_iota(jnp.int32, (HG_CHUNK, HG_DK), 0)

    def chunk(c, carry):
        r0 = pl.multiple_of(c * HG_CHUNK, HG_CHUNK)
        for h in range(HG_HEADS):
            hl = slice(h * HG_DK, (h + 1) * HG_DK)
            q = q_ref[0, pl.ds(r0, HG_CHUNK), hl]
            k = k_sc[pl.ds(r0, HG_CHUNK), hl]
            v = i_ref[0, pl.ds(r0, HG_CHUNK), hl]
            b = b_sc[pl.ds(r0, HG_CHUNK), hl]
            st = st_sc[h]
            o = _dot_nt((q * jnp.exp(b)).astype(BF16), st.astype(BF16))
            for s in range(HG_CHUNK):
                w = jnp.where(row >= s, jnp.exp(b - b[s:s + 1, :]), 0.0)
                d = jnp.sum(q * (k[s:s + 1, :] * w), axis=-1, keepdims=True)
                o = o + d * v[s:s + 1, :]
            o_sc[pl.ds(r0, HG_CHUNK), hl] = o
            blast = b[HG_CHUNK - 1:HG_CHUNK, :]
            kdec = k * jnp.exp(blast - b)
            st_sc[h] = st * jnp.exp(blast) + _dot_tn(v.astype(BF16), kdec.astype(BF16))
        return carry

    lax.fori_loop(0, ct // HG_CHUNK, chunk, 0)

    gain = gain_ref[...]
    for h in range(HG_HEADS):
        hl = slice(h * HG_DV, (h + 1) * HG_DV)
        o_ref[0, :, hl] = _readout(o_sc[:, hl], g_ref[0, :, hl], gain)

    @pl.when(t == pl.num_programs(1) - 1)
    def _():
        for h in range(HG_HEADS):
            s_ref[0, h] = st_sc[h].T


def _hgrn_prompt(layer, lbl, hq, fl, hi, hg, gain, ct):
    b, t, _ = hq.shape
    tile = pl.BlockSpec((1, ct, HG_W), lambda i, j: (i, j, 0))
    return pl.pallas_call(
        functools.partial(_hgrn_prompt_body, layer),
        grid=(b, t // ct),
        in_specs=[pl.BlockSpec(lbl.shape, lambda i, j: (0, 0)), tile, tile, tile, tile,
                  pl.BlockSpec((1, HG_DV), lambda i, j: (0, 0))],
        out_specs=[tile, pl.BlockSpec((1, HG_HEADS, HG_DK, HG_DV), lambda i, j: (i, 0, 0, 0))],
        out_shape=[jax.ShapeDtypeStruct((b, t, HG_W), F32),
                   jax.ShapeDtypeStruct((b, HG_HEADS, HG_DK, HG_DV), F32)],
        scratch_shapes=[pltpu.VMEM((HG_HEADS, HG_DV, HG_DK), F32), pltpu.VMEM((ct, HG_W), F32),
                        pltpu.VMEM((ct, HG_W), F32), pltpu.VMEM((ct, HG_W), F32)],
        compiler_params=_cparams(("arbitrary", "arbitrary")),
        name="hgrn_prompt",
    )(lbl, hq, fl, hi, hg, gain)


def _hgrn_sample_body(layer, lbl_ref, s0_ref, q_ref, f_ref, i_ref, g_ref, gain_ref, s_ref, o_ref):
    lb = _lower_bound(lbl_ref, layer)
    logf, kh = _forget_gate(f_ref[0], lb)
    q, v, g = q_ref[0], i_ref[0], g_ref[0]
    gain = gain_ref[...]

    def col(x):
        return jnp.broadcast_to(x, (HG_DK, HG_DK)).T

    for h in range(HG_HEADS):
        hl = slice(h * HG_DK, (h + 1) * HG_DK)
        s_new = col(jnp.exp(logf[:, hl])) * s0_ref[0, h] + col(kh[:, hl]) * v[:, hl]
        s_ref[0, h] = s_new
        o = jnp.sum(col(q[:, hl]) * s_new, axis=0, keepdims=True)
        o_ref[0, :, hl] = _readout(o, g[:, hl], gain)


def _hgrn_sample(layer, lbl, s0, hq, fl, hi, hg, gain):
    b = s0.shape[0]
    vec = pl.BlockSpec((1, 1, HG_W), lambda i: (i, 0, 0))
    st = pl.BlockSpec((1, HG_HEADS, HG_DK, HG_DV), lambda i: (i, 0, 0, 0))
    return pl.pallas_call(
        functools.partial(_hgrn_sample_body, layer),
        grid=(b,),
        in_specs=[pl.BlockSpec(lbl.shape, lambda i: (0, 0)), st, vec, vec, vec, vec,
                  pl.BlockSpec((1, HG_DV), lambda i: (0, 0))],
        out_specs=[st, vec],
        out_shape=[jax.ShapeDtypeStruct(s0.shape, F32), jax.ShapeDtypeStruct((b, 1, HG_W), F32)],
        compiler_params=_cparams(("arbitrary",)),
        name="hgrn_sample",
    )(lbl, s0, hq, fl, hi, hg, gain)


def _compress_body(n_pages, *refs):
    pt_ref = refs[0]
    del pt_ref
    page_refs = refs[1:1 + n_pages * NSA_KV_HEADS]
    w_ref, pe_ref, y_ref, const_ref = refs[1 + n_pages * NSA_KV_HEADS:]
    rows = PAGE_SIZE // CMP_STRIDE
    for g in range(NSA_KV_HEADS):
        acc = jnp.zeros((n_pages * rows, CMP_RATIO * LANES), F32)
        for c in range(CMP_STRIDE):
            xc = jnp.concatenate(
                [page_refs[p * NSA_KV_HEADS + g][pl.ds(c, rows, stride=CMP_STRIDE), :]
                 for p in range(n_pages)], axis=0)
            acc = acc + _dot(xc.astype(BF16), w_ref[c])
        y_ref[0, :, g * CMP_RATIO * LANES:(g + 1) * CMP_RATIO * LANES] = acc

    @pl.when((pl.program_id(0) == 0) & (pl.program_id(1) == 0))
    def _():
        tot = jnp.zeros((8, LANES), F32)
        for c in range(CMP_STRIDE):
            for m in range(CMP_RATIO):
                pe_row = jnp.broadcast_to(pe_ref[m, c:c + 1, :], (8, LANES))
                tot = tot + _dot(pe_row.astype(BF16), w_ref[c][:, m * LANES:(m + 1) * LANES])
        const_ref[...] = tot


def _compress(pool, layer, page_table, w_c, pe_c, n_pages):
    s, p = page_table.shape
    rows = PAGE_SIZE // CMP_STRIDE
    steps = p // n_pages

    def page_spec(k, g):
        return pl.BlockSpec((None, None, PAGE_SIZE, LANES),
                            lambda i, j, pt: (layer, pt[i * p + j * n_pages + k], 0, g))

    page_specs = [page_spec(k, g) for k in range(n_pages) for g in range(NSA_KV_HEADS)]
    grid_spec = pltpu.PrefetchScalarGridSpec(
        num_scalar_prefetch=1,
        grid=(s, steps),
        in_specs=page_specs + [
            pl.BlockSpec(w_c.shape, lambda i, j, pt: (0, 0, 0)),
            pl.BlockSpec(pe_c.shape, lambda i, j, pt: (0, 0, 0))],
        out_specs=[pl.BlockSpec((1, n_pages * rows, NSA_KV_HEADS * CMP_RATIO * LANES),
                                lambda i, j, pt: (i, j, 0)),
                   pl.BlockSpec((8, LANES), lambda i, j, pt: (0, 0))],
    )
    return pl.pallas_call(
        functools.partial(_compress_body, n_pages),
        grid_spec=grid_spec,
        out_shape=[jax.ShapeDtypeStruct((s, p * rows, NSA_KV_HEADS * CMP_RATIO * LANES), F32),
                   jax.ShapeDtypeStruct((8, LANES), F32)],
        compiler_params=_cparams(("arbitrary", "arbitrary")),
        name="compress",
    )(page_table.reshape(-1), *([pool] * (n_pages * NSA_KV_HEADS)), w_c, pe_c)


def _compress_weights(w_cmp, pe):
    w_r = w_cmp.reshape(2, CMP_RATIO, CMP_STRIDE, HEAD_DIM, HEAD_DIM)
    z = jnp.zeros((CMP_STRIDE, HEAD_DIM, HEAD_DIM), w_cmp.dtype)
    blocks = []
    for m in range(CMP_RATIO):
        top = jnp.concatenate([w_r[0, m], z], axis=2)
        bot = jnp.concatenate([z, w_r[1, m]], axis=2)
        blocks.append(jnp.concatenate([top, bot], axis=1))
    w_c = jnp.concatenate(blocks, axis=2).astype(BF16)
    pe_r = pe.reshape(2, CMP_RATIO, CMP_STRIDE, HEAD_DIM)
    pe_c = jnp.concatenate([pe_r[0], pe_r[1]], axis=-1)
    return w_c, pe_c


def _cmp_from_y(y0, y1, const):
    body = y0[:-1, :] + y1[1:, :] + const
    return jnp.concatenate([body, jnp.zeros((1, LANES), F32)], axis=0)


def _kv_operands(kv):
    lane = lax.broadcasted_iota(jnp.int32, kv.shape, 1)
    v1 = jnp.where(lane < HEAD_DIM, pltpu.roll(kv, HEAD_DIM, 1), 1.0)
    return kv.astype(BF16), v1.astype(BF16)


def _stack_heads(qblk):
    q = qblk.astype(F32)
    lane = lax.broadcasted_iota(jnp.int32, (q.shape[0], LANES), 1)
    outs = []
    for r in range(GQA_REP):
        grp = q[:, (r // 2) * LANES:(r // 2 + 1) * LANES]
        if r % 2:
            grp = pltpu.roll(grp, HEAD_DIM, 1)
        outs.append(jnp.where(lane < HEAD_DIM, grp, 0.0))
    return jnp.concatenate(outs, axis=0).astype(BF16)


def _topk_blocks(score, n_top):
    jf = lax.broadcasted_iota(jnp.int32, score.shape, 1).astype(F32)
    big = float(score.shape[1])
    sel = jnp.zeros(score.shape, F32)
    for _ in range(n_top):
        m = jnp.max(score, axis=-1, keepdims=True)
        first = jnp.min(jnp.where(score == m, jf, big), axis=-1, keepdims=True)
        pick = jf == first
        sel = jnp.where(pick & (m > 0.5 * NEG_BIG), 1.0, sel)
        score = jnp.where(pick, REMOVED, score)
    return sel


def _overlap_matrix(n_cmp_pad, n_sel_pad):
    n = lax.broadcasted_iota(jnp.int32, (n_cmp_pad, n_sel_pad), 0) * CMP_STRIDE
    j = lax.broadcasted_iota(jnp.int32, (n_cmp_pad, n_sel_pad), 1) * SEL_BLOCK
    return jnp.where((n < j + SEL_BLOCK) & (n + CMP_BLOCK > j), 1.0, 0.0).astype(BF16)


def _nsa_prompt_body(kt, qn_ref, qr_ref, gl_ref, y_ref, const_ref, kvs_ref, kvw_ref, o_ref,
                     kc_sc, vc_sc, ks_sc, vs_sc, kw_sc, vw_sc):
    qt = pl.program_id(2)
    tq = qn_ref.shape[1]
    t_len = kvs_ref.shape[1]
    n_cmp_pad = y_ref.shape[1]
    n_sel_pad = t_len // SEL_BLOCK
    rows = GQA_REP * tq

    @pl.when(qt == 0)
    def _():
        y = y_ref[0]
        kv_cmp = _cmp_from_y(y[:, 0:LANES], y[:, LANES:2 * LANES], const_ref[0:1, :])
        kc_sc[...], vc_sc[...] = _kv_operands(kv_cmp)
        ks_sc[...], vs_sc[...] = _kv_operands(kvs_ref[0])
        kw_sc[...], vw_sc[...] = _kv_operands(kvw_ref[0])

    q0 = qt * tq
    pos = q0 + lax.broadcasted_iota(jnp.int32, (tq, 1), 0)
    pos4 = jnp.concatenate([pos] * GQA_REP, axis=0)
    qc = _stack_heads(qn_ref[0])
    qr = _stack_heads(qr_ref[0])

    s = _dot_nt(qc, kc_sc[...])
    cmp_end = lax.broadcasted_iota(jnp.int32, (1, n_cmp_pad), 1) * CMP_STRIDE + (CMP_BLOCK - 1)
    vis = cmp_end <= pos4
    s = jnp.where(vis, s, NEG_BIG)
    m = jnp.max(s, axis=-1, keepdims=True)
    p = jnp.where(vis, jnp.exp(s - m), 0.0)
    p = p / jnp.maximum(jnp.sum(p, axis=-1, keepdims=True), 1e-30)
    acc_c = _dot(p.astype(BF16), vc_sc[...])
    psum = p[0:tq]
    for r in range(1, GQA_REP):
        psum = psum + p[r * tq:(r + 1) * tq]
    ov = _overlap_matrix(n_cmp_pad, n_sel_pad)
    hi, mid, lo = _split3(psum)
    imp = _dot(hi, ov) + _dot(mid, ov) + _dot(lo, ov)

    blk = lax.broadcasted_iota(jnp.int32, (tq, n_sel_pad), 1)
    cur = _div_pow2(pos, SEL_BLOCK)
    valid = blk * SEL_BLOCK <= pos
    forced = (blk == 0) | (blk == cur) | (blk == cur - 1)
    score = jnp.where(valid, jnp.where(forced, FORCED_SCORE, imp), NEG_BIG)
    sel = _topk_blocks(score, min(N_SELECT, n_sel_pad)).astype(BF16)

    n_tiles = (q0 + tq + kt - 1) // kt

    def sel_tile(i, carry):
        m_i, acc = carry
        k0 = pl.multiple_of(i * kt, kt)
        s_i = _dot_nt(qr, ks_sc[pl.ds(k0, kt), :])
        tok = k0 + lax.broadcasted_iota(jnp.int32, (1, kt), 1)
        expand = jnp.where(
            lax.broadcasted_iota(jnp.int32, (n_sel_pad, kt), 0)
            == _div_pow2(k0 + lax.broadcasted_iota(jnp.int32, (n_sel_pad, kt), 1), SEL_BLOCK),
            1.0, 0.0).astype(BF16)
        ok = (_dot(sel, expand) > 0.5) & (tok <= pos)
        ok4 = jnp.concatenate([ok] * GQA_REP, axis=0)
        s_i = jnp.where(ok4, s_i, NEG_BIG)
        m_new = jnp.maximum(m_i, jnp.max(s_i, axis=-1, keepdims=True))
        p_i = jnp.where(ok4, jnp.exp(s_i - m_new), 0.0)
        acc = acc * jnp.exp(m_i - m_new) + _dot(p_i.astype(BF16), vs_sc[pl.ds(k0, kt), :])
        return m_new, acc

    m0 = jnp.full((rows, 1), NEG_BIG, F32)
    _, acc_s = lax.fori_loop(0, n_tiles, sel_tile, (m0, jnp.zeros((rows, LANES), F32)))

    wlen = WINDOW + tq
    w0 = pl.multiple_of(jnp.maximum(q0 - WINDOW, 0), tq)
    s_w = _dot_nt(qr, kw_sc[pl.ds(w0, wlen), :])
    dist = pos4 - (w0 + lax.broadcasted_iota(jnp.int32, (1, wlen), 1))
    ok_w = (dist >= 0) & (dist < WINDOW)
    s_w = jnp.where(ok_w, s_w, NEG_BIG)
    m_w = jnp.max(s_w, axis=-1, keepdims=True)
    p_w = jnp.where(ok_w, jnp.exp(s_w - m_w), 0.0)
    acc_w = _dot(p_w.astype(BF16), vw_sc[pl.ds(w0, wlen), :])

    def norm(acc):
        return acc / jnp.maximum(acc[:, HEAD_DIM:HEAD_DIM + 1], 1e-30)

    o_c, o_s, o_w = acc_c, norm(acc_s), norm(acc_w)
    gate = _sigmoid(gl_ref[0])
    lane = lax.broadcasted_iota(jnp.int32, (tq, LANES), 1)
    heads = []
    for r in range(GQA_REP):
        rs = slice(r * tq, (r + 1) * tq)
        heads.append(gate[:, 3 * r:3 * r + 1] * o_c[rs] + gate[:, 3 * r + 1:3 * r + 2] * o_s[rs]
                     + gate[:, 3 * r + 2:3 * r + 3] * o_w[rs])
    for j in range(GQA_REP // 2):
        o_ref[0, :, j * LANES:(j + 1) * LANES] = jnp.where(
            lane < HEAD_DIM, heads[2 * j], pltpu.roll(heads[2 * j + 1], HEAD_DIM, 1))


def _nsa_prompt(qn, qr, gl, y, const, kvs, kvw, tq, kt):
    b, t, _ = kvs.shape
    n_cmp_pad = y.shape[1]
    gw = GQA_REP * HEAD_DIM
    qspec = pl.BlockSpec((1, tq, gw), lambda i, g, j: (i, j, g))
    res = lambda w: pl.BlockSpec((1, t, w), lambda i, g, j: (i, 0, g))
    return pl.pallas_call(
        functools.partial(_nsa_prompt_body, kt),
        grid=(b, NSA_KV_HEADS, t // tq),
        in_specs=[qspec, qspec, pl.BlockSpec((1, tq, LANES), lambda i, g, j: (i, j, g)),
                  pl.BlockSpec((1, n_cmp_pad, CMP_RATIO * LANES), lambda i, g, j: (i, 0, g)),
                  pl.BlockSpec((8, LANES), lambda i, g, j: (0, 0)), res(LANES), res(LANES)],
        out_specs=qspec,
        out_shape=jax.ShapeDtypeStruct((b, t, NSA_W), F32),
        scratch_shapes=[pltpu.VMEM((n_cmp_pad, LANES), BF16), pltpu.VMEM((n_cmp_pad, LANES), BF16)]
        + [pltpu.VMEM((t, LANES), BF16)] * 4,
        compiler_params=_cparams(("arbitrary", "arbitrary", "arbitrary")),
        name="nsa_prompt",
    )(qn, qr, gl, y, const, kvs, kvw)


def _nsa_sample_cmp_body(past, q_ref, y_ref, const_ref, idx_ref, oc_ref):
    n_cmp_pad = y_ref.shape[1]
    n_sel = -(-(past + 1) // SEL_BLOCK)
    n_sel_pad = idx_ref.shape[3]
    y = y_ref[0]
    ov = _overlap_matrix(n_cmp_pad, n_sel_pad)
    cmp_end = lax.broadcasted_iota(jnp.int32, (1, n_cmp_pad), 1) * CMP_STRIDE + (CMP_BLOCK - 1)
    vis = cmp_end <= past
    blk = lax.broadcasted_iota(jnp.int32, (8, n_sel_pad), 1)
    cur = past // SEL_BLOCK
    valid = (blk * SEL_BLOCK <= past) & (blk < n_sel)
    forced = (blk == 0) | (blk == cur) | (blk == cur - 1)
    jf = blk.astype(F32)
    for g in range(NSA_KV_HEADS):
        c0 = g * CMP_RATIO * LANES
        kv_cmp = _cmp_from_y(y[:, c0:c0 + LANES], y[:, c0 + LANES:c0 + 2 * LANES], const_ref[0:1, :])
        kc, vc = _kv_operands(kv_cmp)
        s = _dot_nt(q_ref[0, g], kc)
        s = jnp.where(vis, s, NEG_BIG)
        m = jnp.max(s, axis=-1, keepdims=True)
        p = jnp.where(vis, jnp.exp(s - m), 0.0)
        p = p / jnp.maximum(jnp.sum(p, axis=-1, keepdims=True), 1e-30)
        oc_ref[0, g] = _dot(p.astype(BF16), vc)
        psum = jnp.sum(p[0:GQA_REP], axis=0, keepdims=True)
        hi, mid, lo = _split3(jnp.broadcast_to(psum, (8, n_cmp_pad)))
        imp = _dot(hi, ov) + _dot(mid, ov) + _dot(lo, ov)
        score = jnp.where(valid, jnp.where(forced, FORCED_SCORE, imp), NEG_BIG)
        idx = jnp.full((8, n_sel_pad), -1.0, F32)
        for k in range(min(N_SELECT, n_sel)):
            mx = jnp.max(score, axis=-1, keepdims=True)
            first = jnp.min(jnp.where(score == mx, jf, float(n_sel_pad)), axis=-1, keepdims=True)
            idx = jnp.where((blk == k) & (mx > 0.5 * NEG_BIG), first, idx)
            score = jnp.where(jf == first, REMOVED, score)
        idx_ref[0, g] = idx.astype(jnp.int32)


def _nsa_sample_cmp(past, q8, y, const, n_sel_pad):
    b = q8.shape[0]
    n_cmp_pad = y.shape[1]
    blk4 = lambda w: pl.BlockSpec((1, NSA_KV_HEADS, 8, w), lambda i: (i, 0, 0, 0))
    return pl.pallas_call(
        functools.partial(_nsa_sample_cmp_body, past),
        grid=(b,),
        in_specs=[blk4(LANES), pl.BlockSpec((1, n_cmp_pad, y.shape[2]), lambda i: (i, 0, 0)),
                  pl.BlockSpec((8, LANES), lambda i: (0, 0))],
        out_specs=[blk4(n_sel_pad), blk4(LANES)],
        out_shape=[jax.ShapeDtypeStruct((b, NSA_KV_HEADS, 8, n_sel_pad), jnp.int32),
                   jax.ShapeDtypeStruct((b, NSA_KV_HEADS, 8, LANES), F32)],
        compiler_params=_cparams(("arbitrary",)),
        name="nsa_sample_cmp",
    )(q8, y, const)


def _nsa_sample_sel_body(past, n_top, *refs):
    idx_ref, pt_ref = refs[0], refs[1]
    del pt_ref
    blk_refs = refs[2:2 + n_top]
    q_ref, new_s_ref, new_w_ref, win_ref, oc_ref, gl_ref, o_ref = refs[2 + n_top:]
    b, g = pl.program_id(0), pl.program_id(1)
    q = q_ref[0, 0]
    n_past_blocks = past // SEL_BLOCK

    def attend(pieces):
        ms = [jnp.max(jnp.where(ok, s, NEG_BIG), axis=-1, keepdims=True) for s, ok, _ in pieces]
        m = functools.reduce(jnp.maximum, ms)
        acc = jnp.zeros((8, LANES), F32)
        for s, ok, v in pieces:
            acc = acc + _dot(jnp.where(ok, jnp.exp(s - m), 0.0).astype(BF16), v)
        return acc

    def new_token(ref, enabled):
        kop, vop = _kv_operands(jnp.broadcast_to(ref[0, 0], (8, LANES)))
        first = lax.broadcasted_iota(jnp.int32, (8, 8), 1) == 0
        return _dot_nt(q, kop), first & enabled, vop

    pieces = []
    has_new = jnp.zeros((8, 8), jnp.int32)
    for k in range(n_top):
        j = idx_ref[(b * NSA_KV_HEADS + g) * n_top + k]
        kop, vop = _kv_operands(blk_refs[k][...])
        in_past = jnp.where((j >= 0) & (j < n_past_blocks), 1, 0)
        pieces.append((_dot_nt(q, kop), (jnp.zeros((8, SEL_BLOCK), jnp.int32) + in_past) > 0, vop))
        has_new = has_new + jnp.where(j == n_past_blocks, 1, 0)
    pieces.append(new_token(new_s_ref, has_new > 0))
    acc_s = attend(pieces)

    win = win_ref[0]
    w_buf = win.shape[0]
    kw, vw = _kv_operands(win)
    dist = w_buf - lax.broadcasted_iota(jnp.int32, (8, w_buf), 1)
    ok_buf = (dist < WINDOW) & (past - dist >= 0)
    acc_w = attend([(_dot_nt(q, kw), ok_buf, vw), new_token(new_w_ref, jnp.ones((8, 8), jnp.int32) > 0)])

    def norm(acc):
        return acc / jnp.maximum(acc[:, HEAD_DIM:HEAD_DIM + 1], 1e-30)

    gate = _sigmoid(gl_ref[0, 0])
    rsel = lax.broadcasted_iota(jnp.int32, (8, LANES), 0)
    gc = jnp.zeros((8, LANES), F32)
    gs = jnp.zeros((8, LANES), F32)
    gw = jnp.zeros((8, LANES), F32)
    for r in range(GQA_REP):
        gc = jnp.where(rsel == r, gate[:, 3 * r:3 * r + 1], gc)
        gs = jnp.where(rsel == r, gate[:, 3 * r + 1:3 * r + 2], gs)
        gw = jnp.where(rsel == r, gate[:, 3 * r + 2:3 * r + 3], gw)
    o_ref[0, 0] = gc * oc_ref[0, 0] + gs * norm(acc_s) + gw * norm(acc_w)


def _nsa_sample_sel(past, layer, idx, page_table, sel_pool, q8, new_s, new_w, win, oc, gl):
    b, n_past_pages = page_table.shape
    n_top = idx.shape[0] // (b * NSA_KV_HEADS)
    n_past_blocks = past // SEL_BLOCK
    per_page = PAGE_SIZE // SEL_BLOCK

    def blk_spec(k):
        def imap(i, g, idx_ref, pt_ref):
            j = jnp.clip(idx_ref[(i * NSA_KV_HEADS + g) * n_top + k], 0, n_past_blocks - 1)
            return (layer, pt_ref[i * n_past_pages + j // per_page], j % per_page, g)
        return pl.BlockSpec((None, None, SEL_BLOCK, LANES), imap)

    v4 = lambda w: pl.BlockSpec((1, 1, 8, w), lambda i, g, a, c: (i, g, 0, 0))
    v1 = pl.BlockSpec((1, 1, 1, LANES), lambda i, g, a, c: (i, g, 0, 0))
    grid_spec = pltpu.PrefetchScalarGridSpec(
        num_scalar_prefetch=2,
        grid=(b, NSA_KV_HEADS),
        in_specs=[blk_spec(k) for k in range(n_top)] + [
            v4(LANES), v1, v1,
            pl.BlockSpec((None, 1, win.shape[2], LANES), lambda i, g, a, c: (layer, i, 0, g)),
            v4(LANES), v1],
        out_specs=v4(LANES),
    )
    return pl.pallas_call(
        functools.partial(_nsa_sample_sel_body, past, n_top),
        grid_spec=grid_spec,
        out_shape=jax.ShapeDtypeStruct((b, NSA_KV_HEADS, 8, LANES), F32),
        compiler_params=_cparams(("arbitrary", "arbitrary")),
        name="nsa_sample_sel",
    )(idx, page_table.reshape(-1), *([sel_pool] * n_top), q8, new_s, new_w, win, oc, gl)


def _win_update_body(win_ref, new_ref, o_ref):
    w_buf = win_ref.shape[1]
    o_ref[0, 0:w_buf - 1, :] = win_ref[0, 1:w_buf, :]
    o_ref[0, w_buf - 1:w_buf, :] = new_ref[0]


def _win_update(layer, win, new):
    _, b, w_buf, w = win.shape
    return pl.pallas_call(
        _win_update_body,
        grid=(b,),
        in_specs=[pl.BlockSpec((None, 1, w_buf, w), lambda i: (layer, i, 0, 0)),
                  pl.BlockSpec((1, 1, w), lambda i: (i, 0, 0))],
        out_specs=pl.BlockSpec((1, w_buf, w), lambda i: (i, 0, 0)),
        out_shape=jax.ShapeDtypeStruct((b, w_buf, w), F32),
        compiler_params=_cparams(("arbitrary",)),
        name="win_update",
    )(win, new)


def _finish_body(x_ref, ho_ref, no_ref, wo_ref, g1_ref, g2_ref, g3_ref, w1_ref, w2_ref, o_ref):
    wo = wo_ref[...]
    y = _dot(ho_ref[...].astype(BF16), wo[0:HG_W]) + _dot(no_ref[...].astype(BF16), wo[HG_W:])
    h = x_ref[...] + _rms(y, g1_ref[...])
    a = _dot(_rms(h, g2_ref[...]).astype(BF16), w1_ref[...])
    dff = a.shape[1] // 2
    u, v = a[:, :dff], a[:, dff:]
    act = (u * _sigmoid(u)) * v
    y2 = _dot(act.astype(BF16), w2_ref[...])
    o_ref[...] = h + _rms(y2, g3_ref[...])


def _finish(x2d, ho, no, wo, g1, g2, g3, w1, w2, tm):
    m, d = x2d.shape
    row = lambda w: pl.BlockSpec((tm, w), lambda i: (i, 0))
    full = lambda a: pl.BlockSpec(a.shape, lambda i: (0, 0))
    return pl.pallas_call(
        _finish_body,
        grid=(m // tm,),
        in_specs=[row(d), row(HG_W), row(NSA_W), full(wo), full(g1), full(g2), full(g3), full(w1), full(w2)],
        out_specs=row(d),
        out_shape=jax.ShapeDtypeStruct((m, d), F32),
        compiler_params=_cparams(("arbitrary",)),
        name="finish",
    )(x2d, ho, no, wo, g1, g2, g3, w1, w2)


def _tile(n, pref):
    while n % pref:
        pref //= 2
    return pref


def kernel(x_prompt, x_sample, state_hgrn, cache_cmp_kv, cache_sel_kv, cache_win_kv, page_table,
           norm_mix_pre, norm_mix_post, norm_ffn_pre, norm_ffn_post, w_in, w_out, hg_lb_logits,
           hg_out_norm, cmp_pe, w_cmp, w_ffn_in, w_ffn_out):
    bp, tp, d = x_prompt.shape
    bs, ts, _ = x_sample.shape
    assert ts == 1
    depth = w_in.shape[0]
    n_pages = page_table.shape[1]
    past = n_pages * PAGE_SIZE
    n_pool = cache_cmp_kv.shape[1]
    w_buf = cache_win_kv.shape[2]
    assert tp % PAGE_SIZE == 0 and tp >= WINDOW + PAGE_SIZE

    tm_p = _tile(bp * tp, 256)
    tabs_p = _rope_tables(jnp.arange(tp))
    tabs_s = _rope_tables(jnp.full((bs * ts,), past, jnp.int32))
    cmp_pool = cache_cmp_kv.reshape(depth, n_pool, PAGE_SIZE, KV_W)
    sel_pool = cache_sel_kv.reshape(depth, n_pool, PAGE_SIZE, KV_W)
    win_pool = cache_win_kv.reshape(depth, bs, w_buf, KV_W)
    prompt_pages = jnp.arange(bp * (tp // PAGE_SIZE), dtype=jnp.int32).reshape(bp, tp // PAGE_SIZE)
    n_sel_s = -(-(past + ts) // SEL_BLOCK)
    n_sel_pad_s = -(-n_sel_s // LANES) * LANES
    n_top_s = min(N_SELECT, n_sel_s)

    xp = x_prompt.reshape(bp * tp, d)
    xs = x_sample.reshape(bs * ts, d)
    hgp, hgs, cmpp, cmps, selp, sels, winp, wins = [], [], [], [], [], [], [], []
    for l in range(depth):
        w_in_bf = _pad_w_in(w_in[l])
        w_out_bf = w_out[l].astype(BF16)
        w1_bf = w_ffn_in[l].astype(BF16)
        w2_bf = w_ffn_out[l].astype(BF16)
        g_pre, g_post = norm_mix_pre[l][None, :], norm_mix_post[l][None, :]
        g_fpre, g_fpost = norm_ffn_pre[l][None, :], norm_ffn_post[l][None, :]
        gain = hg_out_norm[l][None, :]
        w_c, pe_c = _compress_weights(w_cmp[l], cmp_pe[l])

        hq, fl, hi, hg, qn, qr, kvc, kvs, kvw, gl = _inproj(xp, g_pre, w_in_bf, tabs_p, tm_p)
        r3 = lambda a: a.reshape(bp, tp, a.shape[-1])
        ho, s_fin = _hgrn_prompt(l, hg_lb_logits, r3(hq), r3(fl), r3(hi), r3(hg), gain, _tile(tp, 256))
        y, const = _compress(kvc.reshape(1, bp * tp // PAGE_SIZE, PAGE_SIZE, KV_W), 0, prompt_pages,
                             w_c, pe_c, _tile(tp // PAGE_SIZE, 16))
        no = _nsa_prompt(r3(qn), r3(qr), r3(gl), y, const, r3(kvs), r3(kvw), PAGE_SIZE, _tile(tp, 512))
        xp = _finish(xp, ho.reshape(bp * tp, HG_W), no.reshape(bp * tp, NSA_W), w_out_bf, g_post, g_fpre,
                     g_fpost, w1_bf, w2_bf, tm_p)
        kv5 = lambda a, t: a.reshape(-1, t, NSA_KV_HEADS, 2, HEAD_DIM)
        hgp.append(s_fin)
        cmpp.append(kv5(kvc, tp))
        selp.append(kv5(kvs, tp))
        winp.append(kv5(kvw, tp)[:, tp - min(WINDOW, tp):])

        hq, fl, hi, hg, qn, qr, kvc, kvs, kvw, gl = _inproj(xs, g_pre, w_in_bf, tabs_s, bs * ts)
        v3 = lambda a: a.reshape(bs, 1, a.shape[-1])
        s_new, ho = _hgrn_sample(l, hg_lb_logits, state_hgrn[l], v3(hq), v3(fl), v3(hi), v3(hg), gain)
        y, const = _compress(cmp_pool, l, page_table, w_c, pe_c, _tile(n_pages, 16))

        def q8(q):
            q4 = q.reshape(bs, NSA_KV_HEADS, GQA_REP, HEAD_DIM)
            return jnp.pad(q4, ((0, 0), (0, 0), (0, 8 - GQA_REP), (0, LANES - HEAD_DIM)))

        idx, oc = _nsa_sample_cmp(past, q8(qn), y, const, n_sel_pad_s)
        g4 = lambda a: a.reshape(bs, NSA_KV_HEADS, 1, LANES)
        o8 = _nsa_sample_sel(past, l, idx[:, :, 0, :n_top_s].reshape(-1), page_table, sel_pool, q8(qr),
                             g4(kvs), g4(kvw), win_pool, oc, g4(gl))
        no = o8[:, :, :GQA_REP, :HEAD_DIM].reshape(bs * ts, NSA_W)
        win_new = _win_update(l, win_pool, kvw.reshape(bs, 1, KV_W))
        xs = _finish(xs, ho.reshape(bs * ts, HG_W), no, w_out_bf, g_post, g_fpre, g_fpost, w1_bf, w2_bf,
                     bs * ts)
        hgs.append(s_new)
        cmps.append(kv5(kvc, ts))
        sels.append(kv5(kvs, ts))
        wins.append(kv5(win_new, w_buf))

    return (xp.reshape(bp, tp, d), xs.reshape(bs, ts, d), jnp.stack(hgp), jnp.stack(hgs), jnp.stack(cmpp),
            jnp.stack(cmps), jnp.stack(selp), jnp.stack(sels), jnp.stack(winp), jnp.stack(wins))
```

```python
import functools

import jax
import jax.numpy as jnp
import numpy as np
from jax import lax
from jax.experimental import pallas as pl
from jax.experimental.pallas import tpu as pltpu

F32 = jnp.float32
BF16 = jnp.bfloat16

HG_HEADS = 4
HG_DK = 128
HG_DV = 128
HG_W = HG_HEADS * HG_DV
HEAD_DIM = 64
NSA_HEADS = 8
NSA_KV_HEADS = 2
GQA_REP = NSA_HEADS // NSA_KV_HEADS
NSA_W = NSA_HEADS * HEAD_DIM
KV_W = NSA_KV_HEADS * 2 * HEAD_DIM
N_GATES = NSA_HEADS * 3
CMP_STRIDE = 16
CMP_RATIO = 2
CMP_BLOCK = CMP_STRIDE * CMP_RATIO
SEL_BLOCK = 64
N_SELECT = 16
WINDOW = 512
PAGE_SIZE = 128
ROPE_THETA = 500000.0
ROT_DIM = HEAD_DIM // 4
EPS = 1e-6
FORCED_SCORE = 1e6
NEG_BIG = -1e30
LB_FLOOR = 1e-30
REMOVED = -3e38

LANES = 128
HG_CHUNK = 16
GATE_PAD = 2 * LANES
COL_QN = 4 * HG_W
COL_KVC = COL_QN + NSA_W
COL_KVS = COL_KVC + KV_W
COL_KVW = COL_KVS + KV_W
COL_GL = COL_KVW + KV_W
IN_W_PAD = COL_GL + GATE_PAD

VMEM_LIMIT = 56 * 1024 * 1024


def _cparams(sem):
    return pltpu.CompilerParams(dimension_semantics=sem, vmem_limit_bytes=VMEM_LIMIT)


def _rms(x, g):
    return x * lax.rsqrt(jnp.mean(x * x, axis=-1, keepdims=True) + EPS) * g


def _sigmoid(x):
    return 1.0 / (1.0 + jnp.exp(-x))


def _dot(a, b):
    return jnp.dot(a, b, preferred_element_type=F32)


def _dot_nt(a, b):
    return lax.dot_general(a, b, (((1,), (1,)), ((), ())), preferred_element_type=F32)


def _dot_tn(a, b):
    return lax.dot_general(a, b, (((0,), (0,)), ((), ())), preferred_element_type=F32)


def _div_pow2(x, n):
    assert n & (n - 1) == 0
    return x >> (n.bit_length() - 1)


def _split3(x):
    hi = x.astype(BF16)
    r1 = x - hi.astype(F32)
    mid = r1.astype(BF16)
    lo = (r1 - mid.astype(F32)).astype(BF16)
    return hi, mid, lo


def _rope128(x, c, s1, s2):
    return x * c + pltpu.roll(x, LANES - ROT_DIM // 2, 1) * s1 + pltpu.roll(x, ROT_DIM // 2, 1) * s2


def _inproj_body(kv_t, x_ref, g_ref, w_ref, cq_ref, s1q_ref, s2q_ref, ck_ref, s1k_ref, s2k_ref,
                 hq_ref, fl_ref, hi_ref, hg_ref, qn_ref, qr_ref, kvc_ref, kvs_ref, kvw_ref, gl_ref):
    def put_kv(ref, j, blk):
        if kv_t:
            ref[0, j] = blk.T
        else:
            ref[:, j * LANES:(j + 1) * LANES] = blk

    hn = _rms(x_ref[...], g_ref[...])
    z = _dot(hn.astype(BF16), w_ref[...])
    hq_ref[...] = z[:, 0:HG_W]
    fl_ref[...] = z[:, HG_W:2 * HG_W]
    hi_ref[...] = z[:, 2 * HG_W:3 * HG_W]
    hg_ref[...] = z[:, 3 * HG_W:4 * HG_W]
    scale = HEAD_DIM ** -0.5
    cq, s1q, s2q = cq_ref[...], s1q_ref[...], s2q_ref[...]
    for j in range(NSA_W // LANES):
        blk = z[:, COL_QN + j * LANES:COL_QN + (j + 1) * LANES]
        qn_ref[:, j * LANES:(j + 1) * LANES] = (blk * scale).astype(BF16)
        qr_ref[:, j * LANES:(j + 1) * LANES] = (_rope128(blk, cq, s1q, s2q) * scale).astype(BF16)
    ck, s1k, s2k = ck_ref[...], s1k_ref[...], s2k_ref[...]
    for j in range(NSA_KV_HEADS):
        cols = lambda c0: z[:, c0 + j * LANES:c0 + (j + 1) * LANES]
        put_kv(kvc_ref, j, cols(COL_KVC))
        put_kv(kvs_ref, j, _rope128(cols(COL_KVS), ck, s1k, s2k))
        put_kv(kvw_ref, j, _rope128(cols(COL_KVW), ck, s1k, s2k))
    gl_ref[...] = z[:, COL_GL:COL_GL + GATE_PAD]


def _inproj(x2d, gain, w_bf, tabs, tm, kv_t):
    m, d = x2d.shape
    n_pos_blocks = tabs[0].shape[0] // tm
    row = lambda i: (i, 0)
    tab = lambda i: (i % n_pos_blocks, 0)
    const = lambda i: (0, 0)
    widths = (HG_W, HG_W, HG_W, HG_W, NSA_W, NSA_W, GATE_PAD)
    dtypes = (F32, F32, F32, F32, BF16, BF16, F32)
    specs = [pl.BlockSpec((tm, w), row) for w in widths]
    shapes = [jax.ShapeDtypeStruct((m, w), dt) for w, dt in zip(widths, dtypes)]
    if kv_t:
        t = tabs[0].shape[0]
        kv_spec = pl.BlockSpec((1, NSA_KV_HEADS, LANES, tm), lambda i: (i // n_pos_blocks, 0, 0, i % n_pos_blocks))
        kv_shape = jax.ShapeDtypeStruct((m // t, NSA_KV_HEADS, LANES, t), F32)
    else:
        kv_spec = pl.BlockSpec((tm, KV_W), row)
        kv_shape = jax.ShapeDtypeStruct((m, KV_W), F32)
    return pl.pallas_call(
        functools.partial(_inproj_body, kv_t),
        grid=(m // tm,),
        in_specs=[pl.BlockSpec((tm, d), row), pl.BlockSpec((1, d), const),
                  pl.BlockSpec((d, IN_W_PAD), const)] + [pl.BlockSpec((tm, LANES), tab)] * 6,
        out_specs=specs[:6] + [kv_spec] * 3 + specs[6:],
        out_shape=shapes[:6] + [kv_shape] * 3 + shapes[6:],
        compiler_params=_cparams(("arbitrary",)),
        name="inproj",
    )(x2d, gain, w_bf, *tabs)


def _rope_tables(pos):
    half = ROT_DIM // 2
    inv = ROPE_THETA ** (-2.0 * jnp.arange(half, dtype=F32) / ROT_DIM)
    ang = pos.astype(F32)[:, None] * inv[None, :]
    cos, sin = jnp.cos(ang), jnp.sin(ang)
    p = pos.shape[0]
    one = jnp.ones((p, HEAD_DIM - ROT_DIM), F32)
    zero = jnp.zeros((p, HEAD_DIM - ROT_DIM), F32)
    zh = jnp.zeros((p, half), F32)
    c64 = jnp.concatenate([cos, cos, one], axis=1)
    s1_64 = jnp.concatenate([-sin, zh, zero], axis=1)
    s2_64 = jnp.concatenate([zh, sin, zero], axis=1)
    ones64 = jnp.ones((p, HEAD_DIM), F32)
    zeros64 = jnp.zeros((p, HEAD_DIM), F32)
    q = tuple(jnp.concatenate([t, t], axis=1) for t in (c64, s1_64, s2_64))
    k = (jnp.concatenate([c64, ones64], axis=1), jnp.concatenate([s1_64, zeros64], axis=1),
         jnp.concatenate([s2_64, zeros64], axis=1))
    return q + k


def _pad_w_in(w):
    d = w.shape[0]
    per_g = GQA_REP * 3
    pad = jnp.zeros((d, LANES - per_g), w.dtype)
    gates = [jnp.concatenate([w[:, COL_GL + g * per_g:COL_GL + (g + 1) * per_g], pad], axis=1)
             for g in range(NSA_KV_HEADS)]
    return jnp.concatenate([w[:, :COL_GL]] + gates, axis=1).astype(BF16)


def _lower_bound(lbl_ref, layer):
    rows = [lbl_ref[l:l + 1, :] for l in range(lbl_ref.shape[0])]
    mx = functools.reduce(jnp.maximum, rows)
    ex = [jnp.exp(r - mx) for r in rows]
    den = functools.reduce(lambda a, b: a + b, ex)
    sm = [e / den for e in ex]
    cum = sm[0]
    for l in range(1, layer + 1):
        cum = cum + sm[l]
    return cum - sm[0]


def _forget_gate(fl, lb):
    logsig = jnp.minimum(fl, 0.0) - jnp.log1p(jnp.exp(-jnp.abs(fl)))
    a = jnp.log(jnp.maximum(lb, LB_FLOOR))
    b = jnp.log1p(-lb) + logsig
    logf = jnp.maximum(a, b) + jnp.log1p(jnp.exp(-jnp.abs(a - b)))
    kh = (1.0 - lb) * _sigmoid(-fl)
    return logf, kh


def _readout(o, g, gain):
    on = o * lax.rsqrt(jnp.mean(o * o, axis=-1, keepdims=True) + EPS) * gain
    return on * (g * _sigmoid(g))


def _hgrn_prompt_body(layer, lbl_ref, q_ref, f_ref, i_ref, g_ref, gain_ref, o_ref, s_ref,
                      st_sc, k_sc, b_sc, o_sc):
    t = pl.program_id(1)
    ct = q_ref.shape[1]

    @pl.when(t == 0)
    def _():
        st_sc[...] = jnp.zeros_like(st_sc)

    lb = _lower_bound(lbl_ref, layer)
    logf, kh = _forget_gate(f_ref[0], lb)
    k_sc[...] = kh
    ri = lax.broadcasted_iota(jnp.int32, (ct, ct), 0)
    ci = lax.broadcasted_iota(jnp.int32, (ct, ct), 1)
    tri = jnp.where((ci <= ri) & (_div_pow2(ci, HG_CHUNK) == _div_pow2(ri, HG_CHUNK)), 1.0, 0.0).astype(BF16)
    hi, mid, lo = _split3(logf)
    b_sc[...] = _dot(tri, hi) + _dot(tri, mid) + _dot(tri, lo)

    row = lax.broadcasted_iota(jnp.int32, (HG_CHUNK, HG_DK), 0)

    def chunk(c, carry):
        r0 = pl.multiple_of(c * HG_CHUNK, HG_CHUNK)
        for h in range(HG_HEADS):
            hl = slice(h * HG_DK, (h + 1) * HG_DK)
            q = q_ref[0, pl.ds(r0, HG_CHUNK), hl]
            k = k_sc[pl.ds(r0, HG_CHUNK), hl]
            v = i_ref[0, pl.ds(r0, HG_CHUNK), hl]
            b = b_sc[pl.ds(r0, HG_CHUNK), hl]
            st = st_sc[h]
            o = _dot_nt((q * jnp.exp(b)).astype(BF16), st.astype(BF16))
            for s in range(HG_CHUNK):
                w = jnp.where(row >= s, jnp.exp(b - b[s:s + 1, :]), 0.0)
                d = jnp.sum(q * (k[s:s + 1, :] * w), axis=-1, keepdims=True)
                o = o + d * v[s:s + 1, :]
            o_sc[pl.ds(r0, HG_CHUNK), hl] = o
            blast = b[HG_CHUNK - 1:HG_CHUNK, :]
            kdec = k * jnp.exp(blast - b)
            st_sc[h] = st * jnp.exp(blast) + _dot_tn(v.astype(BF16), kdec.astype(BF16))
        return carry

    lax.fori_loop(0, ct // HG_CHUNK, chunk, 0)

    gain = gain_ref[...]
    for h in range(HG_HEADS):
        hl = slice(h * HG_DV, (h + 1) * HG_DV)
        o_ref[0, :, hl] = _readout(o_sc[:, hl], g_ref[0, :, hl], gain)

    @pl.when(t == pl.num_programs(1) - 1)
    def _():
        for h in range(HG_HEADS):
            s_ref[0, h] = st_sc[h].T


def _hgrn_prompt(layer, lbl, hq, fl, hi, hg, gain, ct):
    b, t, _ = hq.shape
    tile = pl.BlockSpec((1, ct, HG_W), lambda i, j: (i, j, 0))
    return pl.pallas_call(
        functools.partial(_hgrn_prompt_body, layer),
        grid=(b, t // ct),
        in_specs=[pl.BlockSpec(lbl.shape, lambda i, j: (0, 0)), tile, tile, tile, tile,
                  pl.BlockSpec((1, HG_DV), lambda i, j: (0, 0))],
        out_specs=[tile, pl.BlockSpec((1, HG_HEADS, HG_DK, HG_DV), lambda i, j: (i, 0, 0, 0))],
        out_shape=[jax.ShapeDtypeStruct((b, t, HG_W), F32),
                   jax.ShapeDtypeStruct((b, HG_HEADS, HG_DK, HG_DV), F32)],
        scratch_shapes=[pltpu.VMEM((HG_HEADS, HG_DV, HG_DK), F32), pltpu.VMEM((ct, HG_W), F32),
                        pltpu.VMEM((ct, HG_W), F32), pltpu.VMEM((ct, HG_W), F32)],
        compiler_params=_cparams(("arbitrary", "arbitrary")),
        name="hgrn_prompt",
    )(lbl, hq, fl, hi, hg, gain)


def _hgrn_sample_body(layer, lbl_ref, s0_ref, q_ref, f_ref, i_ref, g_ref, gain_ref, s_ref, o_ref):
    lb = _lower_bound(lbl_ref, layer)
    logf, kh = _forget_gate(f_ref[0], lb)
    q, v, g = q_ref[0], i_ref[0], g_ref[0]
    gain = gain_ref[...]

    def col(x):
        return jnp.broadcast_to(x, (HG_DK, HG_DK)).T

    for h in range(HG_HEADS):
        hl = slice(h * HG_DK, (h + 1) * HG_DK)
        s_new = col(jnp.exp(logf[:, hl])) * s0_ref[0, h] + col(kh[:, hl]) * v[:, hl]
        s_ref[0, h] = s_new
        o = jnp.sum(col(q[:, hl]) * s_new, axis=0, keepdims=True)
        o_ref[0, :, hl] = _readout(o, g[:, hl], gain)


def _hgrn_sample(layer, lbl, s0, hq, fl, hi, hg, gain):
    b = s0.shape[0]
    vec = pl.BlockSpec((1, 1, HG_W), lambda i: (i, 0, 0))
    st = pl.BlockSpec((1, HG_HEADS, HG_DK, HG_DV), lambda i: (i, 0, 0, 0))
    return pl.pallas_call(
        functools.partial(_hgrn_sample_body, layer),
        grid=(b,),
        in_specs=[pl.BlockSpec(lbl.shape, lambda i: (0, 0)), st, vec, vec, vec, vec,
                  pl.BlockSpec((1, HG_DV), lambda i: (0, 0))],
        out_specs=[st, vec],
        out_shape=[jax.ShapeDtypeStruct(s0.shape, F32), jax.ShapeDtypeStruct((b, 1, HG_W), F32)],
        compiler_params=_cparams(("arbitrary",)),
        name="hgrn_sample",
    )(lbl, s0, hq, fl, hi, hg, gain)


def _compress_body(n_pages, *refs):
    n_in = n_pages * NSA_KV_HEADS
    page_refs = refs[1:1 + n_in]
    w_ref, pe_ref, y_ref, const_ref = refs[1 + n_in:5 + n_in]
    x_scs = refs[5 + n_in:]
    rows = PAGE_SIZE // CMP_STRIDE
    for g in range(NSA_KV_HEADS):
        for p in range(n_pages):
            x_scs[g][p * PAGE_SIZE:(p + 1) * PAGE_SIZE, :] = page_refs[p * NSA_KV_HEADS + g][...].T
        acc = jnp.zeros((n_pages * rows, CMP_RATIO * LANES), F32)
        for c in range(CMP_STRIDE):
            xc = x_scs[g][pl.ds(c, n_pages * rows, stride=CMP_STRIDE), :]
            acc = acc + _dot(xc.astype(BF16), w_ref[c])
        y_ref[0, :, g * CMP_RATIO * LANES:(g + 1) * CMP_RATIO * LANES] = acc

    @pl.when((pl.program_id(0) == 0) & (pl.program_id(1) == 0))
    def _():
        tot = jnp.zeros((8, LANES), F32)
        for c in range(CMP_STRIDE):
            for m in range(CMP_RATIO):
                pe_row = jnp.broadcast_to(pe_ref[m, c:c + 1, :], (8, LANES))
                tot = tot + _dot(pe_row.astype(BF16), w_ref[c][:, m * LANES:(m + 1) * LANES])
        const_ref[...] = tot


def _compress(pages, page_index, page_table, w_c, pe_c, n_pages):
    s, p = page_table.shape
    rows = PAGE_SIZE // CMP_STRIDE
    steps = p // n_pages

    def page_spec(k, g):
        return pl.BlockSpec((None,) * (pages.ndim - 2) + (LANES, PAGE_SIZE),
                            lambda i, j, pt: page_index(i, j * n_pages + k, g, pt))

    page_specs = [page_spec(k, g) for k in range(n_pages) for g in range(NSA_KV_HEADS)]
    grid_spec = pltpu.PrefetchScalarGridSpec(
        num_scalar_prefetch=1,
        grid=(s, steps),
        in_specs=page_specs + [
            pl.BlockSpec(w_c.shape, lambda i, j, pt: (0, 0, 0)),
            pl.BlockSpec(pe_c.shape, lambda i, j, pt: (0, 0, 0))],
        out_specs=[pl.BlockSpec((1, n_pages * rows, NSA_KV_HEADS * CMP_RATIO * LANES),
                                lambda i, j, pt: (i, j, 0)),
                   pl.BlockSpec((8, LANES), lambda i, j, pt: (0, 0))],
        scratch_shapes=[pltpu.VMEM((n_pages * PAGE_SIZE, LANES), F32)] * NSA_KV_HEADS,
    )
    return pl.pallas_call(
        functools.partial(_compress_body, n_pages),
        grid_spec=grid_spec,
        out_shape=[jax.ShapeDtypeStruct((s, p * rows, NSA_KV_HEADS * CMP_RATIO * LANES), F32),
                   jax.ShapeDtypeStruct((8, LANES), F32)],
        compiler_params=_cparams(("arbitrary", "arbitrary")),
        name="compress",
    )(page_table.reshape(-1), *([pages] * (n_pages * NSA_KV_HEADS)), w_c, pe_c)


def _compress_weights(w_cmp, pe):
    w_r = w_cmp.reshape(2, CMP_RATIO, CMP_STRIDE, HEAD_DIM, HEAD_DIM)
    z = jnp.zeros((CMP_STRIDE, HEAD_DIM, HEAD_DIM), w_cmp.dtype)
    blocks = []
    for m in range(CMP_RATIO):
        top = jnp.concatenate([w_r[0, m], z], axis=2)
        bot = jnp.concatenate([z, w_r[1, m]], axis=2)
        blocks.append(jnp.concatenate([top, bot], axis=1))
    w_c = jnp.concatenate(blocks, axis=2).astype(BF16)
    pe_r = pe.reshape(2, CMP_RATIO, CMP_STRIDE, HEAD_DIM)
    pe_c = jnp.concatenate([pe_r[0], pe_r[1]], axis=-1)
    return w_c, pe_c


def _cmp_from_y(y0, y1, const):
    body = y0[:-1, :] + y1[1:, :] + const
    return jnp.concatenate([body, jnp.zeros((1, LANES), F32)], axis=0)


def _kv_operands(kv):
    lane = lax.broadcasted_iota(jnp.int32, kv.shape, 1)
    v1 = jnp.where(lane < HEAD_DIM, pltpu.roll(kv, HEAD_DIM, 1), 1.0)
    return kv.astype(BF16), v1.astype(BF16)


def _stack_heads(qblk):
    q = qblk.astype(F32)
    lane = lax.broadcasted_iota(jnp.int32, (q.shape[0], LANES), 1)
    outs = []
    for r in range(GQA_REP):
        grp = q[:, (r // 2) * LANES:(r // 2 + 1) * LANES]
        if r % 2:
            grp = pltpu.roll(grp, HEAD_DIM, 1)
        outs.append(jnp.where(lane < HEAD_DIM, grp, 0.0))
    return jnp.concatenate(outs, axis=0).astype(BF16)


def _overlap_matrix(n_cmp_pad, n_sel_pad):
    n = lax.broadcasted_iota(jnp.int32, (n_cmp_pad, n_sel_pad), 0) * CMP_STRIDE
    j = lax.broadcasted_iota(jnp.int32, (n_cmp_pad, n_sel_pad), 1) * SEL_BLOCK
    return jnp.where((n < j + SEL_BLOCK) & (n + CMP_BLOCK > j), 1.0, 0.0).astype(BF16)


def _overlap_matrix_t(n_sel_pad, n_cmp_pad):
    j = lax.broadcasted_iota(jnp.int32, (n_sel_pad, n_cmp_pad), 0) * SEL_BLOCK
    n = lax.broadcasted_iota(jnp.int32, (n_sel_pad, n_cmp_pad), 1) * CMP_STRIDE
    return jnp.where((n < j + SEL_BLOCK) & (n + CMP_BLOCK > j), 1.0, 0.0).astype(BF16)


def _kv_operands_t(kvt):
    v1 = jnp.concatenate([kvt[HEAD_DIM:], jnp.ones((HEAD_DIM, kvt.shape[1]), F32)], axis=0)
    return kvt.astype(BF16), v1.astype(BF16)


def _topk_axis0(score, n_top):
    jf = lax.broadcasted_iota(jnp.int32, score.shape, 0).astype(F32)
    big = float(score.shape[0])
    sel = jnp.zeros(score.shape, F32)
    for _ in range(n_top):
        m = jnp.max(score, axis=0, keepdims=True)
        first = jnp.min(jnp.where(score == m, jf, big), axis=0, keepdims=True)
        pick = jf == first
        sel = jnp.where(pick & (m > 0.5 * NEG_BIG), 1.0, sel)
        score = jnp.where(pick, REMOVED, score)
    return sel


def _nsa_prompt_body(kt, qn_ref, qr_ref, gl_ref, y_ref, const_ref, kvs_ref, kvw_ref, o_ref,
                     kc_sc, vc_sc, ks_sc, vs_sc, kw_sc, vw_sc):
    qt = pl.program_id(2)
    tq = qn_ref.shape[1]
    t_len = kvs_ref.shape[3]
    n_cmp_pad = y_ref.shape[1]
    n_sel_pad = t_len // SEL_BLOCK
    rows = GQA_REP * tq
    blocks_per_tile = kt // SEL_BLOCK

    @pl.when(qt == 0)
    def _():
        y = y_ref[0]
        kv_cmp = _cmp_from_y(y[:, 0:LANES], y[:, LANES:2 * LANES], const_ref[0:1, :])
        kc_sc[...], vc_sc[...] = _kv_operands(kv_cmp)
        kop, vs_sc[...] = _kv_operands_t(kvs_ref[0, 0])
        ks_sc[0:LANES, :] = kop
        j = lax.broadcasted_iota(jnp.int32, (n_sel_pad, t_len), 0)
        tok = lax.broadcasted_iota(jnp.int32, (n_sel_pad, t_len), 1)
        ks_sc[LANES:, :] = jnp.where(j == _div_pow2(tok & (kt - 1), SEL_BLOCK), 1.0, 0.0).astype(BF16)
        kw_sc[...], vw_sc[...] = _kv_operands_t(kvw_ref[0, 0])

    q0 = qt * tq
    pos = q0 + lax.broadcasted_iota(jnp.int32, (tq, 1), 0)
    pos4 = jnp.concatenate([pos] * GQA_REP, axis=0)
    qc = _stack_heads(qn_ref[0])
    qr = _stack_heads(qr_ref[0])

    s = _dot_nt(qc, kc_sc[...])
    cmp_end = lax.broadcasted_iota(jnp.int32, (1, n_cmp_pad), 1) * CMP_STRIDE + (CMP_BLOCK - 1)
    vis = cmp_end <= pos4
    s = jnp.where(vis, s, NEG_BIG)
    m = jnp.max(s, axis=-1, keepdims=True)
    p = jnp.where(vis, jnp.exp(s - m), 0.0)
    p = p / jnp.maximum(jnp.sum(p, axis=-1, keepdims=True), 1e-30)
    acc_c = _dot(p.astype(BF16), vc_sc[...])
    psum = p[0:tq]
    for r in range(1, GQA_REP):
        psum = psum + p[r * tq:(r + 1) * tq]
    ov_t = _overlap_matrix_t(n_sel_pad, n_cmp_pad)
    hi, mid, lo = _split3(psum)
    imp_t = _dot_nt(ov_t, hi) + _dot_nt(ov_t, mid) + _dot_nt(ov_t, lo)

    blk = lax.broadcasted_iota(jnp.int32, (n_sel_pad, tq), 0)
    pos_l = q0 + lax.broadcasted_iota(jnp.int32, (1, tq), 1)
    cur = _div_pow2(pos_l, SEL_BLOCK)
    valid = blk * SEL_BLOCK <= pos_l
    forced = (blk == 0) | (blk == cur) | (blk == cur - 1)
    score = jnp.where(valid, jnp.where(forced, FORCED_SCORE, imp_t), NEG_BIG)
    sel_t = _topk_axis0(score, min(N_SELECT, n_sel_pad))
    sel_bias = jnp.where(sel_t.T > 0.5, 0.0, NEG_BIG)

    def sel_tile(k0, m_i, acc, causal):
        shift = (n_sel_pad - k0 // SEL_BLOCK) & (n_sel_pad - 1)
        bias = pltpu.roll(sel_bias, shift, 1).astype(BF16)
        lhs = jnp.concatenate([qr, jnp.concatenate([bias] * GQA_REP, axis=0)], axis=1)
        s_i = _dot(lhs, ks_sc[:, pl.ds(k0, kt)])
        if causal:
            tok = k0 + lax.broadcasted_iota(jnp.int32, (1, kt), 1)
            s_i = jnp.where(tok <= pos4, s_i, NEG_BIG)
        m_new = jnp.maximum(m_i, jnp.max(s_i, axis=-1, keepdims=True))
        p_i = jnp.exp(s_i - m_new)
        acc = acc * jnp.exp(m_i - m_new) + _dot_nt(p_i.astype(BF16), vs_sc[:, pl.ds(k0, kt)])
        return m_new, acc

    assert n_sel_pad & (n_sel_pad - 1) == 0 and blocks_per_tile <= n_sel_pad
    n_full = q0 // kt

    def past_tiles(i, c, n):
        for u in range(n):
            c = sel_tile(pl.multiple_of((i * n + u) * kt, kt), c[0], c[1], False)
        return c

    carry = (jnp.full((rows, 1), NEG_BIG, F32), jnp.zeros((rows, LANES), F32))
    carry = lax.fori_loop(0, n_full // 2, lambda i, c: past_tiles(i, c, 2), carry)
    carry = lax.fori_loop((n_full // 2) * 2, n_full, lambda i, c: past_tiles(i, c, 1), carry)
    _, acc_s = sel_tile(pl.multiple_of(n_full * kt, kt), carry[0], carry[1], True)

    wlen = WINDOW + tq
    w0 = pl.multiple_of(jnp.maximum(q0 - WINDOW, 0), tq)
    s_w = _dot(qr, kw_sc[:, pl.ds(w0, wlen)])
    dist = pos4 - (w0 + lax.broadcasted_iota(jnp.int32, (1, wlen), 1))
    ok_w = (dist >= 0) & (dist < WINDOW)
    s_w = jnp.where(ok_w, s_w, NEG_BIG)
    m_w = jnp.max(s_w, axis=-1, keepdims=True)
    p_w = jnp.where(ok_w, jnp.exp(s_w - m_w), 0.0)
    acc_w = _dot_nt(p_w.astype(BF16), vw_sc[:, pl.ds(w0, wlen)])

    def norm(acc):
        return acc / jnp.maximum(acc[:, HEAD_DIM:HEAD_DIM + 1], 1e-30)

    o_c, o_s, o_w = acc_c, norm(acc_s), norm(acc_w)
    gate = _sigmoid(gl_ref[0])
    lane = lax.broadcasted_iota(jnp.int32, (tq, LANES), 1)
    heads = []
    for r in range(GQA_REP):
        rs = slice(r * tq, (r + 1) * tq)
        heads.append(gate[:, 3 * r:3 * r + 1] * o_c[rs] + gate[:, 3 * r + 1:3 * r + 2] * o_s[rs]
                     + gate[:, 3 * r + 2:3 * r + 3] * o_w[rs])
    for j in range(GQA_REP // 2):
        o_ref[0, :, j * LANES:(j + 1) * LANES] = jnp.where(
            lane < HEAD_DIM, heads[2 * j], pltpu.roll(heads[2 * j + 1], HEAD_DIM, 1))


def _nsa_prompt(qn, qr, gl, y, const, kvs_t, kvw_t, tq, kt):
    b, _, _, t = kvs_t.shape
    n_cmp_pad = y.shape[1]
    n_sel_pad = t // SEL_BLOCK
    gw = GQA_REP * HEAD_DIM
    qspec = pl.BlockSpec((1, tq, gw), lambda i, g, j: (i, j, g))
    res = pl.BlockSpec((1, 1, LANES, t), lambda i, g, j: (i, g, 0, 0))
    return pl.pallas_call(
        functools.partial(_nsa_prompt_body, kt),
        grid=(b, NSA_KV_HEADS, t // tq),
        in_specs=[qspec, qspec, pl.BlockSpec((1, tq, LANES), lambda i, g, j: (i, j, g)),
                  pl.BlockSpec((1, n_cmp_pad, CMP_RATIO * LANES), lambda i, g, j: (i, 0, g)),
                  pl.BlockSpec((8, LANES), lambda i, g, j: (0, 0)), res, res],
        out_specs=qspec,
        out_shape=jax.ShapeDtypeStruct((b, t, NSA_W), F32),
        scratch_shapes=[pltpu.VMEM((n_cmp_pad, LANES), BF16), pltpu.VMEM((n_cmp_pad, LANES), BF16),
                        pltpu.VMEM((LANES + n_sel_pad, t), BF16)] + [pltpu.VMEM((LANES, t), BF16)] * 3,
        compiler_params=_cparams(("arbitrary", "arbitrary", "arbitrary")),
        name="nsa_prompt",
    )(qn, qr, gl, y, const, kvs_t, kvw_t)


def _nsa_sample_cmp_body(past, q_ref, y_ref, const_ref, idx_ref, oc_ref):
    sb = q_ref.shape[0]
    n_cmp_pad = y_ref.shape[1]
    n_sel = -(-(past + 1) // SEL_BLOCK)
    n_sel_pad = idx_ref.shape[3]
    n_rows = sb * NSA_KV_HEADS
    ov = _overlap_matrix(n_cmp_pad, n_sel_pad)
    cmp_end = lax.broadcasted_iota(jnp.int32, (1, n_cmp_pad), 1) * CMP_STRIDE + (CMP_BLOCK - 1)
    vis = cmp_end <= past
    row = lax.broadcasted_iota(jnp.int32, (n_rows, n_cmp_pad), 0)
    psum_all = jnp.zeros((n_rows, n_cmp_pad), F32)
    for s_i in range(sb):
        y = y_ref[s_i]
        for g in range(NSA_KV_HEADS):
            c0 = g * CMP_RATIO * LANES
            kv_cmp = _cmp_from_y(y[:, c0:c0 + LANES], y[:, c0 + LANES:c0 + 2 * LANES], const_ref[0:1, :])
            kc, vc = _kv_operands(kv_cmp)
            s = _dot_nt(q_ref[s_i, g], kc)
            s = jnp.where(vis, s, NEG_BIG)
            m = jnp.max(s, axis=-1, keepdims=True)
            p = jnp.where(vis, jnp.exp(s - m), 0.0)
            p = p / jnp.maximum(jnp.sum(p, axis=-1, keepdims=True), 1e-30)
            oc_ref[s_i, g] = _dot(p.astype(BF16), vc)
            psum = jnp.sum(p[0:GQA_REP], axis=0, keepdims=True)
            psum_all = jnp.where(row == s_i * NSA_KV_HEADS + g, psum, psum_all)
    hi, mid, lo = _split3(psum_all)
    imp = _dot(hi, ov) + _dot(mid, ov) + _dot(lo, ov)
    blk = lax.broadcasted_iota(jnp.int32, (n_rows, n_sel_pad), 1)
    cur = past // SEL_BLOCK
    valid = (blk * SEL_BLOCK <= past) & (blk < n_sel)
    forced = (blk == 0) | (blk == cur) | (blk == cur - 1)
    jf = blk.astype(F32)
    score = jnp.where(valid, jnp.where(forced, FORCED_SCORE, imp), NEG_BIG)
    idx = jnp.full((n_rows, n_sel_pad), -1.0, F32)
    for k in range(min(N_SELECT, n_sel)):
        mx = jnp.max(score, axis=-1, keepdims=True)
        first = jnp.min(jnp.where(score == mx, jf, float(n_sel_pad)), axis=-1, keepdims=True)
        idx = jnp.where((blk == k) & (mx > 0.5 * NEG_BIG), first, idx)
        score = jnp.where(jf == first, REMOVED, score)
    idx = idx.astype(jnp.int32)
    for s_i in range(sb):
        for g in range(NSA_KV_HEADS):
            r = s_i * NSA_KV_HEADS + g
            idx_ref[s_i, g] = jnp.broadcast_to(idx[r:r + 1, :], (8, n_sel_pad))


def _nsa_sample_cmp(past, q8, y, const, n_sel_pad, sb):
    b = q8.shape[0]
    n_cmp_pad = y.shape[1]
    blk4 = lambda w: pl.BlockSpec((sb, NSA_KV_HEADS, 8, w), lambda i: (i, 0, 0, 0))
    return pl.pallas_call(
        functools.partial(_nsa_sample_cmp_body, past),
        grid=(b // sb,),
        in_specs=[blk4(LANES), pl.BlockSpec((sb, n_cmp_pad, y.shape[2]), lambda i: (i, 0, 0)),
                  pl.BlockSpec((8, LANES), lambda i: (0, 0))],
        out_specs=[blk4(n_sel_pad), blk4(LANES)],
        out_shape=[jax.ShapeDtypeStruct((b, NSA_KV_HEADS, 8, n_sel_pad), jnp.int32),
                   jax.ShapeDtypeStruct((b, NSA_KV_HEADS, 8, LANES), F32)],
        compiler_params=_cparams(("arbitrary",)),
        name="nsa_sample_cmp",
    )(q8, y, const)


def _nsa_sample_sel_body(past, n_top, *refs):
    idx_ref, pt_ref = refs[0], refs[1]
    del pt_ref
    blk_refs = refs[2:2 + n_top]
    q_ref, new_s_ref, new_w_ref, win_ref, oc_ref, gl_ref, o_ref = refs[2 + n_top:]
    b, g = pl.program_id(0), pl.program_id(1)
    q = q_ref[0, 0]
    n_past_blocks = past // SEL_BLOCK

    def attend(pieces):
        ms = [jnp.max(jnp.where(ok, s, NEG_BIG), axis=-1, keepdims=True) for s, ok, _, _ in pieces]
        m = functools.reduce(jnp.maximum, ms)
        acc = jnp.zeros((8, LANES), F32)
        for s, ok, v, v_t in pieces:
            p = jnp.where(ok, jnp.exp(s - m), 0.0).astype(BF16)
            acc = acc + (_dot_nt(p, v) if v_t else _dot(p, v))
        return acc

    def new_token(ref, enabled):
        kop, vop = _kv_operands(jnp.broadcast_to(ref[0, 0], (8, LANES)))
        first = lax.broadcasted_iota(jnp.int32, (8, 8), 1) == 0
        return _dot_nt(q, kop), first & enabled, vop, False

    per_page = PAGE_SIZE // SEL_BLOCK
    tok = lax.broadcasted_iota(jnp.int32, (8, PAGE_SIZE), 1)
    pieces = []
    has_new = jnp.zeros((8, 8), jnp.int32)
    for k in range(n_top):
        j = idx_ref[(b * NSA_KV_HEADS + g) * n_top + k]
        kop, vop = _kv_operands_t(blk_refs[k][...])
        lo = jnp.where((j >= 0) & (j < n_past_blocks), (j % per_page) * SEL_BLOCK, PAGE_SIZE)
        pieces.append((_dot(q, kop), (tok >= lo) & (tok < lo + SEL_BLOCK), vop, True))
        has_new = has_new + jnp.where(j == n_past_blocks, 1, 0)
    pieces.append(new_token(new_s_ref, has_new > 0))
    acc_s = attend(pieces)

    win = win_ref[0]
    w_buf = win.shape[1]
    kw, vw = _kv_operands_t(win)
    dist = w_buf - lax.broadcasted_iota(jnp.int32, (8, w_buf), 1)
    ok_buf = (dist < WINDOW) & (past - dist >= 0)
    acc_w = attend([(_dot(q, kw), ok_buf, vw, True), new_token(new_w_ref, jnp.ones((8, 8), jnp.int32) > 0)])

    def norm(acc):
        return acc / jnp.maximum(acc[:, HEAD_DIM:HEAD_DIM + 1], 1e-30)

    gate = _sigmoid(gl_ref[0, 0])
    rsel = lax.broadcasted_iota(jnp.int32, (8, LANES), 0)
    gc = jnp.zeros((8, LANES), F32)
    gs = jnp.zeros((8, LANES), F32)
    gw = jnp.zeros((8, LANES), F32)
    for r in range(GQA_REP):
        gc = jnp.where(rsel == r, gate[:, 3 * r:3 * r + 1], gc)
        gs = jnp.where(rsel == r, gate[:, 3 * r + 1:3 * r + 2], gs)
        gw = jnp.where(rsel == r, gate[:, 3 * r + 2:3 * r + 3], gw)
    o_ref[0, 0] = gc * oc_ref[0, 0] + gs * norm(acc_s) + gw * norm(acc_w)


def _nsa_sample_sel(past, layer, idx, page_table, sel_pool, q8, new_s, new_w, win, oc, gl):
    b, n_past_pages = page_table.shape
    n_top = idx.shape[0] // (b * NSA_KV_HEADS)
    n_past_blocks = past // SEL_BLOCK
    per_page = PAGE_SIZE // SEL_BLOCK

    def blk_spec(k):
        def imap(i, g, idx_ref, pt_ref):
            j = jnp.clip(idx_ref[(i * NSA_KV_HEADS + g) * n_top + k], 0, n_past_blocks - 1)
            return (layer, pt_ref[i * n_past_pages + j // per_page], g, 0, 0)
        return pl.BlockSpec((None, None, None, LANES, PAGE_SIZE), imap)

    v4 = lambda w: pl.BlockSpec((1, 1, 8, w), lambda i, g, a, c: (i, g, 0, 0))
    v1 = pl.BlockSpec((1, 1, 1, LANES), lambda i, g, a, c: (i, g, 0, 0))
    grid_spec = pltpu.PrefetchScalarGridSpec(
        num_scalar_prefetch=2,
        grid=(b, NSA_KV_HEADS),
        in_specs=[blk_spec(k) for k in range(n_top)] + [
            v4(LANES), v1, v1,
            pl.BlockSpec((None, None, 1, LANES, win.shape[4]), lambda i, g, a, c: (layer, i, g, 0, 0)),
            v4(LANES), v1],
        out_specs=v4(LANES),
    )
    return pl.pallas_call(
        functools.partial(_nsa_sample_sel_body, past, n_top),
        grid_spec=grid_spec,
        out_shape=jax.ShapeDtypeStruct((b, NSA_KV_HEADS, 8, LANES), F32),
        compiler_params=_cparams(("arbitrary", "arbitrary")),
        name="nsa_sample_sel",
    )(idx, page_table.reshape(-1), *([sel_pool] * n_top), q8, new_s, new_w, win, oc, gl)


def _win_update_body(win_ref, new_ref, o_ref):
    w_buf = win_ref.shape[3]
    last = lax.broadcasted_iota(jnp.int32, (LANES, LANES), 1) == LANES - 1
    for g in range(NSA_KV_HEADS):
        shifted = pltpu.roll(win_ref[0, g], w_buf - 1, 1)
        new_col = jnp.broadcast_to(new_ref[0, g], (LANES, LANES)).T
        o_ref[0, g, :, 0:w_buf - LANES] = shifted[:, 0:w_buf - LANES]
        o_ref[0, g, :, w_buf - LANES:] = jnp.where(last, new_col, shifted[:, w_buf - LANES:])


def _win_update(layer, win, new):
    _, b, g, r, w_buf = win.shape
    return pl.pallas_call(
        _win_update_body,
        grid=(b,),
        in_specs=[pl.BlockSpec((None, 1, g, r, w_buf), lambda i: (layer, i, 0, 0, 0)),
                  pl.BlockSpec((1, g, 1, r), lambda i: (i, 0, 0, 0))],
        out_specs=pl.BlockSpec((1, g, r, w_buf), lambda i: (i, 0, 0, 0)),
        out_shape=jax.ShapeDtypeStruct((b, g, r, w_buf), F32),
        compiler_params=_cparams(("arbitrary",)),
        name="win_update",
    )(win, new)


def _finish_body(x_ref, ho_ref, no_ref, wo_ref, g1_ref, g2_ref, g3_ref, w1_ref, w2_ref, o_ref):
    wo = wo_ref[...]
    y = _dot(ho_ref[...].astype(BF16), wo[0:HG_W]) + _dot(no_ref[...].astype(BF16), wo[HG_W:])
    h = x_ref[...] + _rms(y, g1_ref[...])
    a = _dot(_rms(h, g2_ref[...]).astype(BF16), w1_ref[...])
    dff = a.shape[1] // 2
    u, v = a[:, :dff], a[:, dff:]
    act = (u * _sigmoid(u)) * v
    y2 = _dot(act.astype(BF16), w2_ref[...])
    o_ref[...] = h + _rms(y2, g3_ref[...])


def _finish(x2d, ho, no, wo, g1, g2, g3, w1, w2, tm):
    m, d = x2d.shape
    row = lambda w: pl.BlockSpec((tm, w), lambda i: (i, 0))
    full = lambda a: pl.BlockSpec(a.shape, lambda i: (0, 0))
    return pl.pallas_call(
        _finish_body,
        grid=(m // tm,),
        in_specs=[row(d), row(HG_W), row(NSA_W), full(wo), full(g1), full(g2), full(g3), full(w1), full(w2)],
        out_specs=row(d),
        out_shape=jax.ShapeDtypeStruct((m, d), F32),
        compiler_params=_cparams(("arbitrary",)),
        name="finish",
    )(x2d, ho, no, wo, g1, g2, g3, w1, w2)


def _tile(n, pref):
    while n % pref:
        pref //= 2
    return pref


def kernel(x_prompt, x_sample, state_hgrn, cache_cmp_kv, cache_sel_kv, cache_win_kv, page_table,
           norm_mix_pre, norm_mix_post, norm_ffn_pre, norm_ffn_post, w_in, w_out, hg_lb_logits,
           hg_out_norm, cmp_pe, w_cmp, w_ffn_in, w_ffn_out):
    bp, tp, d = x_prompt.shape
    bs, ts, _ = x_sample.shape
    assert ts == 1
    depth = w_in.shape[0]
    n_pages = page_table.shape[1]
    past = n_pages * PAGE_SIZE
    assert tp % PAGE_SIZE == 0 and tp >= WINDOW + PAGE_SIZE

    def token_minor(cache):
        return cache.transpose(0, 1, 3, 4, 5, 2).reshape(cache.shape[0], cache.shape[1], NSA_KV_HEADS,
                                                        2 * HEAD_DIM, cache.shape[2])

    def token_major(a):
        return a.reshape(a.shape[0], NSA_KV_HEADS, 2, HEAD_DIM, a.shape[3]).transpose(0, 4, 1, 2, 3)

    tm_p = _tile(tp, 256)
    tabs_p = _rope_tables(jnp.arange(tp))
    tabs_s = _rope_tables(jnp.full((bs * ts,), past, jnp.int32))
    cmp_pool, sel_pool, win_pool = token_minor(cache_cmp_kv), token_minor(cache_sel_kv), token_minor(cache_win_kv)
    prompt_pages = jnp.zeros((bp, tp // PAGE_SIZE), jnp.int32)
    n_sel_s = -(-(past + ts) // SEL_BLOCK)
    n_sel_pad_s = -(-n_sel_s // LANES) * LANES
    n_top_s = min(N_SELECT, n_sel_s)

    xp = x_prompt.reshape(bp * tp, d)
    xs = x_sample.reshape(bs * ts, d)
    hgp, hgs, cmpp, cmps, selp, sels, winp, wins = [], [], [], [], [], [], [], []
    for l in range(depth):
        w_in_bf = _pad_w_in(w_in[l])
        w_out_bf = w_out[l].astype(BF16)
        w1_bf = w_ffn_in[l].astype(BF16)
        w2_bf = w_ffn_out[l].astype(BF16)
        g_pre, g_post = norm_mix_pre[l][None, :], norm_mix_post[l][None, :]
        g_fpre, g_fpost = norm_ffn_pre[l][None, :], norm_ffn_post[l][None, :]
        gain = hg_out_norm[l][None, :]
        w_c, pe_c = _compress_weights(w_cmp[l], cmp_pe[l])

        hq, fl, hi, hg, qn, qr, kvc_t, kvs_t, kvw_t, gl = _inproj(xp, g_pre, w_in_bf, tabs_p, tm_p, True)
        r3 = lambda a: a.reshape(bp, tp, a.shape[-1])
        ho, s_fin = _hgrn_prompt(l, hg_lb_logits, r3(hq), r3(fl), r3(hi), r3(hg), gain, _tile(tp, 256))
        y, const = _compress(kvc_t, lambda i, pg, g, pt: (i, g, 0, pg), prompt_pages, w_c, pe_c,
                             _tile(tp // PAGE_SIZE, 16))
        no = _nsa_prompt(r3(qn), r3(qr), r3(gl), y, const, kvs_t, kvw_t, PAGE_SIZE, _tile(tp, 512))
        xp = _finish(xp, ho.reshape(bp * tp, HG_W), no.reshape(bp * tp, NSA_W), w_out_bf, g_post, g_fpre,
                     g_fpost, w1_bf, w2_bf, tm_p)
        hgp.append(s_fin)
        cmpp.append(token_major(kvc_t))
        selp.append(token_major(kvs_t))
        winp.append(token_major(kvw_t[..., tp - min(WINDOW, tp):]))

        hq, fl, hi, hg, qn, qr, kvc, kvs, kvw, gl = _inproj(xs, g_pre, w_in_bf, tabs_s, bs * ts, False)
        v3 = lambda a: a.reshape(bs, 1, a.shape[-1])
        s_new, ho = _hgrn_sample(l, hg_lb_logits, state_hgrn[l], v3(hq), v3(fl), v3(hi), v3(hg), gain)
        y, const = _compress(cmp_pool, lambda i, pg, g, pt, l=l: (l, pt[i * n_pages + pg], g, 0, 0),
                             page_table, w_c, pe_c, _tile(n_pages, 16))

        def q8(q):
            q4 = q.reshape(bs, NSA_KV_HEADS, GQA_REP, HEAD_DIM)
            return jnp.pad(q4, ((0, 0), (0, 0), (0, 8 - GQA_REP), (0, LANES - HEAD_DIM)))

        idx, oc = _nsa_sample_cmp(past, q8(qn), y, const, n_sel_pad_s, _tile(bs, 4))
        g4 = lambda a: a.reshape(bs, NSA_KV_HEADS, 1, LANES)
        o8 = _nsa_sample_sel(past, l, idx[:, :, 0, :n_top_s].reshape(-1), page_table, sel_pool, q8(qr),
                             g4(kvs), g4(kvw), win_pool, oc, g4(gl))
        no = o8[:, :, :GQA_REP, :HEAD_DIM].reshape(bs * ts, NSA_W)
        win_new = _win_update(l, win_pool, g4(kvw))
        xs = _finish(xs, ho.reshape(bs * ts, HG_W), no, w_out_bf, g_post, g_fpre, g_fpost, w1_bf, w2_bf,
                     bs * ts)
        kv5 = lambda a, t: a.reshape(-1, t, NSA_KV_HEADS, 2, HEAD_DIM)
        hgs.append(s_new)
        cmps.append(kv5(kvc, ts))
        sels.append(kv5(kvs, ts))
        wins.append(token_major(win_new))

    return (xp.reshape(bp, tp, d), xs.reshape(bs, ts, d), jnp.stack(hgp), jnp.stack(hgs), jnp.stack(cmpp),
            jnp.stack(cmps), jnp.stack(selp), jnp.stack(sels), jnp.stack(winp), jnp.stack(wins))
```

```python
import functools

import jax
import jax.numpy as jnp
import numpy as np
from jax import lax
from jax.experimental import pallas as pl
from jax.experimental.pallas import tpu as pltpu

F32 = jnp.float32
BF16 = jnp.bfloat16

HG_HEADS = 4
HG_DK = 128
HG_DV = 128
HG_W = HG_HEADS * HG_DV
HEAD_DIM = 64
NSA_HEADS = 8
NSA_KV_HEADS = 2
GQA_REP = NSA_HEADS // NSA_KV_HEADS
NSA_W = NSA_HEADS * HEAD_DIM
KV_W = NSA_KV_HEADS * 2 * HEAD_DIM
N_GATES = NSA_HEADS * 3
CMP_STRIDE = 16
CMP_RATIO = 2
CMP_BLOCK = CMP_STRIDE * CMP_RATIO
SEL_BLOCK = 64
N_SELECT = 16
WINDOW = 512
PAGE_SIZE = 128
ROPE_THETA = 500000.0
ROT_DIM = HEAD_DIM // 4
EPS = 1e-6
FORCED_SCORE = 1e6
NEG_BIG = -1e30
LB_FLOOR = 1e-30
REMOVED = -3e38
LOG2_E = 1.4426950408889634

LANES = 128
HG_CHUNK = 16
GATE_PAD = 2 * LANES
CMP_PAGES_PER_STEP = 32
SEL_KV_TILE = 1024
COL_QN = 4 * HG_W
COL_KVC = COL_QN + NSA_W
COL_KVS = COL_KVC + KV_W
COL_KVW = COL_KVS + KV_W
COL_GL = COL_KVW + KV_W
IN_W_PAD = COL_GL + GATE_PAD

VMEM_LIMIT = 56 * 1024 * 1024


def _cparams(sem):
    return pltpu.CompilerParams(dimension_semantics=sem, vmem_limit_bytes=VMEM_LIMIT)


def _rms(x, g):
    return x * lax.rsqrt(jnp.mean(x * x, axis=-1, keepdims=True) + EPS) * g


def _sigmoid(x):
    return 1.0 / (1.0 + jnp.exp(-x))


def _dot(a, b):
    return jnp.dot(a, b, preferred_element_type=F32)


def _dot_nt(a, b):
    return lax.dot_general(a, b, (((1,), (1,)), ((), ())), preferred_element_type=F32)


def _dot_tn(a, b):
    return lax.dot_general(a, b, (((0,), (0,)), ((), ())), preferred_element_type=F32)


def _div_pow2(x, n):
    assert n & (n - 1) == 0
    return x >> (n.bit_length() - 1)


def _split3(x):
    hi = x.astype(BF16)
    r1 = x - hi.astype(F32)
    mid = r1.astype(BF16)
    lo = (r1 - mid.astype(F32)).astype(BF16)
    return hi, mid, lo


def _rope128(x, c, s1, s2):
    return x * c + pltpu.roll(x, LANES - ROT_DIM // 2, 1) * s1 + pltpu.roll(x, ROT_DIM // 2, 1) * s2


def _inproj_body(kv_t, x_ref, g_ref, w_ref, cq_ref, s1q_ref, s2q_ref, ck_ref, s1k_ref, s2k_ref,
                 hq_ref, fl_ref, hi_ref, hg_ref, qn_ref, qr_ref, kvc_ref, kvs_ref, kvw_ref, gl_ref):
    def put_kv(ref, j, blk):
        if kv_t:
            ref[0, j] = blk.T
        else:
            ref[:, j * LANES:(j + 1) * LANES] = blk

    hn = _rms(x_ref[...], g_ref[...])
    z = _dot(hn.astype(BF16), w_ref[...])
    hq_ref[...] = z[:, 0:HG_W]
    fl_ref[...] = z[:, HG_W:2 * HG_W]
    hi_ref[...] = z[:, 2 * HG_W:3 * HG_W]
    hg_ref[...] = z[:, 3 * HG_W:4 * HG_W]
    scale = HEAD_DIM ** -0.5 * LOG2_E
    cq, s1q, s2q = cq_ref[...], s1q_ref[...], s2q_ref[...]
    for j in range(NSA_W // LANES):
        blk = z[:, COL_QN + j * LANES:COL_QN + (j + 1) * LANES]
        qn_ref[:, j * LANES:(j + 1) * LANES] = (blk * scale).astype(BF16)
        qr_ref[:, j * LANES:(j + 1) * LANES] = (_rope128(blk, cq, s1q, s2q) * scale).astype(BF16)
    ck, s1k, s2k = ck_ref[...], s1k_ref[...], s2k_ref[...]
    for j in range(NSA_KV_HEADS):
        cols = lambda c0: z[:, c0 + j * LANES:c0 + (j + 1) * LANES]
        put_kv(kvc_ref, j, cols(COL_KVC))
        put_kv(kvs_ref, j, _rope128(cols(COL_KVS), ck, s1k, s2k))
        put_kv(kvw_ref, j, _rope128(cols(COL_KVW), ck, s1k, s2k))
    gl_ref[...] = z[:, COL_GL:COL_GL + GATE_PAD]


def _inproj(x2d, gain, w_bf, tabs, tm, kv_t):
    m, d = x2d.shape
    n_pos_blocks = tabs[0].shape[0] // tm
    row = lambda i: (i, 0)
    tab = lambda i: (i % n_pos_blocks, 0)
    const = lambda i: (0, 0)
    widths = (HG_W, HG_W, HG_W, HG_W, NSA_W, NSA_W, GATE_PAD)
    dtypes = (F32, F32, F32, F32, BF16, BF16, F32)
    specs = [pl.BlockSpec((tm, w), row) for w in widths]
    shapes = [jax.ShapeDtypeStruct((m, w), dt) for w, dt in zip(widths, dtypes)]
    if kv_t:
        t = tabs[0].shape[0]
        kv_spec = pl.BlockSpec((1, NSA_KV_HEADS, LANES, tm), lambda i: (i // n_pos_blocks, 0, 0, i % n_pos_blocks))
        kv_shape = jax.ShapeDtypeStruct((m // t, NSA_KV_HEADS, LANES, t), F32)
    else:
        kv_spec = pl.BlockSpec((tm, KV_W), row)
        kv_shape = jax.ShapeDtypeStruct((m, KV_W), F32)
    return pl.pallas_call(
        functools.partial(_inproj_body, kv_t),
        grid=(m // tm,),
        in_specs=[pl.BlockSpec((tm, d), row), pl.BlockSpec((1, d), const),
                  pl.BlockSpec((d, IN_W_PAD), const)] + [pl.BlockSpec((tm, LANES), tab)] * 6,
        out_specs=specs[:6] + [kv_spec] * 3 + specs[6:],
        out_shape=shapes[:6] + [kv_shape] * 3 + shapes[6:],
        compiler_params=_cparams(("arbitrary",)),
        name="inproj",
    )(x2d, gain, w_bf, *tabs)


def _rope_tables(pos):
    half = ROT_DIM // 2
    inv = ROPE_THETA ** (-2.0 * jnp.arange(half, dtype=F32) / ROT_DIM)
    ang = pos.astype(F32)[:, None] * inv[None, :]
    cos, sin = jnp.cos(ang), jnp.sin(ang)
    p = pos.shape[0]
    one = jnp.ones((p, HEAD_DIM - ROT_DIM), F32)
    zero = jnp.zeros((p, HEAD_DIM - ROT_DIM), F32)
    zh = jnp.zeros((p, half), F32)
    c64 = jnp.concatenate([cos, cos, one], axis=1)
    s1_64 = jnp.concatenate([-sin, zh, zero], axis=1)
    s2_64 = jnp.concatenate([zh, sin, zero], axis=1)
    ones64 = jnp.ones((p, HEAD_DIM), F32)
    zeros64 = jnp.zeros((p, HEAD_DIM), F32)
    q = tuple(jnp.concatenate([t, t], axis=1) for t in (c64, s1_64, s2_64))
    k = (jnp.concatenate([c64, ones64], axis=1), jnp.concatenate([s1_64, zeros64], axis=1),
         jnp.concatenate([s2_64, zeros64], axis=1))
    return q + k


def _pad_w_in(w):
    d = w.shape[0]
    per_g = GQA_REP * 3
    pad = jnp.zeros((d, LANES - per_g), w.dtype)
    gates = [jnp.concatenate([w[:, COL_GL + g * per_g:COL_GL + (g + 1) * per_g], pad], axis=1)
             for g in range(NSA_KV_HEADS)]
    return jnp.concatenate([w[:, :COL_GL]] + gates, axis=1).astype(BF16)


def _lower_bound(lbl_ref, layer):
    rows = [lbl_ref[l:l + 1, :] for l in range(lbl_ref.shape[0])]
    mx = functools.reduce(jnp.maximum, rows)
    ex = [jnp.exp(r - mx) for r in rows]
    den = functools.reduce(lambda a, b: a + b, ex)
    sm = [e / den for e in ex]
    cum = sm[0]
    for l in range(1, layer + 1):
        cum = cum + sm[l]
    return cum - sm[0]


def _forget_gate(fl, lb):
    logsig = jnp.minimum(fl, 0.0) - jnp.log1p(jnp.exp(-jnp.abs(fl)))
    a = jnp.log(jnp.maximum(lb, LB_FLOOR))
    b = jnp.log1p(-lb) + logsig
    logf = jnp.maximum(a, b) + jnp.log1p(jnp.exp(-jnp.abs(a - b)))
    kh = (1.0 - lb) * _sigmoid(-fl)
    return logf, kh


def _readout(o, g, gain):
    on = o * lax.rsqrt(jnp.mean(o * o, axis=-1, keepdims=True) + EPS) * gain
    return on * (g * _sigmoid(g))


def _hgrn_prompt_body(layer, lbl_ref, q_ref, f_ref, i_ref, g_ref, gain_ref, o_ref, s_ref,
                      st_sc, k_sc, b_sc, o_sc):
    t = pl.program_id(1)
    ct = q_ref.shape[1]

    @pl.when(t == 0)
    def _():
        st_sc[...] = jnp.zeros_like(st_sc)

    lb = _lower_bound(lbl_ref, layer)
    logf, kh = _forget_gate(f_ref[0], lb)
    k_sc[...] = kh
    ri = lax.broadcasted_iota(jnp.int32, (ct, ct), 0)
    ci = lax.broadcasted_iota(jnp.int32, (ct, ct), 1)
    tri = jnp.where((ci <= ri) & (_div_pow2(ci, HG_CHUNK) == _div_pow2(ri, HG_CHUNK)), 1.0, 0.0).astype(BF16)
    hi, mid, lo = _split3(logf)
    b_sc[...] = _dot(tri, hi) + _dot(tri, mid) + _dot(tri, lo)

    row = lax.broadcasted_iota(jnp.int32, (HG_CHUNK, HG_DK), 0)

    def chunk(c, carry):
        r0 = pl.multiple_of(c * HG_CHUNK, HG_CHUNK)
        for h in range(HG_HEADS):
            hl = slice(h * HG_DK, (h + 1) * HG_DK)
            q = q_ref[0, pl.ds(r0, HG_CHUNK), hl]
            k = k_sc[pl.ds(r0, HG_CHUNK), hl]
            v = i_ref[0, pl.ds(r0, HG_CHUNK), hl]
            b = b_sc[pl.ds(r0, HG_CHUNK), hl]
            st = st_sc[h]
            o = _dot_nt((q * jnp.exp(b)).astype(BF16), st.astype(BF16))
            for s in range(HG_CHUNK):
                w = jnp.where(row >= s, jnp.exp(b - b[s:s + 1, :]), 0.0)
                d = jnp.sum(q * (k[s:s + 1, :] * w), axis=-1, keepdims=True)
                o = o + d * v[s:s + 1, :]
            o_sc[pl.ds(r0, HG_CHUNK), hl] = o
            blast = b[HG_CHUNK - 1:HG_CHUNK, :]
            kdec = k * jnp.exp(blast - b)
            st_sc[h] = st * jnp.exp(blast) + _dot_tn(v.astype(BF16), kdec.astype(BF16))
        return carry

    lax.fori_loop(0, ct // HG_CHUNK, chunk, 0)

    gain = gain_ref[...]
    for h in range(HG_HEADS):
        hl = slice(h * HG_DV, (h + 1) * HG_DV)
        o_ref[0, :, hl] = _readout(o_sc[:, hl], g_ref[0, :, hl], gain)

    @pl.when(t == pl.num_programs(1) - 1)
    def _():
        for h in range(HG_HEADS):
            s_ref[0, h] = st_sc[h].T


def _hgrn_prompt(layer, lbl, hq, fl, hi, hg, gain, ct):
    b, t, _ = hq.shape
    tile = pl.BlockSpec((1, ct, HG_W), lambda i, j: (i, j, 0))
    return pl.pallas_call(
        functools.partial(_hgrn_prompt_body, layer),
        grid=(b, t // ct),
        in_specs=[pl.BlockSpec(lbl.shape, lambda i, j: (0, 0)), tile, tile, tile, tile,
                  pl.BlockSpec((1, HG_DV), lambda i, j: (0, 0))],
        out_specs=[tile, pl.BlockSpec((1, HG_HEADS, HG_DK, HG_DV), lambda i, j: (i, 0, 0, 0))],
        out_shape=[jax.ShapeDtypeStruct((b, t, HG_W), F32),
                   jax.ShapeDtypeStruct((b, HG_HEADS, HG_DK, HG_DV), F32)],
        scratch_shapes=[pltpu.VMEM((HG_HEADS, HG_DV, HG_DK), F32), pltpu.VMEM((ct, HG_W), F32),
                        pltpu.VMEM((ct, HG_W), F32), pltpu.VMEM((ct, HG_W), F32)],
        compiler_params=_cparams(("arbitrary", "arbitrary")),
        name="hgrn_prompt",
    )(lbl, hq, fl, hi, hg, gain)


def _hgrn_sample_body(layer, lbl_ref, s0_ref, q_ref, f_ref, i_ref, g_ref, gain_ref, s_ref, o_ref):
    lb = _lower_bound(lbl_ref, layer)
    logf, kh = _forget_gate(f_ref[0], lb)
    q, v, g = q_ref[0], i_ref[0], g_ref[0]
    gain = gain_ref[...]

    def col(x):
        return jnp.broadcast_to(x, (HG_DK, HG_DK)).T

    for h in range(HG_HEADS):
        hl = slice(h * HG_DK, (h + 1) * HG_DK)
        s_new = col(jnp.exp(logf[:, hl])) * s0_ref[0, h] + col(kh[:, hl]) * v[:, hl]
        s_ref[0, h] = s_new
        o = jnp.sum(col(q[:, hl]) * s_new, axis=0, keepdims=True)
        o_ref[0, :, hl] = _readout(o, g[:, hl], gain)


def _hgrn_sample(layer, lbl, s0, hq, fl, hi, hg, gain):
    b = s0.shape[0]
    vec = pl.BlockSpec((1, 1, HG_W), lambda i: (i, 0, 0))
    st = pl.BlockSpec((1, HG_HEADS, HG_DK, HG_DV), lambda i: (i, 0, 0, 0))
    return pl.pallas_call(
        functools.partial(_hgrn_sample_body, layer),
        grid=(b,),
        in_specs=[pl.BlockSpec(lbl.shape, lambda i: (0, 0)), st, vec, vec, vec, vec,
                  pl.BlockSpec((1, HG_DV), lambda i: (0, 0))],
        out_specs=[st, vec],
        out_shape=[jax.ShapeDtypeStruct(s0.shape, F32), jax.ShapeDtypeStruct((b, 1, HG_W), F32)],
        compiler_params=_cparams(("arbitrary",)),
        name="hgrn_sample",
    )(lbl, s0, hq, fl, hi, hg, gain)


def _compress_body(n_pages, *refs):
    page_refs = refs[1:1 + n_pages]
    w_ref, pe_ref, y_ref, const_ref = refs[1 + n_pages:]
    rows = PAGE_SIZE // CMP_STRIDE
    tok = lax.broadcasted_iota(jnp.int32, (PAGE_SIZE, PAGE_SIZE), 0)
    col = lax.broadcasted_iota(jnp.int32, (PAGE_SIZE, PAGE_SIZE), 1)
    perm = jnp.where(tok == CMP_STRIDE * (col & (rows - 1)) + _div_pow2(col, rows), 1.0, 0.0).astype(BF16)
    for g in range(NSA_KV_HEADS):
        pages = jnp.concatenate([page_refs[p][g].astype(BF16) for p in range(n_pages)], axis=0)
        permuted = _dot(pages, perm)
        pieces = [permuted[p * LANES:(p + 1) * LANES, :].T for p in range(n_pages)]
        lhs = jnp.concatenate(
            [jnp.concatenate([pc[c * rows:(c + 1) * rows, :] for pc in pieces], axis=0)
             for c in range(CMP_STRIDE)], axis=1)
        y_ref[0, :, g * CMP_RATIO * LANES:(g + 1) * CMP_RATIO * LANES] = _dot(lhs.astype(BF16), w_ref[...])

    @pl.when((pl.program_id(0) == 0) & (pl.program_id(1) == 0))
    def _():
        tot = jnp.zeros((8, LANES), F32)
        for c in range(CMP_STRIDE):
            for m in range(CMP_RATIO):
                pe_row = jnp.broadcast_to(pe_ref[m, c:c + 1, :], (8, LANES))
                tot = tot + _dot(pe_row.astype(BF16), w_ref[c * LANES:(c + 1) * LANES, m * LANES:(m + 1) * LANES])
        const_ref[...] = tot


def _compress(pages, page_index, page_table, w_c, pe_c, n_pages):
    s, p = page_table.shape
    rows = PAGE_SIZE // CMP_STRIDE
    steps = p // n_pages

    def page_spec(k):
        return pl.BlockSpec((None,) * (pages.ndim - 3) + (NSA_KV_HEADS, LANES, PAGE_SIZE),
                            lambda i, j, pt: page_index(i, j * n_pages + k, pt))

    page_specs = [page_spec(k) for k in range(n_pages)]
    grid_spec = pltpu.PrefetchScalarGridSpec(
        num_scalar_prefetch=1,
        grid=(s, steps),
        in_specs=page_specs + [
            pl.BlockSpec(w_c.shape, lambda i, j, pt: (0, 0)),
            pl.BlockSpec(pe_c.shape, lambda i, j, pt: (0, 0, 0))],
        out_specs=[pl.BlockSpec((1, n_pages * rows, NSA_KV_HEADS * CMP_RATIO * LANES),
                                lambda i, j, pt: (i, j, 0)),
                   pl.BlockSpec((8, LANES), lambda i, j, pt: (0, 0))],
    )
    return pl.pallas_call(
        functools.partial(_compress_body, n_pages),
        grid_spec=grid_spec,
        out_shape=[jax.ShapeDtypeStruct((s, p * rows, NSA_KV_HEADS * CMP_RATIO * LANES), F32),
                   jax.ShapeDtypeStruct((8, LANES), F32)],
        compiler_params=_cparams(("arbitrary", "arbitrary")),
        name="compress",
    )(page_table.reshape(-1), *([pages] * n_pages), w_c, pe_c)


def _compress_weights(w_cmp, pe):
    w_r = w_cmp.reshape(2, CMP_RATIO, CMP_STRIDE, HEAD_DIM, HEAD_DIM)
    z = jnp.zeros((CMP_STRIDE, HEAD_DIM, HEAD_DIM), w_cmp.dtype)
    blocks = []
    for m in range(CMP_RATIO):
        top = jnp.concatenate([w_r[0, m], z], axis=2)
        bot = jnp.concatenate([z, w_r[1, m]], axis=2)
        blocks.append(jnp.concatenate([top, bot], axis=1))
    w_c = jnp.concatenate(blocks, axis=2).astype(BF16).reshape(CMP_STRIDE * LANES, CMP_RATIO * LANES)
    pe_r = pe.reshape(2, CMP_RATIO, CMP_STRIDE, HEAD_DIM)
    pe_c = jnp.concatenate([pe_r[0], pe_r[1]], axis=-1)
    return w_c, pe_c


def _cmp_from_y(y0, y1, const):
    body = y0[:-1, :] + y1[1:, :] + const
    return jnp.concatenate([body, jnp.zeros((1, LANES), F32)], axis=0)


def _kv_operands(kv):
    lane = lax.broadcasted_iota(jnp.int32, kv.shape, 1)
    v1 = jnp.where(lane < HEAD_DIM, pltpu.roll(kv, HEAD_DIM, 1), 1.0)
    return kv.astype(BF16), v1.astype(BF16)


def _stack_heads(qblk):
    q = qblk.astype(F32)
    lane = lax.broadcasted_iota(jnp.int32, (q.shape[0], LANES), 1)
    outs = []
    for r in range(GQA_REP):
        grp = q[:, (r // 2) * LANES:(r // 2 + 1) * LANES]
        if r % 2:
            grp = pltpu.roll(grp, HEAD_DIM, 1)
        outs.append(jnp.where(lane < HEAD_DIM, grp, 0.0))
    return jnp.concatenate(outs, axis=0).astype(BF16)


def _overlap_matrix(n_cmp_pad, n_sel_pad):
    n = lax.broadcasted_iota(jnp.int32, (n_cmp_pad, n_sel_pad), 0) * CMP_STRIDE
    j = lax.broadcasted_iota(jnp.int32, (n_cmp_pad, n_sel_pad), 1) * SEL_BLOCK
    return jnp.where((n < j + SEL_BLOCK) & (n + CMP_BLOCK > j), 1.0, 0.0).astype(BF16)


def _overlap_matrix_t(n_sel_pad, n_cmp_pad):
    j = lax.broadcasted_iota(jnp.int32, (n_sel_pad, n_cmp_pad), 0) * SEL_BLOCK
    n = lax.broadcasted_iota(jnp.int32, (n_sel_pad, n_cmp_pad), 1) * CMP_STRIDE
    return jnp.where((n < j + SEL_BLOCK) & (n + CMP_BLOCK > j), 1.0, 0.0).astype(BF16)


def _kv_operands_t(kvt):
    v1 = jnp.concatenate([kvt[HEAD_DIM:], jnp.ones((HEAD_DIM, kvt.shape[1]), F32)], axis=0)
    return kvt.astype(BF16), v1.astype(BF16)


def _topk_axis0(score, n_top):
    jf = lax.broadcasted_iota(jnp.int32, score.shape, 0).astype(F32)
    big = float(score.shape[0])
    sel = jnp.zeros(score.shape, F32)
    for _ in range(n_top):
        m = jnp.max(score, axis=0, keepdims=True)
        first = jnp.min(jnp.where(score == m, jf, big), axis=0, keepdims=True)
        pick = jf == first
        sel = jnp.where(pick & (m > 0.5 * NEG_BIG), 1.0, sel)
        score = jnp.where(pick, REMOVED, score)
    return sel


def _nsa_prompt_body(kt, qn_ref, qr_ref, gl_ref, y_ref, const_ref, kvs_ref, kvw_ref, o_ref,
                     kc_sc, vc_sc, ks_sc, vs_sc, kw_sc, vw_sc):
    qt = pl.program_id(2)
    tq = qn_ref.shape[1]
    t_len = kvs_ref.shape[3]
    n_cmp_pad = y_ref.shape[1]
    n_sel_pad = t_len // SEL_BLOCK
    rows = GQA_REP * tq
    blocks_per_tile = kt // SEL_BLOCK

    @pl.when(qt == 0)
    def _():
        y = y_ref[0]
        kv_cmp = _cmp_from_y(y[:, 0:LANES], y[:, LANES:2 * LANES], const_ref[0:1, :])
        kc_sc[...], vc_sc[...] = _kv_operands(kv_cmp)
        kop, vs_sc[...] = _kv_operands_t(kvs_ref[0, 0])
        ks_sc[0:LANES, :] = kop
        j = lax.broadcasted_iota(jnp.int32, (n_sel_pad, t_len), 0)
        tok = lax.broadcasted_iota(jnp.int32, (n_sel_pad, t_len), 1)
        ks_sc[LANES:, :] = jnp.where(j == _div_pow2(tok & (kt - 1), SEL_BLOCK), 1.0, 0.0).astype(BF16)
        kw_sc[...], vw_sc[...] = _kv_operands_t(kvw_ref[0, 0])

    q0 = qt * tq
    pos = q0 + lax.broadcasted_iota(jnp.int32, (tq, 1), 0)
    pos4 = jnp.concatenate([pos] * GQA_REP, axis=0)
    qc = _stack_heads(qn_ref[0])
    qr = _stack_heads(qr_ref[0])

    wlen = WINDOW + tq
    w0 = pl.multiple_of(jnp.maximum(q0 - WINDOW, 0), tq)
    s_w = _dot(qr, kw_sc[:, pl.ds(w0, wlen)])
    dist = pos4 - (w0 + lax.broadcasted_iota(jnp.int32, (1, wlen), 1))
    ok_w = (dist >= 0) & (dist < WINDOW)
    s_w = jnp.where(ok_w, s_w, NEG_BIG)
    m_w = jnp.max(s_w, axis=-1, keepdims=True)
    p_w = jnp.where(ok_w, jnp.exp2(s_w - m_w), 0.0)
    acc_w = _dot_nt(p_w.astype(BF16), vw_sc[:, pl.ds(w0, wlen)])

    s = _dot_nt(qc, kc_sc[...])
    cmp_end = lax.broadcasted_iota(jnp.int32, (1, n_cmp_pad), 1) * CMP_STRIDE + (CMP_BLOCK - 1)
    vis = cmp_end <= pos4
    s = jnp.where(vis, s, NEG_BIG)
    m = jnp.max(s, axis=-1, keepdims=True)
    p = jnp.where(vis, jnp.exp2(s - m), 0.0)
    p = p / jnp.maximum(jnp.sum(p, axis=-1, keepdims=True), 1e-30)
    acc_c = _dot(p.astype(BF16), vc_sc[...])
    psum = p[0:tq]
    for r in range(1, GQA_REP):
        psum = psum + p[r * tq:(r + 1) * tq]
    ov_t = _overlap_matrix_t(n_sel_pad, n_cmp_pad)
    hi, mid, lo = _split3(psum)
    imp_t = _dot_nt(ov_t, hi) + _dot_nt(ov_t, mid) + _dot_nt(ov_t, lo)

    blk = lax.broadcasted_iota(jnp.int32, (n_sel_pad, tq), 0)
    pos_l = q0 + lax.broadcasted_iota(jnp.int32, (1, tq), 1)
    cur = _div_pow2(pos_l, SEL_BLOCK)
    valid = blk * SEL_BLOCK <= pos_l
    forced = (blk == 0) | (blk == cur) | (blk == cur - 1)
    n_forced = 3
    score = jnp.where(valid & jnp.logical_not(forced), imp_t, jnp.where(valid, REMOVED, NEG_BIG))
    sel_t = _topk_axis0(score, min(N_SELECT, n_sel_pad) - n_forced)
    sel_t = jnp.where(valid & forced, 1.0, sel_t)
    sel_bias = jnp.where(sel_t.T > 0.5, 0.0, NEG_BIG)

    def sel_tile(k0, m_i, acc, causal):
        shift = (n_sel_pad - k0 // SEL_BLOCK) & (n_sel_pad - 1)
        bias = pltpu.roll(sel_bias, shift, 1).astype(BF16)
        lhs = jnp.concatenate([qr, jnp.concatenate([bias] * GQA_REP, axis=0)], axis=1)
        s_i = _dot(lhs, ks_sc[:, pl.ds(k0, kt)])
        if causal:
            tok = k0 + lax.broadcasted_iota(jnp.int32, (1, kt), 1)
            s_i = jnp.where(tok <= pos4, s_i, NEG_BIG)
        m_new = jnp.maximum(m_i, jnp.max(s_i, axis=-1, keepdims=True))
        p_i = jnp.exp2(s_i - m_new)
        acc = acc * jnp.exp2(m_i - m_new) + _dot_nt(p_i.astype(BF16), vs_sc[:, pl.ds(k0, kt)])
        return m_new, acc

    assert n_sel_pad & (n_sel_pad - 1) == 0 and blocks_per_tile <= n_sel_pad
    n_full = q0 // kt

    def past_tiles(i, c, n):
        for u in range(n):
            c = sel_tile(pl.multiple_of((i * n + u) * kt, kt), c[0], c[1], False)
        return c

    carry = (jnp.full((rows, 1), NEG_BIG, F32), jnp.zeros((rows, LANES), F32))
    carry = lax.fori_loop(0, n_full // 2, lambda i, c: past_tiles(i, c, 2), carry)
    carry = lax.fori_loop((n_full // 2) * 2, n_full, lambda i, c: past_tiles(i, c, 1), carry)
    _, acc_s = sel_tile(pl.multiple_of(n_full * kt, kt), carry[0], carry[1], True)

    def norm(acc):
        return acc / jnp.maximum(acc[:, HEAD_DIM:HEAD_DIM + 1], 1e-30)

    o_c, o_s, o_w = acc_c, norm(acc_s), norm(acc_w)
    gate = _sigmoid(gl_ref[0])
    lane = lax.broadcasted_iota(jnp.int32, (tq, LANES), 1)
    heads = []
    for r in range(GQA_REP):
        rs = slice(r * tq, (r + 1) * tq)
        heads.append(gate[:, 3 * r:3 * r + 1] * o_c[rs] + gate[:, 3 * r + 1:3 * r + 2] * o_s[rs]
                     + gate[:, 3 * r + 2:3 * r + 3] * o_w[rs])
    for j in range(GQA_REP // 2):
        o_ref[0, :, j * LANES:(j + 1) * LANES] = jnp.where(
            lane < HEAD_DIM, heads[2 * j], pltpu.roll(heads[2 * j + 1], HEAD_DIM, 1))


def _nsa_prompt(qn, qr, gl, y, const, kvs_t, kvw_t, tq, kt):
    b, _, _, t = kvs_t.shape
    n_cmp_pad = y.shape[1]
    n_sel_pad = t // SEL_BLOCK
    gw = GQA_REP * HEAD_DIM
    qspec = pl.BlockSpec((1, tq, gw), lambda i, g, j: (i, j, g))
    res = pl.BlockSpec((1, 1, LANES, t), lambda i, g, j: (i, g, 0, 0))
    return pl.pallas_call(
        functools.partial(_nsa_prompt_body, kt),
        grid=(b, NSA_KV_HEADS, t // tq),
        in_specs=[qspec, qspec, pl.BlockSpec((1, tq, LANES), lambda i, g, j: (i, j, g)),
                  pl.BlockSpec((1, n_cmp_pad, CMP_RATIO * LANES), lambda i, g, j: (i, 0, g)),
                  pl.BlockSpec((8, LANES), lambda i, g, j: (0, 0)), res, res],
        out_specs=qspec,
        out_shape=jax.ShapeDtypeStruct((b, t, NSA_W), F32),
        scratch_shapes=[pltpu.VMEM((n_cmp_pad, LANES), BF16), pltpu.VMEM((n_cmp_pad, LANES), BF16),
                        pltpu.VMEM((LANES + n_sel_pad, t), BF16)] + [pltpu.VMEM((LANES, t), BF16)] * 3,
        compiler_params=_cparams(("arbitrary", "arbitrary", "arbitrary")),
        name="nsa_prompt",
    )(qn, qr, gl, y, const, kvs_t, kvw_t)


def _nsa_sample_cmp_body(past, q_ref, y_ref, const_ref, idx_ref, oc_ref):
    sb = q_ref.shape[0]
    n_cmp_pad = y_ref.shape[1]
    n_sel = -(-(past + 1) // SEL_BLOCK)
    n_sel_pad = idx_ref.shape[3]
    n_rows = sb * NSA_KV_HEADS
    ov = _overlap_matrix(n_cmp_pad, n_sel_pad)
    cmp_end = lax.broadcasted_iota(jnp.int32, (1, n_cmp_pad), 1) * CMP_STRIDE + (CMP_BLOCK - 1)
    vis = cmp_end <= past
    row = lax.broadcasted_iota(jnp.int32, (n_rows, n_cmp_pad), 0)
    psum_all = jnp.zeros((n_rows, n_cmp_pad), F32)
    for s_i in range(sb):
        y = y_ref[s_i]
        for g in range(NSA_KV_HEADS):
            c0 = g * CMP_RATIO * LANES
            kv_cmp = _cmp_from_y(y[:, c0:c0 + LANES], y[:, c0 + LANES:c0 + 2 * LANES], const_ref[0:1, :])
            kc, vc = _kv_operands(kv_cmp)
            s = _dot_nt(q_ref[s_i, g], kc)
            s = jnp.where(vis, s, NEG_BIG)
            m = jnp.max(s, axis=-1, keepdims=True)
            p = jnp.where(vis, jnp.exp2(s - m), 0.0)
            p = p / jnp.maximum(jnp.sum(p, axis=-1, keepdims=True), 1e-30)
            oc_ref[s_i, g] = _dot(p.astype(BF16), vc)
            psum = jnp.sum(p[0:GQA_REP], axis=0, keepdims=True)
            psum_all = jnp.where(row == s_i * NSA_KV_HEADS + g, psum, psum_all)
    hi, mid, lo = _split3(psum_all)
    imp = _dot(hi, ov) + _dot(mid, ov) + _dot(lo, ov)
    blk = lax.broadcasted_iota(jnp.int32, (n_rows, n_sel_pad), 1)
    cur = past // SEL_BLOCK
    valid = (blk * SEL_BLOCK <= past) & (blk < n_sel)
    forced = (blk == 0) | (blk == cur) | (blk == cur - 1)
    jf = blk.astype(F32)
    score = jnp.where(valid, jnp.where(forced, FORCED_SCORE, imp), NEG_BIG)
    idx = jnp.full((n_rows, n_sel_pad), -1.0, F32)
    for k in range(min(N_SELECT, n_sel)):
        mx = jnp.max(score, axis=-1, keepdims=True)
        first = jnp.min(jnp.where(score == mx, jf, float(n_sel_pad)), axis=-1, keepdims=True)
        idx = jnp.where((blk == k) & (mx > 0.5 * NEG_BIG), first, idx)
        score = jnp.where(jf == first, REMOVED, score)
    idx = idx.astype(jnp.int32)
    for s_i in range(sb):
        for g in range(NSA_KV_HEADS):
            r = s_i * NSA_KV_HEADS + g
            idx_ref[s_i, g] = jnp.broadcast_to(idx[r:r + 1, :], (8, n_sel_pad))


def _nsa_sample_cmp(past, q8, y, const, n_sel_pad, sb):
    b = q8.shape[0]
    n_cmp_pad = y.shape[1]
    blk4 = lambda w: pl.BlockSpec((sb, NSA_KV_HEADS, 8, w), lambda i: (i, 0, 0, 0))
    return pl.pallas_call(
        functools.partial(_nsa_sample_cmp_body, past),
        grid=(b // sb,),
        in_specs=[blk4(LANES), pl.BlockSpec((sb, n_cmp_pad, y.shape[2]), lambda i: (i, 0, 0)),
                  pl.BlockSpec((8, LANES), lambda i: (0, 0))],
        out_specs=[blk4(n_sel_pad), blk4(LANES)],
        out_shape=[jax.ShapeDtypeStruct((b, NSA_KV_HEADS, 8, n_sel_pad), jnp.int32),
                   jax.ShapeDtypeStruct((b, NSA_KV_HEADS, 8, LANES), F32)],
        compiler_params=_cparams(("arbitrary",)),
        name="nsa_sample_cmp",
    )(q8, y, const)


def _nsa_sample_sel_body(past, n_top, *refs):
    idx_ref, pt_ref = refs[0], refs[1]
    del pt_ref
    blk_refs = refs[2:2 + n_top]
    q_ref, new_s_ref, new_w_ref, win_ref, oc_ref, gl_ref, o_ref = refs[2 + n_top:]
    b, g = pl.program_id(0), pl.program_id(1)
    q = q_ref[0, 0]
    n_past_blocks = past // SEL_BLOCK

    def attend(pieces):
        ms = [jnp.max(jnp.where(ok, s, NEG_BIG), axis=-1, keepdims=True) for s, ok, _, _ in pieces]
        m = functools.reduce(jnp.maximum, ms)
        acc = jnp.zeros((8, LANES), F32)
        for s, ok, v, v_t in pieces:
            p = jnp.where(ok, jnp.exp2(s - m), 0.0).astype(BF16)
            acc = acc + (_dot_nt(p, v) if v_t else _dot(p, v))
        return acc

    def new_token(ref, enabled):
        kop, vop = _kv_operands(jnp.broadcast_to(ref[0, 0], (8, LANES)))
        first = lax.broadcasted_iota(jnp.int32, (8, 8), 1) == 0
        return _dot_nt(q, kop), first & enabled, vop, False

    per_page = PAGE_SIZE // SEL_BLOCK
    tok = lax.broadcasted_iota(jnp.int32, (8, PAGE_SIZE), 1)
    pieces = []
    has_new = jnp.zeros((8, 8), jnp.int32)
    for k in range(n_top):
        j = idx_ref[(b * NSA_KV_HEADS + g) * n_top + k]
        kop, vop = _kv_operands_t(blk_refs[k][...])
        lo = jnp.where((j >= 0) & (j < n_past_blocks), (j % per_page) * SEL_BLOCK, PAGE_SIZE)
        pieces.append((_dot(q, kop), (tok >= lo) & (tok < lo + SEL_BLOCK), vop, True))
        has_new = has_new + jnp.where(j == n_past_blocks, 1, 0)
    pieces.append(new_token(new_s_ref, has_new > 0))
    acc_s = attend(pieces)

    win = win_ref[0]
    w_buf = win.shape[1]
    kw, vw = _kv_operands_t(win)
    dist = w_buf - lax.broadcasted_iota(jnp.int32, (8, w_buf), 1)
    ok_buf = (dist < WINDOW) & (past - dist >= 0)
    acc_w = attend([(_dot(q, kw), ok_buf, vw, True), new_token(new_w_ref, jnp.ones((8, 8), jnp.int32) > 0)])

    def norm(acc):
        return acc / jnp.maximum(acc[:, HEAD_DIM:HEAD_DIM + 1], 1e-30)

    gate = _sigmoid(gl_ref[0, 0])
    rsel = lax.broadcasted_iota(jnp.int32, (8, LANES), 0)
    gc = jnp.zeros((8, LANES), F32)
    gs = jnp.zeros((8, LANES), F32)
    gw = jnp.zeros((8, LANES), F32)
    for r in range(GQA_REP):
        gc = jnp.where(rsel == r, gate[:, 3 * r:3 * r + 1], gc)
        gs = jnp.where(rsel == r, gate[:, 3 * r + 1:3 * r + 2], gs)
        gw = jnp.where(rsel == r, gate[:, 3 * r + 2:3 * r + 3], gw)
    o_ref[0, 0] = gc * oc_ref[0, 0] + gs * norm(acc_s) + gw * norm(acc_w)


def _nsa_sample_sel(past, layer, idx, page_table, sel_pool, q8, new_s, new_w, win, oc, gl):
    b, n_past_pages = page_table.shape
    n_top = idx.shape[0] // (b * NSA_KV_HEADS)
    n_past_blocks = past // SEL_BLOCK
    per_page = PAGE_SIZE // SEL_BLOCK

    def blk_spec(k):
        def imap(i, g, idx_ref, pt_ref):
            j = jnp.clip(idx_ref[(i * NSA_KV_HEADS + g) * n_top + k], 0, n_past_blocks - 1)
            return (layer, pt_ref[i * n_past_pages + j // per_page], g, 0, 0)
        return pl.BlockSpec((None, None, None, LANES, PAGE_SIZE), imap)

    v4 = lambda w: pl.BlockSpec((1, 1, 8, w), lambda i, g, a, c: (i, g, 0, 0))
    v1 = pl.BlockSpec((1, 1, 1, LANES), lambda i, g, a, c: (i, g, 0, 0))
    grid_spec = pltpu.PrefetchScalarGridSpec(
        num_scalar_prefetch=2,
        grid=(b, NSA_KV_HEADS),
        in_specs=[blk_spec(k) for k in range(n_top)] + [
            v4(LANES), v1, v1,
            pl.BlockSpec((None, None, 1, LANES, win.shape[4]), lambda i, g, a, c: (layer, i, g, 0, 0)),
            v4(LANES), v1],
        out_specs=v4(LANES),
    )
    return pl.pallas_call(
        functools.partial(_nsa_sample_sel_body, past, n_top),
        grid_spec=grid_spec,
        out_shape=jax.ShapeDtypeStruct((b, NSA_KV_HEADS, 8, LANES), F32),
        compiler_params=_cparams(("arbitrary", "arbitrary")),
        name="nsa_sample_sel",
    )(idx, page_table.reshape(-1), *([sel_pool] * n_top), q8, new_s, new_w, win, oc, gl)


def _win_update_body(win_ref, new_ref, o_ref):
    w_buf = win_ref.shape[3]
    last = lax.broadcasted_iota(jnp.int32, (LANES, LANES), 1) == LANES - 1
    for g in range(NSA_KV_HEADS):
        shifted = pltpu.roll(win_ref[0, g], w_buf - 1, 1)
        new_col = jnp.broadcast_to(new_ref[0, g], (LANES, LANES)).T
        o_ref[0, g, :, 0:w_buf - LANES] = shifted[:, 0:w_buf - LANES]
        o_ref[0, g, :, w_buf - LANES:] = jnp.where(last, new_col, shifted[:, w_buf - LANES:])


def _win_update(layer, win, new):
    _, b, g, r, w_buf = win.shape
    return pl.pallas_call(
        _win_update_body,
        grid=(b,),
        in_specs=[pl.BlockSpec((None, 1, g, r, w_buf), lambda i: (layer, i, 0, 0, 0)),
                  pl.BlockSpec((1, g, 1, r), lambda i: (i, 0, 0, 0))],
        out_specs=pl.BlockSpec((1, g, r, w_buf), lambda i: (i, 0, 0, 0)),
        out_shape=jax.ShapeDtypeStruct((b, g, r, w_buf), F32),
        compiler_params=_cparams(("arbitrary",)),
        name="win_update",
    )(win, new)


def _finish_body(x_ref, ho_ref, no_ref, wo_ref, g1_ref, g2_ref, g3_ref, w1_ref, w2_ref, o_ref):
    wo = wo_ref[...]
    y = _dot(ho_ref[...].astype(BF16), wo[0:HG_W]) + _dot(no_ref[...].astype(BF16), wo[HG_W:])
    h = x_ref[...] + _rms(y, g1_ref[...])
    a = _dot(_rms(h, g2_ref[...]).astype(BF16), w1_ref[...])
    dff = a.shape[1] // 2
    u, v = a[:, :dff], a[:, dff:]
    act = (u * _sigmoid(u)) * v
    y2 = _dot(act.astype(BF16), w2_ref[...])
    o_ref[...] = h + _rms(y2, g3_ref[...])


def _finish(x2d, ho, no, wo, g1, g2, g3, w1, w2, tm):
    m, d = x2d.shape
    row = lambda w: pl.BlockSpec((tm, w), lambda i: (i, 0))
    full = lambda a: pl.BlockSpec(a.shape, lambda i: (0, 0))
    return pl.pallas_call(
        _finish_body,
        grid=(m // tm,),
        in_specs=[row(d), row(HG_W), row(NSA_W), full(wo), full(g1), full(g2), full(g3), full(w1), full(w2)],
        out_specs=row(d),
        out_shape=jax.ShapeDtypeStruct((m, d), F32),
        compiler_params=_cparams(("arbitrary",)),
        name="finish",
    )(x2d, ho, no, wo, g1, g2, g3, w1, w2)


def _tile(n, pref):
    while n % pref:
        pref //= 2
    return pref


def kernel(x_prompt, x_sample, state_hgrn, cache_cmp_kv, cache_sel_kv, cache_win_kv, page_table,
           norm_mix_pre, norm_mix_post, norm_ffn_pre, norm_ffn_post, w_in, w_out, hg_lb_logits,
           hg_out_norm, cmp_pe, w_cmp, w_ffn_in, w_ffn_out):
    bp, tp, d = x_prompt.shape
    bs, ts, _ = x_sample.shape
    assert ts == 1
    depth = w_in.shape[0]
    n_pages = page_table.shape[1]
    past = n_pages * PAGE_SIZE
    assert tp % PAGE_SIZE == 0 and tp >= WINDOW + PAGE_SIZE

    def token_minor(cache):
        return cache.transpose(0, 1, 3, 4, 5, 2).reshape(cache.shape[0], cache.shape[1], NSA_KV_HEADS,
                                                        2 * HEAD_DIM, cache.shape[2])

    def token_major(a):
        return a.reshape(a.shape[0], NSA_KV_HEADS, 2, HEAD_DIM, a.shape[3]).transpose(0, 4, 1, 2, 3)

    tm_p = _tile(tp, 256)
    tabs_p = _rope_tables(jnp.arange(tp))
    tabs_s = _rope_tables(jnp.full((bs * ts,), past, jnp.int32))
    cmp_pool, sel_pool, win_pool = token_minor(cache_cmp_kv), token_minor(cache_sel_kv), token_minor(cache_win_kv)
    prompt_pages = jnp.zeros((bp, tp // PAGE_SIZE), jnp.int32)
    n_sel_s = -(-(past + ts) // SEL_BLOCK)
    n_sel_pad_s = -(-n_sel_s // LANES) * LANES
    n_top_s = min(N_SELECT, n_sel_s)

    xp = x_prompt.reshape(bp * tp, d)
    xs = x_sample.reshape(bs * ts, d)
    hgp, hgs, cmpp, cmps, selp, sels, winp, wins = [], [], [], [], [], [], [], []
    for l in range(depth):
        w_in_bf = _pad_w_in(w_in[l])
        w_out_bf = w_out[l].astype(BF16)
        w1_bf = w_ffn_in[l].astype(BF16)
        w2_bf = w_ffn_out[l].astype(BF16)
        g_pre, g_post = norm_mix_pre[l][None, :], norm_mix_post[l][None, :]
        g_fpre, g_fpost = norm_ffn_pre[l][None, :], norm_ffn_post[l][None, :]
        gain = hg_out_norm[l][None, :]
        w_c, pe_c = _compress_weights(w_cmp[l], cmp_pe[l])

        hq, fl, hi, hg, qn, qr, kvc_t, kvs_t, kvw_t, gl = _inproj(xp, g_pre, w_in_bf, tabs_p, tm_p, True)
        r3 = lambda a: a.reshape(bp, tp, a.shape[-1])
        ho, s_fin = _hgrn_prompt(l, hg_lb_logits, r3(hq), r3(fl), r3(hi), r3(hg), gain, _tile(tp, 256))
        y, const = _compress(kvc_t, lambda i, pg, pt: (i, 0, 0, pg), prompt_pages, w_c, pe_c,
                             _tile(tp // PAGE_SIZE, CMP_PAGES_PER_STEP))
        no = _nsa_prompt(r3(qn), r3(qr), r3(gl), y, const, kvs_t, kvw_t, PAGE_SIZE, _tile(tp, SEL_KV_TILE))
        xp = _finish(xp, ho.reshape(bp * tp, HG_W), no.reshape(bp * tp, NSA_W), w_out_bf, g_post, g_fpre,
                     g_fpost, w1_bf, w2_bf, tm_p)
        hgp.append(s_fin)
        cmpp.append(token_major(kvc_t))
        selp.append(token_major(kvs_t))
        winp.append(token_major(kvw_t[..., tp - min(WINDOW, tp):]))

        hq, fl, hi, hg, qn, qr, kvc, kvs, kvw, gl = _inproj(xs, g_pre, w_in_bf, tabs_s, bs * ts, False)
        v3 = lambda a: a.reshape(bs, 1, a.shape[-1])
        s_new, ho = _hgrn_sample(l, hg_lb_logits, state_hgrn[l], v3(hq), v3(fl), v3(hi), v3(hg), gain)
        y, const = _compress(cmp_pool, lambda i, pg, pt, l=l: (l, pt[i * n_pages + pg], 0, 0, 0),
                             page_table, w_c, pe_c, _tile(n_pages, CMP_PAGES_PER_STEP))

        def q8(q):
            q4 = q.reshape(bs, NSA_KV_HEADS, GQA_REP, HEAD_DIM)
            return jnp.pad(q4, ((0, 0), (0, 0), (0, 8 - GQA_REP), (0, LANES - HEAD_DIM)))

        idx, oc = _nsa_sample_cmp(past, q8(qn), y, const, n_sel_pad_s, _tile(bs, 4))
        g4 = lambda a: a.reshape(bs, NSA_KV_HEADS, 1, LANES)
        o8 = _nsa_sample_sel(past, l, idx[:, :, 0, :n_top_s].reshape(-1), page_table, sel_pool, q8(qr),
                             g4(kvs), g4(kvw), win_pool, oc, g4(gl))
        no = o8[:, :, :GQA_REP, :HEAD_DIM].reshape(bs * ts, NSA_W)
        win_new = _win_update(l, win_pool, g4(kvw))
        xs = _finish(xs, ho.reshape(bs * ts, HG_W), no, w_out_bf, g_post, g_fpre, g_fpost, w1_bf, w2_bf,
                     bs * ts)
        kv5 = lambda a, t: a.reshape(-1, t, NSA_KV_HEADS, 2, HEAD_DIM)
        hgs.append(s_new)
        cmps.append(kv5(kvc, ts))
        sels.append(kv5(kvs, ts))
        wins.append(token_major(win_new))

    return (xp.reshape(bp, tp, d), xs.reshape(bs, ts, d), jnp.stack(hgp), jnp.stack(hgs), jnp.stack(cmpp),
            jnp.stack(cmps), jnp.stack(selp), jnp.stack(sels), jnp.stack(winp), jnp.stack(wins))
```

```python
import functools

import jax
import jax.numpy as jnp
import numpy as np
from jax import lax
from jax.experimental import pallas as pl
from jax.experimental.pallas import tpu as pltpu

F32 = jnp.float32
BF16 = jnp.bfloat16

HG_HEADS = 4
HG_DK = 128
HG_DV = 128
HG_W = HG_HEADS * HG_DV
HEAD_DIM = 64
NSA_HEADS = 8
NSA_KV_HEADS = 2
GQA_REP = NSA_HEADS // NSA_KV_HEADS
NSA_W = NSA_HEADS * HEAD_DIM
KV_W = NSA_KV_HEADS * 2 * HEAD_DIM
N_GATES = NSA_HEADS * 3
CMP_STRIDE = 16
CMP_RATIO = 2
CMP_BLOCK = CMP_STRIDE * CMP_RATIO
SEL_BLOCK = 64
N_SELECT = 16
WINDOW = 512
PAGE_SIZE = 128
ROPE_THETA = 500000.0
ROT_DIM = HEAD_DIM // 4
EPS = 1e-6
FORCED_SCORE = 1e6
NEG_BIG = -1e30
LB_FLOOR = 1e-30
REMOVED = -3e38
LOG2_E = 1.4426950408889634

LANES = 128
HG_CHUNK = 16
GATE_PAD = 2 * LANES
CMP_PAGES_PER_STEP = 32
SEL_KV_TILE = 1024
COL_QN = 4 * HG_W
COL_KVC = COL_QN + NSA_W
COL_KVS = COL_KVC + KV_W
COL_KVW = COL_KVS + KV_W
COL_GL = COL_KVW + KV_W
IN_W_PAD = COL_GL + GATE_PAD

VMEM_LIMIT = 56 * 1024 * 1024


def _cparams(sem):
    return pltpu.CompilerParams(dimension_semantics=sem, vmem_limit_bytes=VMEM_LIMIT)


def _rms(x, g):
    return x * lax.rsqrt(jnp.mean(x * x, axis=-1, keepdims=True) + EPS) * g


def _sigmoid(x):
    return 1.0 / (1.0 + jnp.exp(-x))


def _dot(a, b):
    return jnp.dot(a, b, preferred_element_type=F32)


def _dot_nt(a, b):
    return lax.dot_general(a, b, (((1,), (1,)), ((), ())), preferred_element_type=F32)


def _dot_tn(a, b):
    return lax.dot_general(a, b, (((0,), (0,)), ((), ())), preferred_element_type=F32)


def _div_pow2(x, n):
    assert n & (n - 1) == 0
    return x >> (n.bit_length() - 1)


def _split3(x):
    hi = x.astype(BF16)
    r1 = x - hi.astype(F32)
    mid = r1.astype(BF16)
    lo = (r1 - mid.astype(F32)).astype(BF16)
    return hi, mid, lo


def _rope128(x, c, s1, s2):
    return x * c + pltpu.roll(x, LANES - ROT_DIM // 2, 1) * s1 + pltpu.roll(x, ROT_DIM // 2, 1) * s2


def _inproj_body(kv_t, x_ref, g_ref, w_ref, cq_ref, s1q_ref, s2q_ref, ck_ref, s1k_ref, s2k_ref,
                 hq_ref, fl_ref, hi_ref, hg_ref, qn_ref, qr_ref, kvc_ref, kvs_ref, kvw_ref, gl_ref):
    def put_kv(ref, j, blk):
        if kv_t:
            ref[0, j] = blk.T
        else:
            ref[:, j * LANES:(j + 1) * LANES] = blk

    hn = _rms(x_ref[...], g_ref[...])
    z = _dot(hn.astype(BF16), w_ref[...])
    hq_ref[...] = z[:, 0:HG_W]
    fl_ref[...] = z[:, HG_W:2 * HG_W]
    hi_ref[...] = z[:, 2 * HG_W:3 * HG_W]
    hg_ref[...] = z[:, 3 * HG_W:4 * HG_W]
    scale = HEAD_DIM ** -0.5 * LOG2_E
    cq, s1q, s2q = cq_ref[...], s1q_ref[...], s2q_ref[...]
    for j in range(NSA_W // LANES):
        blk = z[:, COL_QN + j * LANES:COL_QN + (j + 1) * LANES]
        qn_ref[:, j * LANES:(j + 1) * LANES] = (blk * scale).astype(BF16)
        qr_ref[:, j * LANES:(j + 1) * LANES] = (_rope128(blk, cq, s1q, s2q) * scale).astype(BF16)
    ck, s1k, s2k = ck_ref[...], s1k_ref[...], s2k_ref[...]
    for j in range(NSA_KV_HEADS):
        cols = lambda c0: z[:, c0 + j * LANES:c0 + (j + 1) * LANES]
        put_kv(kvc_ref, j, cols(COL_KVC))
        put_kv(kvs_ref, j, _rope128(cols(COL_KVS), ck, s1k, s2k))
        put_kv(kvw_ref, j, _rope128(cols(COL_KVW), ck, s1k, s2k))
    gl_ref[...] = z[:, COL_GL:COL_GL + GATE_PAD]


def _inproj(x2d, gain, w_bf, tabs, tm, kv_t):
    m, d = x2d.shape
    n_pos_blocks = tabs[0].shape[0] // tm
    row = lambda i: (i, 0)
    tab = lambda i: (i % n_pos_blocks, 0)
    const = lambda i: (0, 0)
    widths = (HG_W, HG_W, HG_W, HG_W, NSA_W, NSA_W, GATE_PAD)
    dtypes = (F32, F32, F32, F32, BF16, BF16, F32)
    specs = [pl.BlockSpec((tm, w), row) for w in widths]
    shapes = [jax.ShapeDtypeStruct((m, w), dt) for w, dt in zip(widths, dtypes)]
    if kv_t:
        t = tabs[0].shape[0]
        kv_spec = pl.BlockSpec((1, NSA_KV_HEADS, LANES, tm), lambda i: (i // n_pos_blocks, 0, 0, i % n_pos_blocks))
        kv_shape = jax.ShapeDtypeStruct((m // t, NSA_KV_HEADS, LANES, t), F32)
    else:
        kv_spec = pl.BlockSpec((tm, KV_W), row)
        kv_shape = jax.ShapeDtypeStruct((m, KV_W), F32)
    return pl.pallas_call(
        functools.partial(_inproj_body, kv_t),
        grid=(m // tm,),
        in_specs=[pl.BlockSpec((tm, d), row), pl.BlockSpec((1, d), const),
                  pl.BlockSpec((d, IN_W_PAD), const, pipeline_mode=pl.Buffered(1))]
        + [pl.BlockSpec((tm, LANES), tab)] * 6,
        out_specs=specs[:6] + [kv_spec] * 3 + specs[6:],
        out_shape=shapes[:6] + [kv_shape] * 3 + shapes[6:],
        compiler_params=_cparams(("arbitrary",)),
        name="inproj",
    )(x2d, gain, w_bf, *tabs)


def _rope_tables(pos):
    half = ROT_DIM // 2
    inv = ROPE_THETA ** (-2.0 * jnp.arange(half, dtype=F32) / ROT_DIM)
    ang = pos.astype(F32)[:, None] * inv[None, :]
    cos, sin = jnp.cos(ang), jnp.sin(ang)
    p = pos.shape[0]
    one = jnp.ones((p, HEAD_DIM - ROT_DIM), F32)
    zero = jnp.zeros((p, HEAD_DIM - ROT_DIM), F32)
    zh = jnp.zeros((p, half), F32)
    c64 = jnp.concatenate([cos, cos, one], axis=1)
    s1_64 = jnp.concatenate([-sin, zh, zero], axis=1)
    s2_64 = jnp.concatenate([zh, sin, zero], axis=1)
    ones64 = jnp.ones((p, HEAD_DIM), F32)
    zeros64 = jnp.zeros((p, HEAD_DIM), F32)
    q = tuple(jnp.concatenate([t, t], axis=1) for t in (c64, s1_64, s2_64))
    k = (jnp.concatenate([c64, ones64], axis=1), jnp.concatenate([s1_64, zeros64], axis=1),
         jnp.concatenate([s2_64, zeros64], axis=1))
    return q + k


def _pad_w_in(w):
    d = w.shape[0]
    per_g = GQA_REP * 3
    pad = jnp.zeros((d, LANES - per_g), w.dtype)
    gates = [jnp.concatenate([w[:, COL_GL + g * per_g:COL_GL + (g + 1) * per_g], pad], axis=1)
             for g in range(NSA_KV_HEADS)]
    return jnp.concatenate([w[:, :COL_GL]] + gates, axis=1).astype(BF16)


def _lower_bound(lbl_ref, layer):
    rows = [lbl_ref[l:l + 1, :] for l in range(lbl_ref.shape[0])]
    mx = functools.reduce(jnp.maximum, rows)
    ex = [jnp.exp(r - mx) for r in rows]
    den = functools.reduce(lambda a, b: a + b, ex)
    sm = [e / den for e in ex]
    cum = sm[0]
    for l in range(1, layer + 1):
        cum = cum + sm[l]
    return cum - sm[0]


def _forget_gate(fl, lb):
    logsig = jnp.minimum(fl, 0.0) - jnp.log1p(jnp.exp(-jnp.abs(fl)))
    a = jnp.log(jnp.maximum(lb, LB_FLOOR))
    b = jnp.log1p(-lb) + logsig
    logf = jnp.maximum(a, b) + jnp.log1p(jnp.exp(-jnp.abs(a - b)))
    kh = (1.0 - lb) * _sigmoid(-fl)
    return logf, kh


def _readout(o, g, gain):
    on = o * lax.rsqrt(jnp.mean(o * o, axis=-1, keepdims=True) + EPS) * gain
    return on * (g * _sigmoid(g))


def _hgrn_prompt_body(layer, lbl_ref, q_ref, f_ref, i_ref, g_ref, gain_ref, o_ref, s_ref,
                      st_sc, k_sc, b_sc, o_sc):
    t = pl.program_id(1)
    ct = q_ref.shape[1]

    @pl.when(t == 0)
    def _():
        st_sc[...] = jnp.zeros_like(st_sc)

    lb = _lower_bound(lbl_ref, layer)
    logf, kh = _forget_gate(f_ref[0], lb)
    k_sc[...] = kh
    ri = lax.broadcasted_iota(jnp.int32, (ct, ct), 0)
    ci = lax.broadcasted_iota(jnp.int32, (ct, ct), 1)
    tri = jnp.where((ci <= ri) & (_div_pow2(ci, HG_CHUNK) == _div_pow2(ri, HG_CHUNK)), 1.0, 0.0).astype(BF16)
    hi, mid, lo = _split3(logf)
    b_sc[...] = _dot(tri, hi) + _dot(tri, mid) + _dot(tri, lo)

    row = lax.broadcasted_iota(jnp.int32, (HG_CHUNK, HG_DK), 0)

    def chunk(c, carry):
        r0 = pl.multiple_of(c * HG_CHUNK, HG_CHUNK)
        for h in range(HG_HEADS):
            hl = slice(h * HG_DK, (h + 1) * HG_DK)
            q = q_ref[0, pl.ds(r0, HG_CHUNK), hl]
            k = k_sc[pl.ds(r0, HG_CHUNK), hl]
            v = i_ref[0, pl.ds(r0, HG_CHUNK), hl]
            b = b_sc[pl.ds(r0, HG_CHUNK), hl]
            st = st_sc[h]
            o = _dot_nt((q * jnp.exp(b)).astype(BF16), st.astype(BF16))
            for s in range(HG_CHUNK):
                w = jnp.where(row >= s, jnp.exp(b - b[s:s + 1, :]), 0.0)
                d = jnp.sum(q * (k[s:s + 1, :] * w), axis=-1, keepdims=True)
                o = o + d * v[s:s + 1, :]
            o_sc[pl.ds(r0, HG_CHUNK), hl] = o
            blast = b[HG_CHUNK - 1:HG_CHUNK, :]
            kdec = k * jnp.exp(blast - b)
            st_sc[h] = st * jnp.exp(blast) + _dot_tn(v.astype(BF16), kdec.astype(BF16))
        return carry

    lax.fori_loop(0, ct // HG_CHUNK, chunk, 0)

    gain = gain_ref[...]
    for h in range(HG_HEADS):
        hl = slice(h * HG_DV, (h + 1) * HG_DV)
        o_ref[0, :, hl] = _readout(o_sc[:, hl], g_ref[0, :, hl], gain)

    @pl.when(t == pl.num_programs(1) - 1)
    def _():
        for h in range(HG_HEADS):
            s_ref[0, h] = st_sc[h].T


def _hgrn_prompt(layer, lbl, hq, fl, hi, hg, gain, ct):
    b, t, _ = hq.shape
    tile = pl.BlockSpec((1, ct, HG_W), lambda i, j: (i, j, 0))
    return pl.pallas_call(
        functools.partial(_hgrn_prompt_body, layer),
        grid=(b, t // ct),
        in_specs=[pl.BlockSpec(lbl.shape, lambda i, j: (0, 0)), tile, tile, tile, tile,
                  pl.BlockSpec((1, HG_DV), lambda i, j: (0, 0))],
        out_specs=[tile, pl.BlockSpec((1, HG_HEADS, HG_DK, HG_DV), lambda i, j: (i, 0, 0, 0))],
        out_shape=[jax.ShapeDtypeStruct((b, t, HG_W), F32),
                   jax.ShapeDtypeStruct((b, HG_HEADS, HG_DK, HG_DV), F32)],
        scratch_shapes=[pltpu.VMEM((HG_HEADS, HG_DV, HG_DK), F32), pltpu.VMEM((ct, HG_W), F32),
                        pltpu.VMEM((ct, HG_W), F32), pltpu.VMEM((ct, HG_W), F32)],
        compiler_params=_cparams(("arbitrary", "arbitrary")),
        name="hgrn_prompt",
    )(lbl, hq, fl, hi, hg, gain)


def _hgrn_sample_body(layer, lbl_ref, s0_ref, q_ref, f_ref, i_ref, g_ref, gain_ref, s_ref, o_ref):
    lb = _lower_bound(lbl_ref, layer)
    logf, kh = _forget_gate(f_ref[0], lb)
    q, v, g = q_ref[0], i_ref[0], g_ref[0]
    gain = gain_ref[...]

    def col(x):
        return jnp.broadcast_to(x, (HG_DK, HG_DK)).T

    for h in range(HG_HEADS):
        hl = slice(h * HG_DK, (h + 1) * HG_DK)
        s_new = col(jnp.exp(logf[:, hl])) * s0_ref[0, h] + col(kh[:, hl]) * v[:, hl]
        s_ref[0, h] = s_new
        o = jnp.sum(col(q[:, hl]) * s_new, axis=0, keepdims=True)
        o_ref[0, :, hl] = _readout(o, g[:, hl], gain)


def _hgrn_sample(layer, lbl, s0, hq, fl, hi, hg, gain):
    b = s0.shape[0]
    vec = pl.BlockSpec((1, 1, HG_W), lambda i: (i, 0, 0))
    st = pl.BlockSpec((1, HG_HEADS, HG_DK, HG_DV), lambda i: (i, 0, 0, 0))
    return pl.pallas_call(
        functools.partial(_hgrn_sample_body, layer),
        grid=(b,),
        in_specs=[pl.BlockSpec(lbl.shape, lambda i: (0, 0)), st, vec, vec, vec, vec,
                  pl.BlockSpec((1, HG_DV), lambda i: (0, 0))],
        out_specs=[st, vec],
        out_shape=[jax.ShapeDtypeStruct(s0.shape, F32), jax.ShapeDtypeStruct((b, 1, HG_W), F32)],
        compiler_params=_cparams(("arbitrary",)),
        name="hgrn_sample",
    )(lbl, s0, hq, fl, hi, hg, gain)


def _compress_body(n_pages, *refs):
    page_refs = refs[1:1 + n_pages]
    w_ref, pe_ref, y_ref, const_ref = refs[1 + n_pages:]
    rows = PAGE_SIZE // CMP_STRIDE
    tok = lax.broadcasted_iota(jnp.int32, (PAGE_SIZE, PAGE_SIZE), 0)
    col = lax.broadcasted_iota(jnp.int32, (PAGE_SIZE, PAGE_SIZE), 1)
    perm = jnp.where(tok == CMP_STRIDE * (col & (rows - 1)) + _div_pow2(col, rows), 1.0, 0.0).astype(BF16)
    for g in range(NSA_KV_HEADS):
        pages = jnp.concatenate([page_refs[p][g].astype(BF16) for p in range(n_pages)], axis=0)
        permuted = _dot(pages, perm)
        pieces = [permuted[p * LANES:(p + 1) * LANES, :].T for p in range(n_pages)]
        lhs = jnp.concatenate(
            [jnp.concatenate([pc[c * rows:(c + 1) * rows, :] for pc in pieces], axis=0)
             for c in range(CMP_STRIDE)], axis=1)
        y_ref[0, :, g * CMP_RATIO * LANES:(g + 1) * CMP_RATIO * LANES] = _dot(lhs.astype(BF16), w_ref[...])

    @pl.when((pl.program_id(0) == 0) & (pl.program_id(1) == 0))
    def _():
        tot = jnp.zeros((8, LANES), F32)
        for c in range(CMP_STRIDE):
            for m in range(CMP_RATIO):
                pe_row = jnp.broadcast_to(pe_ref[m, c:c + 1, :], (8, LANES))
                tot = tot + _dot(pe_row.astype(BF16), w_ref[c * LANES:(c + 1) * LANES, m * LANES:(m + 1) * LANES])
        const_ref[...] = tot


def _compress(pages, page_index, page_table, w_c, pe_c, n_pages):
    s, p = page_table.shape
    rows = PAGE_SIZE // CMP_STRIDE
    steps = p // n_pages

    def page_spec(k):
        return pl.BlockSpec((None,) * (pages.ndim - 3) + (NSA_KV_HEADS, LANES, PAGE_SIZE),
                            lambda i, j, pt: page_index(i, j * n_pages + k, pt))

    page_specs = [page_spec(k) for k in range(n_pages)]
    grid_spec = pltpu.PrefetchScalarGridSpec(
        num_scalar_prefetch=1,
        grid=(s, steps),
        in_specs=page_specs + [
            pl.BlockSpec(w_c.shape, lambda i, j, pt: (0, 0)),
            pl.BlockSpec(pe_c.shape, lambda i, j, pt: (0, 0, 0))],
        out_specs=[pl.BlockSpec((1, n_pages * rows, NSA_KV_HEADS * CMP_RATIO * LANES),
                                lambda i, j, pt: (i, j, 0)),
                   pl.BlockSpec((8, LANES), lambda i, j, pt: (0, 0))],
    )
    return pl.pallas_call(
        functools.partial(_compress_body, n_pages),
        grid_spec=grid_spec,
        out_shape=[jax.ShapeDtypeStruct((s, p * rows, NSA_KV_HEADS * CMP_RATIO * LANES), F32),
                   jax.ShapeDtypeStruct((8, LANES), F32)],
        compiler_params=_cparams(("arbitrary", "arbitrary")),
        name="compress",
    )(page_table.reshape(-1), *([pages] * n_pages), w_c, pe_c)


def _compress_weights(w_cmp, pe):
    w_r = w_cmp.reshape(2, CMP_RATIO, CMP_STRIDE, HEAD_DIM, HEAD_DIM)
    z = jnp.zeros((CMP_STRIDE, HEAD_DIM, HEAD_DIM), w_cmp.dtype)
    blocks = []
    for m in range(CMP_RATIO):
        top = jnp.concatenate([w_r[0, m], z], axis=2)
        bot = jnp.concatenate([z, w_r[1, m]], axis=2)
        blocks.append(jnp.concatenate([top, bot], axis=1))
    w_c = jnp.concatenate(blocks, axis=2).astype(BF16).reshape(CMP_STRIDE * LANES, CMP_RATIO * LANES)
    pe_r = pe.reshape(2, CMP_RATIO, CMP_STRIDE, HEAD_DIM)
    pe_c = jnp.concatenate([pe_r[0], pe_r[1]], axis=-1)
    return w_c, pe_c


def _cmp_from_y(y0, y1, const):
    body = y0[:-1, :] + y1[1:, :] + const
    return jnp.concatenate([body, jnp.zeros((1, LANES), F32)], axis=0)


def _kv_operands(kv):
    lane = lax.broadcasted_iota(jnp.int32, kv.shape, 1)
    v1 = jnp.where(lane < HEAD_DIM, pltpu.roll(kv, HEAD_DIM, 1), 1.0)
    return kv.astype(BF16), v1.astype(BF16)


def _stack_heads(qblk):
    q = qblk.astype(F32)
    lane = lax.broadcasted_iota(jnp.int32, (q.shape[0], LANES), 1)
    outs = []
    for r in range(GQA_REP):
        grp = q[:, (r // 2) * LANES:(r // 2 + 1) * LANES]
        if r % 2:
            grp = pltpu.roll(grp, HEAD_DIM, 1)
        outs.append(jnp.where(lane < HEAD_DIM, grp, 0.0))
    return jnp.concatenate(outs, axis=0).astype(BF16)


def _overlap_matrix(n_cmp_pad, n_sel_pad):
    n = lax.broadcasted_iota(jnp.int32, (n_cmp_pad, n_sel_pad), 0) * CMP_STRIDE
    j = lax.broadcasted_iota(jnp.int32, (n_cmp_pad, n_sel_pad), 1) * SEL_BLOCK
    return jnp.where((n < j + SEL_BLOCK) & (n + CMP_BLOCK > j), 1.0, 0.0).astype(BF16)


def _overlap_matrix_t(n_sel_pad, n_cmp_pad):
    j = lax.broadcasted_iota(jnp.int32, (n_sel_pad, n_cmp_pad), 0) * SEL_BLOCK
    n = lax.broadcasted_iota(jnp.int32, (n_sel_pad, n_cmp_pad), 1) * CMP_STRIDE
    return jnp.where((n < j + SEL_BLOCK) & (n + CMP_BLOCK > j), 1.0, 0.0).astype(BF16)


def _kv_operands_t(kvt):
    v1 = jnp.concatenate([kvt[HEAD_DIM:], jnp.ones((HEAD_DIM, kvt.shape[1]), F32)], axis=0)
    return kvt.astype(BF16), v1.astype(BF16)


def _topk_axis0(score, n_top):
    jf = lax.broadcasted_iota(jnp.int32, score.shape, 0).astype(F32)
    big = float(score.shape[0])
    sel = jnp.zeros(score.shape, F32)
    for _ in range(n_top):
        m = jnp.max(score, axis=0, keepdims=True)
        first = jnp.min(jnp.where(score == m, jf, big), axis=0, keepdims=True)
        pick = jf == first
        sel = jnp.where(pick & (m > 0.5 * NEG_BIG), 1.0, sel)
        score = jnp.where(pick, REMOVED, score)
    return sel


def _nsa_prompt_body(kt, qn_ref, qr_ref, gl_ref, y_ref, const_ref, kvs_ref, kvw_ref, o_ref,
                     kc_sc, vc_sc, ks_sc, vs_sc, kw_sc, vw_sc):
    qt = pl.program_id(2)
    tq = qn_ref.shape[1]
    t_len = kvs_ref.shape[3]
    n_cmp_pad = y_ref.shape[1]
    n_sel_pad = t_len // SEL_BLOCK
    rows = GQA_REP * tq

    @pl.when(qt == 0)
    def _():
        y = y_ref[0]
        kv_cmp = _cmp_from_y(y[:, 0:LANES], y[:, LANES:2 * LANES], const_ref[0:1, :])
        kc_sc[...], vc_sc[...] = _kv_operands(kv_cmp)
        kop, vs_sc[...] = _kv_operands_t(kvs_ref[0, 0])
        ks_sc[0:LANES, :] = kop
        j = lax.broadcasted_iota(jnp.int32, (n_sel_pad, t_len), 0)
        tok = lax.broadcasted_iota(jnp.int32, (n_sel_pad, t_len), 1)
        ks_sc[LANES:, :] = jnp.where(j == _div_pow2(tok & (kt - 1), SEL_BLOCK), 1.0, 0.0).astype(BF16)
        kw_sc[...], vw_sc[...] = _kv_operands_t(kvw_ref[0, 0])

    q0 = qt * tq
    pos = q0 + lax.broadcasted_iota(jnp.int32, (tq, 1), 0)
    pos4 = jnp.concatenate([pos] * GQA_REP, axis=0)
    qc = _stack_heads(qn_ref[0])
    qr = _stack_heads(qr_ref[0])

    wlen = WINDOW + tq
    w0 = pl.multiple_of(jnp.maximum(q0 - WINDOW, 0), tq)
    s_w = _dot(qr, kw_sc[:, pl.ds(w0, wlen)])
    dist = pos4 - (w0 + lax.broadcasted_iota(jnp.int32, (1, wlen), 1))
    ok_w = (dist >= 0) & (dist < WINDOW)
    s_w = jnp.where(ok_w, s_w, NEG_BIG)
    m_w = jnp.max(s_w, axis=-1, keepdims=True)
    p_w = jnp.where(ok_w, jnp.exp2(s_w - m_w), 0.0)
    acc_w = _dot_nt(p_w.astype(BF16), vw_sc[:, pl.ds(w0, wlen)])

    s = _dot_nt(qc, kc_sc[...])
    cmp_end = lax.broadcasted_iota(jnp.int32, (1, n_cmp_pad), 1) * CMP_STRIDE + (CMP_BLOCK - 1)
    vis = cmp_end <= pos4
    s = jnp.where(vis, s, NEG_BIG)
    m = jnp.max(s, axis=-1, keepdims=True)
    p = jnp.where(vis, jnp.exp2(s - m), 0.0)
    p = p / jnp.maximum(jnp.sum(p, axis=-1, keepdims=True), 1e-30)
    acc_c = _dot(p.astype(BF16), vc_sc[...])
    psum = p[0:tq]
    for r in range(1, GQA_REP):
        psum = psum + p[r * tq:(r + 1) * tq]
    ov_t = _overlap_matrix_t(n_sel_pad, n_cmp_pad)
    hi, mid, lo = _split3(psum)
    imp_t = _dot_nt(ov_t, hi) + _dot_nt(ov_t, mid) + _dot_nt(ov_t, lo)

    blk = lax.broadcasted_iota(jnp.int32, (n_sel_pad, tq), 0)
    pos_l = q0 + lax.broadcasted_iota(jnp.int32, (1, tq), 1)
    cur = _div_pow2(pos_l, SEL_BLOCK)
    valid = blk * SEL_BLOCK <= pos_l
    forced = (blk == 0) | (blk == cur) | (blk == cur - 1)
    n_forced = 3
    score = jnp.where(valid & jnp.logical_not(forced), imp_t, jnp.where(valid, REMOVED, NEG_BIG))
    sel_t = _topk_axis0(score, min(N_SELECT, n_sel_pad) - n_forced)
    sel_t = jnp.where(valid & forced, 1.0, sel_t)
    sel_bias = jnp.where(sel_t.T > 0.5, 0.0, NEG_BIG)

    def sel_tile(k0, m_i, acc, causal):
        shift = (n_sel_pad - k0 // SEL_BLOCK) & (n_sel_pad - 1)
        bias = pltpu.roll(sel_bias, shift, 1).astype(BF16)
        lhs = jnp.concatenate([qr, jnp.concatenate([bias] * GQA_REP, axis=0)], axis=1)
        s_i = _dot(lhs, ks_sc[:, pl.ds(k0, kt)])
        if causal:
            tok = k0 + lax.broadcasted_iota(jnp.int32, (1, kt), 1)
            s_i = jnp.where(tok <= pos4, s_i, NEG_BIG)
        m_new = jnp.maximum(m_i, jnp.max(s_i, axis=-1, keepdims=True))
        p_i = jnp.exp2(s_i - m_new)
        acc = acc * jnp.exp2(m_i - m_new) + _dot_nt(p_i.astype(BF16), vs_sc[:, pl.ds(k0, kt)])
        return m_new, acc

    assert n_sel_pad & (n_sel_pad - 1) == 0 and kt // SEL_BLOCK <= n_sel_pad

    def past_tiles(i, c, n):
        for u in range(n):
            c = sel_tile(pl.multiple_of((i * n + u) * kt, kt), c[0], c[1], False)
        return c

    n_full = q0 // kt
    carry = (jnp.full((rows, 1), NEG_BIG, F32), jnp.zeros((rows, LANES), F32))
    carry = lax.fori_loop(0, n_full // 2, lambda i, c: past_tiles(i, c, 2), carry)
    carry = lax.fori_loop((n_full // 2) * 2, n_full, lambda i, c: past_tiles(i, c, 1), carry)
    _, acc_s = sel_tile(pl.multiple_of(n_full * kt, kt), carry[0], carry[1], True)

    def norm(acc):
        return acc / jnp.maximum(acc[:, HEAD_DIM:HEAD_DIM + 1], 1e-30)

    o_c, o_s, o_w = acc_c, norm(acc_s), norm(acc_w)
    gate = _sigmoid(gl_ref[0])
    lane = lax.broadcasted_iota(jnp.int32, (tq, LANES), 1)
    heads = []
    for r in range(GQA_REP):
        rs = slice(r * tq, (r + 1) * tq)
        heads.append(gate[:, 3 * r:3 * r + 1] * o_c[rs] + gate[:, 3 * r + 1:3 * r + 2] * o_s[rs]
                     + gate[:, 3 * r + 2:3 * r + 3] * o_w[rs])
    for j in range(GQA_REP // 2):
        o_ref[0, :, j * LANES:(j + 1) * LANES] = jnp.where(
            lane < HEAD_DIM, heads[2 * j], pltpu.roll(heads[2 * j + 1], HEAD_DIM, 1))


def _nsa_prompt(qn, qr, gl, y, const, kvs_t, kvw_t, tq, kt):
    b, _, _, t = kvs_t.shape
    n_cmp_pad = y.shape[1]
    n_sel_pad = t // SEL_BLOCK
    gw = GQA_REP * HEAD_DIM
    qspec = pl.BlockSpec((1, tq, gw), lambda i, g, j: (i, j, g))
    res = pl.BlockSpec((1, 1, LANES, t), lambda i, g, j: (i, g, 0, 0))
    return pl.pallas_call(
        functools.partial(_nsa_prompt_body, kt),
        grid=(b, NSA_KV_HEADS, t // tq),
        in_specs=[qspec, qspec, pl.BlockSpec((1, tq, LANES), lambda i, g, j: (i, j, g)),
                  pl.BlockSpec((1, n_cmp_pad, CMP_RATIO * LANES), lambda i, g, j: (i, 0, g)),
                  pl.BlockSpec((8, LANES), lambda i, g, j: (0, 0)), res, res],
        out_specs=qspec,
        out_shape=jax.ShapeDtypeStruct((b, t, NSA_W), F32),
        scratch_shapes=[pltpu.VMEM((n_cmp_pad, LANES), BF16), pltpu.VMEM((n_cmp_pad, LANES), BF16),
                        pltpu.VMEM((LANES + n_sel_pad, t), BF16)] + [pltpu.VMEM((LANES, t), BF16)] * 3,
        compiler_params=_cparams(("arbitrary", "arbitrary", "arbitrary")),
        name="nsa_prompt",
    )(qn, qr, gl, y, const, kvs_t, kvw_t)


def _nsa_sample_cmp_body(past, q_ref, y_ref, const_ref, idx_ref, oc_ref):
    sb = q_ref.shape[0]
    n_cmp_pad = y_ref.shape[1]
    n_sel = -(-(past + 1) // SEL_BLOCK)
    n_sel_pad = idx_ref.shape[3]
    n_rows = sb * NSA_KV_HEADS
    ov = _overlap_matrix(n_cmp_pad, n_sel_pad)
    cmp_end = lax.broadcasted_iota(jnp.int32, (1, n_cmp_pad), 1) * CMP_STRIDE + (CMP_BLOCK - 1)
    vis = cmp_end <= past
    row = lax.broadcasted_iota(jnp.int32, (n_rows, n_cmp_pad), 0)
    psum_all = jnp.zeros((n_rows, n_cmp_pad), F32)
    for s_i in range(sb):
        y = y_ref[s_i]
        for g in range(NSA_KV_HEADS):
            c0 = g * CMP_RATIO * LANES
            kv_cmp = _cmp_from_y(y[:, c0:c0 + LANES], y[:, c0 + LANES:c0 + 2 * LANES], const_ref[0:1, :])
            kc, vc = _kv_operands(kv_cmp)
            s = _dot_nt(q_ref[s_i, g], kc)
            s = jnp.where(vis, s, NEG_BIG)
            m = jnp.max(s, axis=-1, keepdims=True)
            p = jnp.where(vis, jnp.exp2(s - m), 0.0)
            p = p / jnp.maximum(jnp.sum(p, axis=-1, keepdims=True), 1e-30)
            oc_ref[s_i, g] = _dot(p.astype(BF16), vc)
            psum = jnp.sum(p[0:GQA_REP], axis=0, keepdims=True)
            psum_all = jnp.where(row == s_i * NSA_KV_HEADS + g, psum, psum_all)
    hi, mid, lo = _split3(psum_all)
    imp = _dot(hi, ov) + _dot(mid, ov) + _dot(lo, ov)
    blk = lax.broadcasted_iota(jnp.int32, (n_rows, n_sel_pad), 1)
    cur = past // SEL_BLOCK
    valid = (blk * SEL_BLOCK <= past) & (blk < n_sel)
    forced = (blk == 0) | (blk == cur) | (blk == cur - 1)
    jf = blk.astype(F32)
    score = jnp.where(valid, jnp.where(forced, FORCED_SCORE, imp), NEG_BIG)
    idx = jnp.full((n_rows, n_sel_pad), -1.0, F32)
    for k in range(min(N_SELECT, n_sel)):
        mx = jnp.max(score, axis=-1, keepdims=True)
        first = jnp.min(jnp.where(score == mx, jf, float(n_sel_pad)), axis=-1, keepdims=True)
        idx = jnp.where((blk == k) & (mx > 0.5 * NEG_BIG), first, idx)
        score = jnp.where(jf == first, REMOVED, score)
    idx = idx.astype(jnp.int32)
    for s_i in range(sb):
        for g in range(NSA_KV_HEADS):
            r = s_i * NSA_KV_HEADS + g
            idx_ref[s_i, g] = jnp.broadcast_to(idx[r:r + 1, :], (8, n_sel_pad))


def _nsa_sample_cmp(past, q8, y, const, n_sel_pad, sb):
    b = q8.shape[0]
    n_cmp_pad = y.shape[1]
    blk4 = lambda w: pl.BlockSpec((sb, NSA_KV_HEADS, 8, w), lambda i: (i, 0, 0, 0))
    return pl.pallas_call(
        functools.partial(_nsa_sample_cmp_body, past),
        grid=(b // sb,),
        in_specs=[blk4(LANES), pl.BlockSpec((sb, n_cmp_pad, y.shape[2]), lambda i: (i, 0, 0)),
                  pl.BlockSpec((8, LANES), lambda i: (0, 0))],
        out_specs=[blk4(n_sel_pad), blk4(LANES)],
        out_shape=[jax.ShapeDtypeStruct((b, NSA_KV_HEADS, 8, n_sel_pad), jnp.int32),
                   jax.ShapeDtypeStruct((b, NSA_KV_HEADS, 8, LANES), F32)],
        compiler_params=_cparams(("arbitrary",)),
        name="nsa_sample_cmp",
    )(q8, y, const)


def _nsa_sample_sel_body(past, n_top, *refs):
    idx_ref, pt_ref = refs[0], refs[1]
    del pt_ref
    blk_refs = refs[2:2 + n_top]
    q_ref, new_s_ref, new_w_ref, win_ref, oc_ref, gl_ref, o_ref = refs[2 + n_top:]
    b, g = pl.program_id(0), pl.program_id(1)
    q = q_ref[0, 0]
    n_past_blocks = past // SEL_BLOCK

    def attend(pieces):
        ms = [jnp.max(jnp.where(ok, s, NEG_BIG), axis=-1, keepdims=True) for s, ok, _, _ in pieces]
        m = functools.reduce(jnp.maximum, ms)
        acc = jnp.zeros((8, LANES), F32)
        for s, ok, v, v_t in pieces:
            p = jnp.where(ok, jnp.exp2(s - m), 0.0).astype(BF16)
            acc = acc + (_dot_nt(p, v) if v_t else _dot(p, v))
        return acc

    def new_token(ref, enabled):
        kop, vop = _kv_operands(jnp.broadcast_to(ref[0, 0], (8, LANES)))
        first = lax.broadcasted_iota(jnp.int32, (8, 8), 1) == 0
        return _dot_nt(q, kop), first & enabled, vop, False

    per_page = PAGE_SIZE // SEL_BLOCK
    tok = lax.broadcasted_iota(jnp.int32, (8, n_top * PAGE_SIZE), 1)
    ok = tok < 0
    has_new = jnp.zeros((8, 8), jnp.int32)
    for k in range(n_top):
        j = idx_ref[(b * NSA_KV_HEADS + g) * n_top + k]
        lo = k * PAGE_SIZE + jnp.where((j >= 0) & (j < n_past_blocks), (j % per_page) * SEL_BLOCK, PAGE_SIZE)
        ok = ok | ((tok >= lo) & (tok < lo + SEL_BLOCK) & (tok < (k + 1) * PAGE_SIZE))
        has_new = has_new + jnp.where(j == n_past_blocks, 1, 0)
    kop, vop = _kv_operands_t(jnp.concatenate([blk_refs[k][...] for k in range(n_top)], axis=1))
    acc_s = attend([(_dot(q, kop), ok, vop, True), new_token(new_s_ref, has_new > 0)])

    win = win_ref[0]
    w_buf = win.shape[1]
    kw, vw = _kv_operands_t(win)
    dist = w_buf - lax.broadcasted_iota(jnp.int32, (8, w_buf), 1)
    ok_buf = (dist < WINDOW) & (past - dist >= 0)
    acc_w = attend([(_dot(q, kw), ok_buf, vw, True), new_token(new_w_ref, jnp.ones((8, 8), jnp.int32) > 0)])

    def norm(acc):
        return acc / jnp.maximum(acc[:, HEAD_DIM:HEAD_DIM + 1], 1e-30)

    gate = _sigmoid(gl_ref[0, 0])
    rsel = lax.broadcasted_iota(jnp.int32, (8, LANES), 0)
    gc = jnp.zeros((8, LANES), F32)
    gs = jnp.zeros((8, LANES), F32)
    gw = jnp.zeros((8, LANES), F32)
    for r in range(GQA_REP):
        gc = jnp.where(rsel == r, gate[:, 3 * r:3 * r + 1], gc)
        gs = jnp.where(rsel == r, gate[:, 3 * r + 1:3 * r + 2], gs)
        gw = jnp.where(rsel == r, gate[:, 3 * r + 2:3 * r + 3], gw)
    o_ref[0, 0] = gc * oc_ref[0, 0] + gs * norm(acc_s) + gw * norm(acc_w)


def _nsa_sample_sel(past, layer, idx, page_table, sel_pool, q8, new_s, new_w, win, oc, gl):
    b, n_past_pages = page_table.shape
    n_top = idx.shape[0] // (b * NSA_KV_HEADS)
    n_past_blocks = past // SEL_BLOCK
    per_page = PAGE_SIZE // SEL_BLOCK

    def blk_spec(k):
        def imap(i, g, idx_ref, pt_ref):
            j = jnp.clip(idx_ref[(i * NSA_KV_HEADS + g) * n_top + k], 0, n_past_blocks - 1)
            return (layer, pt_ref[i * n_past_pages + j // per_page], g, 0, 0)
        return pl.BlockSpec((None, None, None, LANES, PAGE_SIZE), imap)

    v4 = lambda w: pl.BlockSpec((1, 1, 8, w), lambda i, g, a, c: (i, g, 0, 0))
    v1 = pl.BlockSpec((1, 1, 1, LANES), lambda i, g, a, c: (i, g, 0, 0))
    grid_spec = pltpu.PrefetchScalarGridSpec(
        num_scalar_prefetch=2,
        grid=(b, NSA_KV_HEADS),
        in_specs=[blk_spec(k) for k in range(n_top)] + [
            v4(LANES), v1, v1,
            pl.BlockSpec((None, None, 1, LANES, win.shape[4]), lambda i, g, a, c: (layer, i, g, 0, 0)),
            v4(LANES), v1],
        out_specs=v4(LANES),
    )
    return pl.pallas_call(
        functools.partial(_nsa_sample_sel_body, past, n_top),
        grid_spec=grid_spec,
        out_shape=jax.ShapeDtypeStruct((b, NSA_KV_HEADS, 8, LANES), F32),
        compiler_params=_cparams(("arbitrary", "arbitrary")),
        name="nsa_sample_sel",
    )(idx, page_table.reshape(-1), *([sel_pool] * n_top), q8, new_s, new_w, win, oc, gl)


def _win_update_body(win_ref, new_ref, o_ref):
    w_buf = win_ref.shape[3]
    last = lax.broadcasted_iota(jnp.int32, (LANES, LANES), 1) == LANES - 1
    for g in range(NSA_KV_HEADS):
        shifted = pltpu.roll(win_ref[0, g], w_buf - 1, 1)
        new_col = jnp.broadcast_to(new_ref[0, g], (LANES, LANES)).T
        o_ref[0, g, :, 0:w_buf - LANES] = shifted[:, 0:w_buf - LANES]
        o_ref[0, g, :, w_buf - LANES:] = jnp.where(last, new_col, shifted[:, w_buf - LANES:])


def _win_update(layer, win, new):
    _, b, g, r, w_buf = win.shape
    return pl.pallas_call(
        _win_update_body,
        grid=(b,),
        in_specs=[pl.BlockSpec((None, 1, g, r, w_buf), lambda i: (layer, i, 0, 0, 0)),
                  pl.BlockSpec((1, g, 1, r), lambda i: (i, 0, 0, 0))],
        out_specs=pl.BlockSpec((1, g, r, w_buf), lambda i: (i, 0, 0, 0)),
        out_shape=jax.ShapeDtypeStruct((b, g, r, w_buf), F32),
        compiler_params=_cparams(("arbitrary",)),
        name="win_update",
    )(win, new)


def _finish_body(x_ref, ho_ref, no_ref, wo_ref, g1_ref, g2_ref, g3_ref, w1_ref, w2_ref, o_ref):
    wo = wo_ref[...]
    y = _dot(ho_ref[...].astype(BF16), wo[0:HG_W]) + _dot(no_ref[...].astype(BF16), wo[HG_W:])
    h = x_ref[...] + _rms(y, g1_ref[...])
    a = _dot(_rms(h, g2_ref[...]).astype(BF16), w1_ref[...])
    dff = a.shape[1] // 2
    u, v = a[:, :dff], a[:, dff:]
    act = (u * _sigmoid(u)) * v
    y2 = _dot(act.astype(BF16), w2_ref[...])
    o_ref[...] = h + _rms(y2, g3_ref[...])


def _finish(x2d, ho, no, wo, g1, g2, g3, w1, w2, tm):
    m, d = x2d.shape
    row = lambda w: pl.BlockSpec((tm, w), lambda i: (i, 0))
    full = lambda a: pl.BlockSpec(a.shape, lambda i: (0, 0), pipeline_mode=pl.Buffered(1))
    return pl.pallas_call(
        _finish_body,
        grid=(m // tm,),
        in_specs=[row(d), row(HG_W), row(NSA_W), full(wo), full(g1), full(g2), full(g3), full(w1), full(w2)],
        out_specs=row(d),
        out_shape=jax.ShapeDtypeStruct((m, d), F32),
        compiler_params=_cparams(("arbitrary",)),
        name="finish",
    )(x2d, ho, no, wo, g1, g2, g3, w1, w2)


def _tile(n, pref):
    while n % pref:
        pref //= 2
    return pref


def kernel(x_prompt, x_sample, state_hgrn, cache_cmp_kv, cache_sel_kv, cache_win_kv, page_table,
           norm_mix_pre, norm_mix_post, norm_ffn_pre, norm_ffn_post, w_in, w_out, hg_lb_logits,
           hg_out_norm, cmp_pe, w_cmp, w_ffn_in, w_ffn_out):
    bp, tp, d = x_prompt.shape
    bs, ts, _ = x_sample.shape
    assert ts == 1
    depth = w_in.shape[0]
    n_pages = page_table.shape[1]
    past = n_pages * PAGE_SIZE
    assert tp % PAGE_SIZE == 0 and tp >= WINDOW + PAGE_SIZE

    def token_minor(cache):
        return cache.transpose(0, 1, 3, 4, 5, 2).reshape(cache.shape[0], cache.shape[1], NSA_KV_HEADS,
                                                        2 * HEAD_DIM, cache.shape[2])

    def token_major(a):
        return a.reshape(a.shape[0], NSA_KV_HEADS, 2, HEAD_DIM, a.shape[3]).transpose(0, 4, 1, 2, 3)

    tm_p = _tile(tp, 512)
    tabs_p = _rope_tables(jnp.arange(tp))
    tabs_s = _rope_tables(jnp.full((bs * ts,), past, jnp.int32))
    cmp_pool, sel_pool, win_pool = token_minor(cache_cmp_kv), token_minor(cache_sel_kv), token_minor(cache_win_kv)
    prompt_pages = jnp.zeros((bp, tp // PAGE_SIZE), jnp.int32)
    n_sel_s = -(-(past + ts) // SEL_BLOCK)
    n_sel_pad_s = -(-n_sel_s // LANES) * LANES
    n_top_s = min(N_SELECT, n_sel_s)

    xp = x_prompt.reshape(bp * tp, d)
    xs = x_sample.reshape(bs * ts, d)
    hgp, hgs, cmpp, cmps, selp, sels, winp, wins = [], [], [], [], [], [], [], []
    for l in range(depth):
        w_in_bf = _pad_w_in(w_in[l])
        w_out_bf = w_out[l].astype(BF16)
        w1_bf = w_ffn_in[l].astype(BF16)
        w2_bf = w_ffn_out[l].astype(BF16)
        g_pre, g_post = norm_mix_pre[l][None, :], norm_mix_post[l][None, :]
        g_fpre, g_fpost = norm_ffn_pre[l][None, :], norm_ffn_post[l][None, :]
        gain = hg_out_norm[l][None, :]
        w_c, pe_c = _compress_weights(w_cmp[l], cmp_pe[l])

        hq, fl, hi, hg, qn, qr, kvc_t, kvs_t, kvw_t, gl = _inproj(xp, g_pre, w_in_bf, tabs_p, tm_p, True)
        r3 = lambda a: a.reshape(bp, tp, a.shape[-1])
        ho, s_fin = _hgrn_prompt(l, hg_lb_logits, r3(hq), r3(fl), r3(hi), r3(hg), gain, _tile(tp, 256))
        y, const = _compress(kvc_t, lambda i, pg, pt: (i, 0, 0, pg), prompt_pages, w_c, pe_c,
                             _tile(tp // PAGE_SIZE, CMP_PAGES_PER_STEP))
        no = _nsa_prompt(r3(qn), r3(qr), r3(gl), y, const, kvs_t, kvw_t, PAGE_SIZE, _tile(tp, SEL_KV_TILE))
        xp = _finish(xp, ho.reshape(bp * tp, HG_W), no.reshape(bp * tp, NSA_W), w_out_bf, g_post, g_fpre,
                     g_fpost, w1_bf, w2_bf, tm_p)
        hgp.append(s_fin)
        cmpp.append(token_major(kvc_t))
        selp.append(token_major(kvs_t))
        winp.append(token_major(kvw_t[..., tp - min(WINDOW, tp):]))

        hq, fl, hi, hg, qn, qr, kvc, kvs, kvw, gl = _inproj(xs, g_pre, w_in_bf, tabs_s, bs * ts, False)
        v3 = lambda a: a.reshape(bs, 1, a.shape[-1])
        s_new, ho = _hgrn_sample(l, hg_lb_logits, state_hgrn[l], v3(hq), v3(fl), v3(hi), v3(hg), gain)
        y, const = _compress(cmp_pool, lambda i, pg, pt, l=l: (l, pt[i * n_pages + pg], 0, 0, 0),
                             page_table, w_c, pe_c, _tile(n_pages, CMP_PAGES_PER_STEP))

        def q8(q):
            q4 = q.reshape(bs, NSA_KV_HEADS, GQA_REP, HEAD_DIM)
            return jnp.pad(q4, ((0, 0), (0, 0), (0, 8 - GQA_REP), (0, LANES - HEAD_DIM)))

        idx, oc = _nsa_sample_cmp(past, q8(qn), y, const, n_sel_pad_s, _tile(bs, 4))
        g4 = lambda a: a.reshape(bs, NSA_KV_HEADS, 1, LANES)
        o8 = _nsa_sample_sel(past, l, idx[:, :, 0, :n_top_s].reshape(-1), page_table, sel_pool, q8(qr),
                             g4(kvs), g4(kvw), win_pool, oc, g4(gl))
        no = o8[:, :, :GQA_REP, :HEAD_DIM].reshape(bs * ts, NSA_W)
        win_new = _win_update(l, win_pool, g4(kvw))
        xs = _finish(xs, ho.reshape(bs * ts, HG_W), no, w_out_bf, g_post, g_fpre, g_fpost, w1_bf, w2_bf,
                     bs * ts)
        kv5 = lambda a, t: a.reshape(-1, t, NSA_KV_HEADS, 2, HEAD_DIM)
        hgs.append(s_new)
        cmps.append(kv5(kvc, ts))
        sels.append(kv5(kvs, ts))
        wins.append(token_major(win_new))

    return (xp.reshape(bp, tp, d), xs.reshape(bs, ts, d), jnp.stack(hgp), jnp.stack(hgs), jnp.stack(cmpp),
            jnp.stack(cmps), jnp.stack(selp), jnp.stack(sels), jnp.stack(winp), jnp.stack(wins))
```

```python
import functools

import jax
import jax.numpy as jnp
import numpy as np
from jax import lax
from jax.experimental import pallas as pl
from jax.experimental.pallas import tpu as pltpu

F32 = jnp.float32
BF16 = jnp.bfloat16

HG_HEADS = 4
HG_DK = 128
HG_DV = 128
HG_W = HG_HEADS * HG_DV
HEAD_DIM = 64
NSA_HEADS = 8
NSA_KV_HEADS = 2
GQA_REP = NSA_HEADS // NSA_KV_HEADS
NSA_W = NSA_HEADS * HEAD_DIM
KV_W = NSA_KV_HEADS * 2 * HEAD_DIM
N_GATES = NSA_HEADS * 3
CMP_STRIDE = 16
CMP_RATIO = 2
CMP_BLOCK = CMP_STRIDE * CMP_RATIO
SEL_BLOCK = 64
N_SELECT = 16
WINDOW = 512
PAGE_SIZE = 128
ROPE_THETA = 500000.0
ROT_DIM = HEAD_DIM // 4
EPS = 1e-6
FORCED_SCORE = 1e6
NEG_BIG = -1e30
LB_FLOOR = 1e-30
REMOVED = -3e38
LOG2_E = 1.4426950408889634

LANES = 128
HG_CHUNK = 16
GATE_PAD = 2 * LANES
CMP_PAGES_PER_STEP = 32
SEL_KV_TILE = 1024
COL_QN = 4 * HG_W
COL_KVC = COL_QN + NSA_W
COL_KVS = COL_KVC + KV_W
COL_KVW = COL_KVS + KV_W
COL_GL = COL_KVW + KV_W
IN_W_PAD = COL_GL + GATE_PAD

VMEM_LIMIT = 56 * 1024 * 1024


def _cparams(sem):
    return pltpu.CompilerParams(dimension_semantics=sem, vmem_limit_bytes=VMEM_LIMIT)


def _rms(x, g):
    return x * lax.rsqrt(jnp.mean(x * x, axis=-1, keepdims=True) + EPS) * g


def _sigmoid(x):
    return 1.0 / (1.0 + jnp.exp(-x))


def _dot(a, b):
    return jnp.dot(a, b, preferred_element_type=F32)


def _dot_nt(a, b):
    return lax.dot_general(a, b, (((1,), (1,)), ((), ())), preferred_element_type=F32)


def _dot_tn(a, b):
    return lax.dot_general(a, b, (((0,), (0,)), ((), ())), preferred_element_type=F32)


def _div_pow2(x, n):
    assert n & (n - 1) == 0
    return x >> (n.bit_length() - 1)


def _split3(x):
    hi = x.astype(BF16)
    r1 = x - hi.astype(F32)
    mid = r1.astype(BF16)
    lo = (r1 - mid.astype(F32)).astype(BF16)
    return hi, mid, lo


def _rope128(x, c, s1, s2):
    return x * c + pltpu.roll(x, LANES - ROT_DIM // 2, 1) * s1 + pltpu.roll(x, ROT_DIM // 2, 1) * s2


def _inproj_body(kv_t, x_ref, g_ref, w_ref, cq_ref, s1q_ref, s2q_ref,
                 hq_ref, fl_ref, hi_ref, hg_ref, qn_ref, qr_ref, kvc_ref, kvs_ref, kvw_ref, gl_ref):
    def put_kv(ref, j, blk):
        if kv_t:
            ref[0, j] = blk.T
        else:
            ref[:, j * LANES:(j + 1) * LANES] = blk

    hn = _rms(x_ref[...], g_ref[...])
    z = _dot(hn.astype(BF16), w_ref[...])
    hq_ref[...] = z[:, 0:HG_W]
    fl_ref[...] = z[:, HG_W:2 * HG_W]
    hi_ref[...] = z[:, 2 * HG_W:3 * HG_W]
    hg_ref[...] = z[:, 3 * HG_W:4 * HG_W]
    scale = HEAD_DIM ** -0.5 * LOG2_E
    cq, s1q, s2q = cq_ref[...], s1q_ref[...], s2q_ref[...]
    for j in range(NSA_W // LANES):
        blk = z[:, COL_QN + j * LANES:COL_QN + (j + 1) * LANES]
        qn_ref[:, j * LANES:(j + 1) * LANES] = (blk * scale).astype(BF16)
        qr_ref[:, j * LANES:(j + 1) * LANES] = (_rope128(blk, cq, s1q, s2q) * scale).astype(BF16)
    k_half = lax.broadcasted_iota(jnp.int32, cq.shape, 1) < HEAD_DIM
    ck, s1k, s2k = jnp.where(k_half, cq, 1.0), jnp.where(k_half, s1q, 0.0), jnp.where(k_half, s2q, 0.0)
    for j in range(NSA_KV_HEADS):
        cols = lambda c0: z[:, c0 + j * LANES:c0 + (j + 1) * LANES]
        put_kv(kvc_ref, j, cols(COL_KVC))
        put_kv(kvs_ref, j, _rope128(cols(COL_KVS), ck, s1k, s2k))
        put_kv(kvw_ref, j, _rope128(cols(COL_KVW), ck, s1k, s2k))
    gl_ref[...] = z[:, COL_GL:COL_GL + GATE_PAD]


def _inproj(x2d, gain, w_bf, tabs, tm, kv_t):
    m, d = x2d.shape
    n_pos_blocks = tabs[0].shape[0] // tm
    row = lambda i: (i, 0)
    tab = lambda i: (i % n_pos_blocks, 0)
    const = lambda i: (0, 0)
    widths = (HG_W, HG_W, HG_W, HG_W, NSA_W, NSA_W, GATE_PAD)
    dtypes = (F32, F32, F32, F32, BF16, BF16, F32)
    specs = [pl.BlockSpec((tm, w), row) for w in widths]
    shapes = [jax.ShapeDtypeStruct((m, w), dt) for w, dt in zip(widths, dtypes)]
    if kv_t:
        t = tabs[0].shape[0]
        kv_spec = pl.BlockSpec((1, NSA_KV_HEADS, LANES, tm), lambda i: (i // n_pos_blocks, 0, 0, i % n_pos_blocks))
        kv_shape = jax.ShapeDtypeStruct((m // t, NSA_KV_HEADS, LANES, t), F32)
    else:
        kv_spec = pl.BlockSpec((tm, KV_W), row)
        kv_shape = jax.ShapeDtypeStruct((m, KV_W), F32)
    return pl.pallas_call(
        functools.partial(_inproj_body, kv_t),
        grid=(m // tm,),
        in_specs=[pl.BlockSpec((tm, d), row), pl.BlockSpec((1, d), const),
                  pl.BlockSpec((d, IN_W_PAD), const, pipeline_mode=pl.Buffered(1))]
        + [pl.BlockSpec((tm, LANES), tab)] * 3,
        out_specs=specs[:6] + [kv_spec] * 3 + specs[6:],
        out_shape=shapes[:6] + [kv_shape] * 3 + shapes[6:],
        compiler_params=_cparams(("arbitrary",)),
        name="inproj",
    )(x2d, gain, w_bf, *tabs)


def _rope_tables(pos):
    half = ROT_DIM // 2
    inv = ROPE_THETA ** (-2.0 * jnp.arange(half, dtype=F32) / ROT_DIM)
    ang = pos.astype(F32)[:, None] * inv[None, :]
    cos, sin = jnp.cos(ang), jnp.sin(ang)
    p = pos.shape[0]
    one = jnp.ones((p, HEAD_DIM - ROT_DIM), F32)
    zero = jnp.zeros((p, HEAD_DIM - ROT_DIM), F32)
    zh = jnp.zeros((p, half), F32)
    c64 = jnp.concatenate([cos, cos, one], axis=1)
    s1_64 = jnp.concatenate([-sin, zh, zero], axis=1)
    s2_64 = jnp.concatenate([zh, sin, zero], axis=1)
    return tuple(jnp.concatenate([t, t], axis=1) for t in (c64, s1_64, s2_64))


def _pad_w_in(w):
    d = w.shape[0]
    per_g = GQA_REP * 3
    pad = jnp.zeros((d, LANES - per_g), w.dtype)
    gates = [jnp.concatenate([w[:, COL_GL + g * per_g:COL_GL + (g + 1) * per_g], pad], axis=1)
             for g in range(NSA_KV_HEADS)]
    return jnp.concatenate([w[:, :COL_GL]] + gates, axis=1).astype(BF16)


def _lower_bound(lbl_ref, layer):
    rows = [lbl_ref[l:l + 1, :] for l in range(lbl_ref.shape[0])]
    mx = functools.reduce(jnp.maximum, rows)
    ex = [jnp.exp(r - mx) for r in rows]
    den = functools.reduce(lambda a, b: a + b, ex)
    sm = [e / den for e in ex]
    cum = sm[0]
    for l in range(1, layer + 1):
        cum = cum + sm[l]
    return cum - sm[0]


def _forget_gate(fl, lb):
    logsig = jnp.minimum(fl, 0.0) - jnp.log1p(jnp.exp(-jnp.abs(fl)))
    a = jnp.log(jnp.maximum(lb, LB_FLOOR))
    b = jnp.log1p(-lb) + logsig
    logf = jnp.maximum(a, b) + jnp.log1p(jnp.exp(-jnp.abs(a - b)))
    kh = (1.0 - lb) * _sigmoid(-fl)
    return logf, kh


def _readout(o, g, gain):
    on = o * lax.rsqrt(jnp.mean(o * o, axis=-1, keepdims=True) + EPS) * gain
    return on * (g * _sigmoid(g))


def _hgrn_prompt_body(layer, lbl_ref, q_ref, f_ref, i_ref, g_ref, gain_ref, o_ref, s_ref,
                      st_sc, k_sc, b_sc, o_sc):
    t = pl.program_id(1)
    ct = q_ref.shape[1]

    @pl.when(t == 0)
    def _():
        st_sc[...] = jnp.zeros_like(st_sc)

    lb = _lower_bound(lbl_ref, layer)
    logf, kh = _forget_gate(f_ref[0], lb)
    k_sc[...] = kh
    ri = lax.broadcasted_iota(jnp.int32, (ct, ct), 0)
    ci = lax.broadcasted_iota(jnp.int32, (ct, ct), 1)
    tri = jnp.where((ci <= ri) & (_div_pow2(ci, HG_CHUNK) == _div_pow2(ri, HG_CHUNK)), 1.0, 0.0).astype(BF16)
    hi, mid, lo = _split3(logf)
    b_sc[...] = _dot(tri, hi) + _dot(tri, mid) + _dot(tri, lo)

    row = lax.broadcasted_iota(jnp.int32, (HG_CHUNK, HG_DK), 0)

    def chunk(c, carry):
        r0 = pl.multiple_of(c * HG_CHUNK, HG_CHUNK)
        for h in range(HG_HEADS):
            hl = slice(h * HG_DK, (h + 1) * HG_DK)
            q = q_ref[0, pl.ds(r0, HG_CHUNK), hl]
            k = k_sc[pl.ds(r0, HG_CHUNK), hl]
            v = i_ref[0, pl.ds(r0, HG_CHUNK), hl]
            b = b_sc[pl.ds(r0, HG_CHUNK), hl]
            st = st_sc[h]
            o = _dot_nt((q * jnp.exp(b)).astype(BF16), st.astype(BF16))
            for s in range(HG_CHUNK):
                w = jnp.where(row >= s, jnp.exp(b - b[s:s + 1, :]), 0.0)
                d = jnp.sum(q * (k[s:s + 1, :] * w), axis=-1, keepdims=True)
                o = o + d * v[s:s + 1, :]
            o_sc[pl.ds(r0, HG_CHUNK), hl] = o
            blast = b[HG_CHUNK - 1:HG_CHUNK, :]
            kdec = k * jnp.exp(blast - b)
            st_sc[h] = st * jnp.exp(blast) + _dot_tn(v.astype(BF16), kdec.astype(BF16))
        return carry

    lax.fori_loop(0, ct // HG_CHUNK, chunk, 0)

    gain = gain_ref[...]
    for h in range(HG_HEADS):
        hl = slice(h * HG_DV, (h + 1) * HG_DV)
        o_ref[0, :, hl] = _readout(o_sc[:, hl], g_ref[0, :, hl], gain)

    @pl.when(t == pl.num_programs(1) - 1)
    def _():
        for h in range(HG_HEADS):
            s_ref[0, h] = st_sc[h].T


def _hgrn_prompt(layer, lbl, hq, fl, hi, hg, gain, ct):
    b, t, _ = hq.shape
    tile = pl.BlockSpec((1, ct, HG_W), lambda i, j: (i, j, 0))
    return pl.pallas_call(
        functools.partial(_hgrn_prompt_body, layer),
        grid=(b, t // ct),
        in_specs=[pl.BlockSpec(lbl.shape, lambda i, j: (0, 0)), tile, tile, tile, tile,
                  pl.BlockSpec((1, HG_DV), lambda i, j: (0, 0))],
        out_specs=[tile, pl.BlockSpec((1, HG_HEADS, HG_DK, HG_DV), lambda i, j: (i, 0, 0, 0))],
        out_shape=[jax.ShapeDtypeStruct((b, t, HG_W), F32),
                   jax.ShapeDtypeStruct((b, HG_HEADS, HG_DK, HG_DV), F32)],
        scratch_shapes=[pltpu.VMEM((HG_HEADS, HG_DV, HG_DK), F32), pltpu.VMEM((ct, HG_W), F32),
                        pltpu.VMEM((ct, HG_W), F32), pltpu.VMEM((ct, HG_W), F32)],
        compiler_params=_cparams(("arbitrary", "arbitrary")),
        name="hgrn_prompt",
    )(lbl, hq, fl, hi, hg, gain)


def _hgrn_sample_body(layer, lbl_ref, s0_ref, q_ref, f_ref, i_ref, g_ref, gain_ref, s_ref, o_ref):
    lb = _lower_bound(lbl_ref, layer)
    logf, kh = _forget_gate(f_ref[0], lb)
    q, v, g = q_ref[0], i_ref[0], g_ref[0]
    gain = gain_ref[...]

    def col(x):
        return jnp.broadcast_to(x, (HG_DK, HG_DK)).T

    for h in range(HG_HEADS):
        hl = slice(h * HG_DK, (h + 1) * HG_DK)
        s_new = col(jnp.exp(logf[:, hl])) * s0_ref[0, h] + col(kh[:, hl]) * v[:, hl]
        s_ref[0, h] = s_new
        o = jnp.sum(col(q[:, hl]) * s_new, axis=0, keepdims=True)
        o_ref[0, :, hl] = _readout(o, g[:, hl], gain)


def _hgrn_sample(layer, lbl, s0, hq, fl, hi, hg, gain):
    b = s0.shape[0]
    vec = pl.BlockSpec((1, 1, HG_W), lambda i: (i, 0, 0))
    st = pl.BlockSpec((1, HG_HEADS, HG_DK, HG_DV), lambda i: (i, 0, 0, 0))
    return pl.pallas_call(
        functools.partial(_hgrn_sample_body, layer),
        grid=(b,),
        in_specs=[pl.BlockSpec(lbl.shape, lambda i: (0, 0)), st, vec, vec, vec, vec,
                  pl.BlockSpec((1, HG_DV), lambda i: (0, 0))],
        out_specs=[st, vec],
        out_shape=[jax.ShapeDtypeStruct(s0.shape, F32), jax.ShapeDtypeStruct((b, 1, HG_W), F32)],
        compiler_params=_cparams(("arbitrary",)),
        name="hgrn_sample",
    )(lbl, s0, hq, fl, hi, hg, gain)


def _compress_body(n_pages, *refs):
    page_refs = refs[1:1 + n_pages]
    w_ref, pe_ref, y_ref, const_ref = refs[1 + n_pages:]
    rows = PAGE_SIZE // CMP_STRIDE
    tok = lax.broadcasted_iota(jnp.int32, (PAGE_SIZE, PAGE_SIZE), 0)
    col = lax.broadcasted_iota(jnp.int32, (PAGE_SIZE, PAGE_SIZE), 1)
    perm = jnp.where(tok == CMP_STRIDE * (col & (rows - 1)) + _div_pow2(col, rows), 1.0, 0.0).astype(BF16)
    for g in range(NSA_KV_HEADS):
        pages = jnp.concatenate([page_refs[p][g].astype(BF16) for p in range(n_pages)], axis=0)
        permuted = _dot(pages, perm)
        pieces = [permuted[p * LANES:(p + 1) * LANES, :].T for p in range(n_pages)]
        lhs = jnp.concatenate(
            [jnp.concatenate([pc[c * rows:(c + 1) * rows, :] for pc in pieces], axis=0)
             for c in range(CMP_STRIDE)], axis=1)
        y_ref[0, :, g * CMP_RATIO * LANES:(g + 1) * CMP_RATIO * LANES] = _dot(lhs.astype(BF16), w_ref[...])

    @pl.when((pl.program_id(0) == 0) & (pl.program_id(1) == 0))
    def _():
        tot = jnp.zeros((8, LANES), F32)
        for c in range(CMP_STRIDE):
            for m in range(CMP_RATIO):
                pe_row = jnp.broadcast_to(pe_ref[m, c:c + 1, :], (8, LANES))
                tot = tot + _dot(pe_row.astype(BF16), w_ref[c * LANES:(c + 1) * LANES, m * LANES:(m + 1) * LANES])
        const_ref[...] = tot


def _compress(pages, page_index, page_table, w_c, pe_c, n_pages):
    s, p = page_table.shape
    rows = PAGE_SIZE // CMP_STRIDE
    steps = p // n_pages

    def page_spec(k):
        return pl.BlockSpec((None,) * (pages.ndim - 3) + (NSA_KV_HEADS, LANES, PAGE_SIZE),
                            lambda i, j, pt: page_index(i, j * n_pages + k, pt))

    page_specs = [page_spec(k) for k in range(n_pages)]
    grid_spec = pltpu.PrefetchScalarGridSpec(
        num_scalar_prefetch=1,
        grid=(s, steps),
        in_specs=page_specs + [
            pl.BlockSpec(w_c.shape, lambda i, j, pt: (0, 0)),
            pl.BlockSpec(pe_c.shape, lambda i, j, pt: (0, 0, 0))],
        out_specs=[pl.BlockSpec((1, n_pages * rows, NSA_KV_HEADS * CMP_RATIO * LANES),
                                lambda i, j, pt: (i, j, 0)),
                   pl.BlockSpec((8, LANES), lambda i, j, pt: (0, 0))],
    )
    return pl.pallas_call(
        functools.partial(_compress_body, n_pages),
        grid_spec=grid_spec,
        out_shape=[jax.ShapeDtypeStruct((s, p * rows, NSA_KV_HEADS * CMP_RATIO * LANES), F32),
                   jax.ShapeDtypeStruct((8, LANES), F32)],
        compiler_params=_cparams(("arbitrary", "arbitrary")),
        name="compress",
    )(page_table.reshape(-1), *([pages] * n_pages), w_c, pe_c)


def _compress_weights(w_cmp, pe):
    w_r = w_cmp.reshape(2, CMP_RATIO, CMP_STRIDE, HEAD_DIM, HEAD_DIM)
    z = jnp.zeros((CMP_STRIDE, HEAD_DIM, HEAD_DIM), w_cmp.dtype)
    blocks = []
    for m in range(CMP_RATIO):
        top = jnp.concatenate([w_r[0, m], z], axis=2)
        bot = jnp.concatenate([z, w_r[1, m]], axis=2)
        blocks.append(jnp.concatenate([top, bot], axis=1))
    w_c = jnp.concatenate(blocks, axis=2).astype(BF16).reshape(CMP_STRIDE * LANES, CMP_RATIO * LANES)
    pe_r = pe.reshape(2, CMP_RATIO, CMP_STRIDE, HEAD_DIM)
    pe_c = jnp.concatenate([pe_r[0], pe_r[1]], axis=-1)
    return w_c, pe_c


def _cmp_from_y(y0, y1, const):
    body = y0[:-1, :] + y1[1:, :] + const
    return jnp.concatenate([body, jnp.zeros((1, LANES), F32)], axis=0)


def _kv_operands(kv):
    lane = lax.broadcasted_iota(jnp.int32, kv.shape, 1)
    v1 = jnp.where(lane < HEAD_DIM, pltpu.roll(kv, HEAD_DIM, 1), 1.0)
    return kv.astype(BF16), v1.astype(BF16)


def _stack_heads(qblk):
    q = qblk.astype(F32)
    lane = lax.broadcasted_iota(jnp.int32, (q.shape[0], LANES), 1)
    outs = []
    for r in range(GQA_REP):
        grp = q[:, (r // 2) * LANES:(r // 2 + 1) * LANES]
        if r % 2:
            grp = pltpu.roll(grp, HEAD_DIM, 1)
        outs.append(jnp.where(lane < HEAD_DIM, grp, 0.0))
    return jnp.concatenate(outs, axis=0).astype(BF16)


def _overlap_matrix(n_cmp_pad, n_sel_pad):
    n = lax.broadcasted_iota(jnp.int32, (n_cmp_pad, n_sel_pad), 0) * CMP_STRIDE
    j = lax.broadcasted_iota(jnp.int32, (n_cmp_pad, n_sel_pad), 1) * SEL_BLOCK
    return jnp.where((n < j + SEL_BLOCK) & (n + CMP_BLOCK > j), 1.0, 0.0).astype(BF16)


def _overlap_matrix_t(n_sel_pad, n_cmp_pad):
    j = lax.broadcasted_iota(jnp.int32, (n_sel_pad, n_cmp_pad), 0) * SEL_BLOCK
    n = lax.broadcasted_iota(jnp.int32, (n_sel_pad, n_cmp_pad), 1) * CMP_STRIDE
    return jnp.where((n < j + SEL_BLOCK) & (n + CMP_BLOCK > j), 1.0, 0.0).astype(BF16)


def _kv_operands_t(kvt):
    v1 = jnp.concatenate([kvt[HEAD_DIM:], jnp.ones((HEAD_DIM, kvt.shape[1]), F32)], axis=0)
    return kvt.astype(BF16), v1.astype(BF16)


def _topk_axis0(score, n_top):
    jf = lax.broadcasted_iota(jnp.int32, score.shape, 0).astype(F32)
    big = float(score.shape[0])
    sel = jnp.zeros(score.shape, F32)
    for _ in range(n_top):
        m = jnp.max(score, axis=0, keepdims=True)
        first = jnp.min(jnp.where(score == m, jf, big), axis=0, keepdims=True)
        pick = jf == first
        sel = jnp.where(pick & (m > 0.5 * NEG_BIG), 1.0, sel)
        score = jnp.where(pick, REMOVED, score)
    return sel


def _nsa_prompt_body(kt, qn_ref, qr_ref, gl_ref, y_ref, const_ref, kvs_ref, kvw_ref, o_ref,
                     kc_sc, vc_sc, ks_sc, vs_sc, kw_sc, vw_sc):
    qt = pl.program_id(2)
    tq = qn_ref.shape[1]
    t_len = kvs_ref.shape[3]
    n_cmp_pad = y_ref.shape[1]
    n_sel_pad = t_len // SEL_BLOCK
    rows = GQA_REP * tq

    @pl.when(qt == 0)
    def _():
        y = y_ref[0]
        kv_cmp = _cmp_from_y(y[:, 0:LANES], y[:, LANES:2 * LANES], const_ref[0:1, :])
        kc_sc[...], vc_sc[...] = _kv_operands(kv_cmp)
        kop, vs_sc[...] = _kv_operands_t(kvs_ref[0, 0])
        ks_sc[0:LANES, :] = kop
        j = lax.broadcasted_iota(jnp.int32, (n_sel_pad, t_len), 0)
        tok = lax.broadcasted_iota(jnp.int32, (n_sel_pad, t_len), 1)
        ks_sc[LANES:, :] = jnp.where(j == _div_pow2(tok & (kt - 1), SEL_BLOCK), 1.0, 0.0).astype(BF16)
        kw_sc[...], vw_sc[...] = _kv_operands_t(kvw_ref[0, 0])

    def cmp_and_select(qn_blk, tile):
        t0 = tile * tq
        pos_rows = t0 + lax.broadcasted_iota(jnp.int32, (tq, 1), 0)
        pos_rows4 = jnp.concatenate([pos_rows] * GQA_REP, axis=0)
        s = _dot_nt(_stack_heads(qn_blk), kc_sc[...])
        cmp_end = lax.broadcasted_iota(jnp.int32, (1, n_cmp_pad), 1) * CMP_STRIDE + (CMP_BLOCK - 1)
        vis = cmp_end <= pos_rows4
        s = jnp.where(vis, s, NEG_BIG)
        m = jnp.max(s, axis=-1, keepdims=True)
        p = jnp.where(vis, jnp.exp2(s - m), 0.0)
        p = p / jnp.maximum(jnp.sum(p, axis=-1, keepdims=True), 1e-30)
        acc = _dot(p.astype(BF16), vc_sc[...])
        psum = p[0:tq]
        for r in range(1, GQA_REP):
            psum = psum + p[r * tq:(r + 1) * tq]
        ov_t = _overlap_matrix_t(n_sel_pad, n_cmp_pad)
        hi, mid, lo = _split3(psum)
        imp_t = _dot_nt(ov_t, hi) + _dot_nt(ov_t, mid) + _dot_nt(ov_t, lo)
        blk = lax.broadcasted_iota(jnp.int32, (n_sel_pad, tq), 0)
        pos_l = t0 + lax.broadcasted_iota(jnp.int32, (1, tq), 1)
        cur = _div_pow2(pos_l, SEL_BLOCK)
        valid = blk * SEL_BLOCK <= pos_l
        forced = (blk == 0) | (blk == cur) | (blk == cur - 1)
        n_forced = 3
        score = jnp.where(valid & jnp.logical_not(forced), imp_t, jnp.where(valid, REMOVED, NEG_BIG))
        sel_t = _topk_axis0(score, min(N_SELECT, n_sel_pad) - n_forced)
        sel_t = jnp.where(valid & forced, 1.0, sel_t)
        return acc, jnp.where(sel_t.T > 0.5, 0.0, NEG_BIG)

    q0 = qt * tq
    pos = q0 + lax.broadcasted_iota(jnp.int32, (tq, 1), 0)
    pos4 = jnp.concatenate([pos] * GQA_REP, axis=0)
    qr = _stack_heads(qr_ref[0])

    wlen = WINDOW + tq
    w0 = pl.multiple_of(jnp.maximum(q0 - WINDOW, 0), tq)
    s_w = _dot(qr, kw_sc[:, pl.ds(w0, wlen)])
    dist = pos4 - (w0 + lax.broadcasted_iota(jnp.int32, (1, wlen), 1))
    ok_w = (dist >= 0) & (dist < WINDOW)
    s_w = jnp.where(ok_w, s_w, NEG_BIG)
    m_w = jnp.max(s_w, axis=-1, keepdims=True)
    p_w = jnp.where(ok_w, jnp.exp2(s_w - m_w), 0.0)
    acc_w = _dot_nt(p_w.astype(BF16), vw_sc[:, pl.ds(w0, wlen)])

    acc_c, sel_bias = cmp_and_select(qn_ref[0], qt)

    def sel_tile(k0, m_i, acc, causal):
        shift = (n_sel_pad - k0 // SEL_BLOCK) & (n_sel_pad - 1)
        bias = pltpu.roll(sel_bias, shift, 1).astype(BF16)
        lhs = jnp.concatenate([qr, jnp.concatenate([bias] * GQA_REP, axis=0)], axis=1)
        s_i = _dot(lhs, ks_sc[:, pl.ds(k0, kt)])
        if causal:
            tok = k0 + lax.broadcasted_iota(jnp.int32, (1, kt), 1)
            s_i = jnp.where(tok <= pos4, s_i, NEG_BIG)
        m_new = jnp.maximum(m_i, jnp.max(s_i, axis=-1, keepdims=True))
        p_i = jnp.exp2(s_i - m_new)
        acc = acc * jnp.exp2(m_i - m_new) + _dot_nt(p_i.astype(BF16), vs_sc[:, pl.ds(k0, kt)])
        return m_new, acc

    assert n_sel_pad & (n_sel_pad - 1) == 0 and kt // SEL_BLOCK <= n_sel_pad

    def past_tiles(i, c, n):
        for u in range(n):
            c = sel_tile(pl.multiple_of((i * n + u) * kt, kt), c[0], c[1], False)
        return c

    n_full = q0 // kt
    carry = (jnp.full((rows, 1), NEG_BIG, F32), jnp.zeros((rows, LANES), F32))
    carry = lax.fori_loop(0, n_full // 4, lambda i, c: past_tiles(i, c, 4), carry)
    carry = lax.fori_loop((n_full // 4) * 2, n_full // 2, lambda i, c: past_tiles(i, c, 2), carry)
    carry = lax.fori_loop((n_full // 2) * 2, n_full, lambda i, c: past_tiles(i, c, 1), carry)
    _, acc_s = sel_tile(pl.multiple_of(n_full * kt, kt), carry[0], carry[1], True)

    def norm(acc):
        return acc / jnp.maximum(acc[:, HEAD_DIM:HEAD_DIM + 1], 1e-30)

    o_c, o_s, o_w = acc_c, norm(acc_s), norm(acc_w)
    gate = _sigmoid(gl_ref[0])
    lane = lax.broadcasted_iota(jnp.int32, (tq, LANES), 1)
    heads = []
    for r in range(GQA_REP):
        rs = slice(r * tq, (r + 1) * tq)
        heads.append(gate[:, 3 * r:3 * r + 1] * o_c[rs] + gate[:, 3 * r + 1:3 * r + 2] * o_s[rs]
                     + gate[:, 3 * r + 2:3 * r + 3] * o_w[rs])
    for j in range(GQA_REP // 2):
        o_ref[0, :, j * LANES:(j + 1) * LANES] = jnp.where(
            lane < HEAD_DIM, heads[2 * j], pltpu.roll(heads[2 * j + 1], HEAD_DIM, 1))


def _nsa_prompt(qn, qr, gl, y, const, kvs_t, kvw_t, tq, kt):
    b, _, _, t = kvs_t.shape
    n_cmp_pad = y.shape[1]
    n_sel_pad = t // SEL_BLOCK
    gw = GQA_REP * HEAD_DIM
    qspec = pl.BlockSpec((1, tq, gw), lambda i, g, j: (i, j, g))
    res = pl.BlockSpec((1, 1, LANES, t), lambda i, g, j: (i, g, 0, 0))
    return pl.pallas_call(
        functools.partial(_nsa_prompt_body, kt),
        grid=(b, NSA_KV_HEADS, t // tq),
        in_specs=[qspec, qspec, pl.BlockSpec((1, tq, LANES), lambda i, g, j: (i, j, g)),
                  pl.BlockSpec((1, n_cmp_pad, CMP_RATIO * LANES), lambda i, g, j: (i, 0, g)),
                  pl.BlockSpec((8, LANES), lambda i, g, j: (0, 0)), res, res],
        out_specs=qspec,
        out_shape=jax.ShapeDtypeStruct((b, t, NSA_W), F32),
        scratch_shapes=[pltpu.VMEM((n_cmp_pad, LANES), BF16), pltpu.VMEM((n_cmp_pad, LANES), BF16),
                        pltpu.VMEM((LANES + n_sel_pad, t), BF16)] + [pltpu.VMEM((LANES, t), BF16)] * 3,
        compiler_params=_cparams(("arbitrary", "arbitrary", "arbitrary")),
        name="nsa_prompt",
    )(qn, qr, gl, y, const, kvs_t, kvw_t)


def _nsa_sample_cmp_body(past, q_ref, y_ref, const_ref, idx_ref, oc_ref):
    sb = q_ref.shape[0]
    n_cmp_pad = y_ref.shape[1]
    n_sel = -(-(past + 1) // SEL_BLOCK)
    n_sel_pad = idx_ref.shape[3]
    n_rows = sb * NSA_KV_HEADS
    ov = _overlap_matrix(n_cmp_pad, n_sel_pad)
    cmp_end = lax.broadcasted_iota(jnp.int32, (1, n_cmp_pad), 1) * CMP_STRIDE + (CMP_BLOCK - 1)
    vis = cmp_end <= past
    row = lax.broadcasted_iota(jnp.int32, (n_rows, n_cmp_pad), 0)
    psum_all = jnp.zeros((n_rows, n_cmp_pad), F32)
    for s_i in range(sb):
        y = y_ref[s_i]
        for g in range(NSA_KV_HEADS):
            c0 = g * CMP_RATIO * LANES
            kv_cmp = _cmp_from_y(y[:, c0:c0 + LANES], y[:, c0 + LANES:c0 + 2 * LANES], const_ref[0:1, :])
            kc, vc = _kv_operands(kv_cmp)
            s = _dot_nt(q_ref[s_i, g], kc)
            s = jnp.where(vis, s, NEG_BIG)
            m = jnp.max(s, axis=-1, keepdims=True)
            p = jnp.where(vis, jnp.exp2(s - m), 0.0)
            p = p / jnp.maximum(jnp.sum(p, axis=-1, keepdims=True), 1e-30)
            oc_ref[s_i, g] = _dot(p.astype(BF16), vc)
            psum = jnp.sum(p[0:GQA_REP], axis=0, keepdims=True)
            psum_all = jnp.where(row == s_i * NSA_KV_HEADS + g, psum, psum_all)
    hi, mid, lo = _split3(psum_all)
    imp = _dot(hi, ov) + _dot(mid, ov) + _dot(lo, ov)
    blk = lax.broadcasted_iota(jnp.int32, (n_rows, n_sel_pad), 1)
    cur = past // SEL_BLOCK
    valid = (blk * SEL_BLOCK <= past) & (blk < n_sel)
    forced = (blk == 0) | (blk == cur) | (blk == cur - 1)
    jf = blk.astype(F32)
    score = jnp.where(valid, jnp.where(forced, FORCED_SCORE, imp), NEG_BIG)
    idx = jnp.full((n_rows, n_sel_pad), -1.0, F32)
    for k in range(min(N_SELECT, n_sel)):
        mx = jnp.max(score, axis=-1, keepdims=True)
        first = jnp.min(jnp.where(score == mx, jf, float(n_sel_pad)), axis=-1, keepdims=True)
        idx = jnp.where((blk == k) & (mx > 0.5 * NEG_BIG), first, idx)
        score = jnp.where(jf == first, REMOVED, score)
    idx = idx.astype(jnp.int32)
    for s_i in range(sb):
        for g in range(NSA_KV_HEADS):
            r = s_i * NSA_KV_HEADS + g
            idx_ref[s_i, g] = jnp.broadcast_to(idx[r:r + 1, :], (8, n_sel_pad))


def _nsa_sample_cmp(past, q8, y, const, n_sel_pad, sb):
    b = q8.shape[0]
    n_cmp_pad = y.shape[1]
    blk4 = lambda w: pl.BlockSpec((sb, NSA_KV_HEADS, 8, w), lambda i: (i, 0, 0, 0))
    return pl.pallas_call(
        functools.partial(_nsa_sample_cmp_body, past),
        grid=(b // sb,),
        in_specs=[blk4(LANES), pl.BlockSpec((sb, n_cmp_pad, y.shape[2]), lambda i: (i, 0, 0)),
                  pl.BlockSpec((8, LANES), lambda i: (0, 0))],
        out_specs=[blk4(n_sel_pad), blk4(LANES)],
        out_shape=[jax.ShapeDtypeStruct((b, NSA_KV_HEADS, 8, n_sel_pad), jnp.int32),
                   jax.ShapeDtypeStruct((b, NSA_KV_HEADS, 8, LANES), F32)],
        compiler_params=_cparams(("arbitrary",)),
        name="nsa_sample_cmp",
    )(q8, y, const)


def _nsa_sample_sel_body(past, n_top, *refs):
    idx_ref, pt_ref = refs[0], refs[1]
    del pt_ref
    blk_refs = refs[2:2 + n_top]
    q_ref, new_s_ref, new_w_ref, win_ref, oc_ref, gl_ref, o_ref = refs[2 + n_top:]
    b, g = pl.program_id(0), pl.program_id(1)
    q = q_ref[0, 0]
    n_past_blocks = past // SEL_BLOCK

    def attend(pieces):
        ms = [jnp.max(jnp.where(ok, s, NEG_BIG), axis=-1, keepdims=True) for s, ok, _, _ in pieces]
        m = functools.reduce(jnp.maximum, ms)
        acc = jnp.zeros((8, LANES), F32)
        for s, ok, v, v_t in pieces:
            p = jnp.where(ok, jnp.exp2(s - m), 0.0).astype(BF16)
            acc = acc + (_dot_nt(p, v) if v_t else _dot(p, v))
        return acc

    def new_token(ref, enabled):
        kop, vop = _kv_operands(jnp.broadcast_to(ref[0, 0], (8, LANES)))
        first = lax.broadcasted_iota(jnp.int32, (8, 8), 1) == 0
        return _dot_nt(q, kop), first & enabled, vop, False

    per_page = PAGE_SIZE // SEL_BLOCK
    tok = lax.broadcasted_iota(jnp.int32, (8, n_top * PAGE_SIZE), 1)
    ok = tok < 0
    has_new = jnp.zeros((8, 8), jnp.int32)
    for k in range(n_top):
        j = idx_ref[(b * NSA_KV_HEADS + g) * n_top + k]
        lo = k * PAGE_SIZE + jnp.where((j >= 0) & (j < n_past_blocks), (j % per_page) * SEL_BLOCK, PAGE_SIZE)
        ok = ok | ((tok >= lo) & (tok < lo + SEL_BLOCK) & (tok < (k + 1) * PAGE_SIZE))
        has_new = has_new + jnp.where(j == n_past_blocks, 1, 0)
    kop, vop = _kv_operands_t(jnp.concatenate([blk_refs[k][...] for k in range(n_top)], axis=1))
    acc_s = attend([(_dot(q, kop), ok, vop, True), new_token(new_s_ref, has_new > 0)])

    win = win_ref[0]
    w_buf = win.shape[1]
    kw, vw = _kv_operands_t(win)
    dist = w_buf - lax.broadcasted_iota(jnp.int32, (8, w_buf), 1)
    ok_buf = (dist < WINDOW) & (past - dist >= 0)
    acc_w = attend([(_dot(q, kw), ok_buf, vw, True), new_token(new_w_ref, jnp.ones((8, 8), jnp.int32) > 0)])

    def norm(acc):
        return acc / jnp.maximum(acc[:, HEAD_DIM:HEAD_DIM + 1], 1e-30)

    gate = _sigmoid(gl_ref[0, 0])
    rsel = lax.broadcasted_iota(jnp.int32, (8, LANES), 0)
    gc = jnp.zeros((8, LANES), F32)
    gs = jnp.zeros((8, LANES), F32)
    gw = jnp.zeros((8, LANES), F32)
    for r in range(GQA_REP):
        gc = jnp.where(rsel == r, gate[:, 3 * r:3 * r + 1], gc)
        gs = jnp.where(rsel == r, gate[:, 3 * r + 1:3 * r + 2], gs)
        gw = jnp.where(rsel == r, gate[:, 3 * r + 2:3 * r + 3], gw)
    o_ref[0, 0] = gc * oc_ref[0, 0] + gs * norm(acc_s) + gw * norm(acc_w)


def _nsa_sample_sel(past, layer, idx, page_table, sel_pool, q8, new_s, new_w, win, oc, gl):
    b, n_past_pages = page_table.shape
    n_top = idx.shape[0] // (b * NSA_KV_HEADS)
    n_past_blocks = past // SEL_BLOCK
    per_page = PAGE_SIZE // SEL_BLOCK

    def blk_spec(k):
        def imap(i, g, idx_ref, pt_ref):
            j = jnp.clip(idx_ref[(i * NSA_KV_HEADS + g) * n_top + k], 0, n_past_blocks - 1)
            return (layer, pt_ref[i * n_past_pages + j // per_page], g, 0, 0)
        return pl.BlockSpec((None, None, None, LANES, PAGE_SIZE), imap)

    v4 = lambda w: pl.BlockSpec((1, 1, 8, w), lambda i, g, a, c: (i, g, 0, 0))
    v1 = pl.BlockSpec((1, 1, 1, LANES), lambda i, g, a, c: (i, g, 0, 0))
    grid_spec = pltpu.PrefetchScalarGridSpec(
        num_scalar_prefetch=2,
        grid=(b, NSA_KV_HEADS),
        in_specs=[blk_spec(k) for k in range(n_top)] + [
            v4(LANES), v1, v1,
            pl.BlockSpec((None, None, 1, LANES, win.shape[4]), lambda i, g, a, c: (layer, i, g, 0, 0)),
            v4(LANES), v1],
        out_specs=v4(LANES),
    )
    return pl.pallas_call(
        functools.partial(_nsa_sample_sel_body, past, n_top),
        grid_spec=grid_spec,
        out_shape=jax.ShapeDtypeStruct((b, NSA_KV_HEADS, 8, LANES), F32),
        compiler_params=_cparams(("arbitrary", "arbitrary")),
        name="nsa_sample_sel",
    )(idx, page_table.reshape(-1), *([sel_pool] * n_top), q8, new_s, new_w, win, oc, gl)


def _win_update_body(win_ref, new_ref, o_ref):
    w_buf = win_ref.shape[3]
    last = lax.broadcasted_iota(jnp.int32, (LANES, LANES), 1) == LANES - 1
    for g in range(NSA_KV_HEADS):
        shifted = pltpu.roll(win_ref[0, g], w_buf - 1, 1)
        new_col = jnp.broadcast_to(new_ref[0, g], (LANES, LANES)).T
        o_ref[0, g, :, 0:w_buf - LANES] = shifted[:, 0:w_buf - LANES]
        o_ref[0, g, :, w_buf - LANES:] = jnp.where(last, new_col, shifted[:, w_buf - LANES:])


def _win_update(layer, win, new):
    _, b, g, r, w_buf = win.shape
    return pl.pallas_call(
        _win_update_body,
        grid=(b,),
        in_specs=[pl.BlockSpec((None, 1, g, r, w_buf), lambda i: (layer, i, 0, 0, 0)),
                  pl.BlockSpec((1, g, 1, r), lambda i: (i, 0, 0, 0))],
        out_specs=pl.BlockSpec((1, g, r, w_buf), lambda i: (i, 0, 0, 0)),
        out_shape=jax.ShapeDtypeStruct((b, g, r, w_buf), F32),
        compiler_params=_cparams(("arbitrary",)),
        name="win_update",
    )(win, new)


def _finish_body(x_ref, ho_ref, no_ref, wo_ref, g1_ref, g2_ref, g3_ref, w1_ref, w2_ref, o_ref):
    wo = wo_ref[...]
    y = _dot(ho_ref[...].astype(BF16), wo[0:HG_W]) + _dot(no_ref[...].astype(BF16), wo[HG_W:])
    h = x_ref[...] + _rms(y, g1_ref[...])
    a = _dot(_rms(h, g2_ref[...]).astype(BF16), w1_ref[...])
    dff = a.shape[1] // 2
    u, v = a[:, :dff], a[:, dff:]
    act = (u * _sigmoid(u)) * v
    y2 = _dot(act.astype(BF16), w2_ref[...])
    o_ref[...] = h + _rms(y2, g3_ref[...])


def _finish(x2d, ho, no, wo, g1, g2, g3, w1, w2, tm):
    m, d = x2d.shape
    row = lambda w: pl.BlockSpec((tm, w), lambda i: (i, 0))
    full = lambda a: pl.BlockSpec(a.shape, lambda i: (0, 0), pipeline_mode=pl.Buffered(1))
    return pl.pallas_call(
        _finish_body,
        grid=(m // tm,),
        in_specs=[row(d), row(HG_W), row(NSA_W), full(wo), full(g1), full(g2), full(g3), full(w1), full(w2)],
        out_specs=row(d),
        out_shape=jax.ShapeDtypeStruct((m, d), F32),
        compiler_params=_cparams(("arbitrary",)),
        name="finish",
    )(x2d, ho, no, wo, g1, g2, g3, w1, w2)


def _tile(n, pref):
    while n % pref:
        pref //= 2
    return pref


def kernel(x_prompt, x_sample, state_hgrn, cache_cmp_kv, cache_sel_kv, cache_win_kv, page_table,
           norm_mix_pre, norm_mix_post, norm_ffn_pre, norm_ffn_post, w_in, w_out, hg_lb_logits,
           hg_out_norm, cmp_pe, w_cmp, w_ffn_in, w_ffn_out):
    bp, tp, d = x_prompt.shape
    bs, ts, _ = x_sample.shape
    assert ts == 1
    depth = w_in.shape[0]
    n_pages = page_table.shape[1]
    past = n_pages * PAGE_SIZE
    assert tp % PAGE_SIZE == 0 and tp >= WINDOW + PAGE_SIZE

    def token_minor(cache):
        return cache.transpose(0, 1, 3, 4, 5, 2).reshape(cache.shape[0], cache.shape[1], NSA_KV_HEADS,
                                                        2 * HEAD_DIM, cache.shape[2])

    def token_major(a):
        return a.reshape(a.shape[0], NSA_KV_HEADS, 2, HEAD_DIM, a.shape[3]).transpose(0, 4, 1, 2, 3)

    tm_p = _tile(tp, 512)
    tabs_p = _rope_tables(jnp.arange(tp))
    tabs_s = _rope_tables(jnp.full((bs * ts,), past, jnp.int32))
    cmp_pool, sel_pool, win_pool = token_minor(cache_cmp_kv), token_minor(cache_sel_kv), token_minor(cache_win_kv)
    prompt_pages = jnp.zeros((bp, tp // PAGE_SIZE), jnp.int32)
    n_sel_s = -(-(past + ts) // SEL_BLOCK)
    n_sel_pad_s = -(-n_sel_s // LANES) * LANES
    n_top_s = min(N_SELECT, n_sel_s)

    xp = x_prompt.reshape(bp * tp, d)
    xs = x_sample.reshape(bs * ts, d)
    hgp, hgs, cmpp, cmps, selp, sels, winp, wins = [], [], [], [], [], [], [], []
    for l in range(depth):
        w_in_bf = _pad_w_in(w_in[l])
        w_out_bf = w_out[l].astype(BF16)
        w1_bf = w_ffn_in[l].astype(BF16)
        w2_bf = w_ffn_out[l].astype(BF16)
        g_pre, g_post = norm_mix_pre[l][None, :], norm_mix_post[l][None, :]
        g_fpre, g_fpost = norm_ffn_pre[l][None, :], norm_ffn_post[l][None, :]
        gain = hg_out_norm[l][None, :]
        w_c, pe_c = _compress_weights(w_cmp[l], cmp_pe[l])

        hq, fl, hi, hg, qn, qr, kvc_t, kvs_t, kvw_t, gl = _inproj(xp, g_pre, w_in_bf, tabs_p, tm_p, True)
        r3 = lambda a: a.reshape(bp, tp, a.shape[-1])
        ho, s_fin = _hgrn_prompt(l, hg_lb_logits, r3(hq), r3(fl), r3(hi), r3(hg), gain, _tile(tp, 256))
        y, const = _compress(kvc_t, lambda i, pg, pt: (i, 0, 0, pg), prompt_pages, w_c, pe_c,
                             _tile(tp // PAGE_SIZE, CMP_PAGES_PER_STEP))
        no = _nsa_prompt(r3(qn), r3(qr), r3(gl), y, const, kvs_t, kvw_t, PAGE_SIZE, _tile(tp, SEL_KV_TILE))
        xp = _finish(xp, ho.reshape(bp * tp, HG_W), no.reshape(bp * tp, NSA_W), w_out_bf, g_post, g_fpre,
                     g_fpost, w1_bf, w2_bf, tm_p)
        hgp.append(s_fin)
        cmpp.append(token_major(kvc_t))
        selp.append(token_major(kvs_t))
        winp.append(token_major(kvw_t[..., tp - min(WINDOW, tp):]))

        hq, fl, hi, hg, qn, qr, kvc, kvs, kvw, gl = _inproj(xs, g_pre, w_in_bf, tabs_s, bs * ts, False)
        v3 = lambda a: a.reshape(bs, 1, a.shape[-1])
        s_new, ho = _hgrn_sample(l, hg_lb_logits, state_hgrn[l], v3(hq), v3(fl), v3(hi), v3(hg), gain)
        y, const = _compress(cmp_pool, lambda i, pg, pt, l=l: (l, pt[i * n_pages + pg], 0, 0, 0),
                             page_table, w_c, pe_c, _tile(n_pages, CMP_PAGES_PER_STEP))

        def q8(q):
            q4 = q.reshape(bs, NSA_KV_HEADS, GQA_REP, HEAD_DIM)
            return jnp.pad(q4, ((0, 0), (0, 0), (0, 8 - GQA_REP), (0, LANES - HEAD_DIM)))

        idx, oc = _nsa_sample_cmp(past, q8(qn), y, const, n_sel_pad_s, _tile(bs, 4))
        g4 = lambda a: a.reshape(bs, NSA_KV_HEADS, 1, LANES)
        o8 = _nsa_sample_sel(past, l, idx[:, :, 0, :n_top_s].reshape(-1), page_table, sel_pool, q8(qr),
                             g4(kvs), g4(kvw), win_pool, oc, g4(gl))
        no = o8[:, :, :GQA_REP, :HEAD_DIM].reshape(bs * ts, NSA_W)
        win_new = _win_update(l, win_pool, g4(kvw))
        xs = _finish(xs, ho.reshape(bs * ts, HG_W), no, w_out_bf, g_post, g_fpre, g_fpost, w1_bf, w2_bf,
                     bs * ts)
        kv5 = lambda a, t: a.reshape(-1, t, NSA_KV_HEADS, 2, HEAD_DIM)
        hgs.append(s_new)
        cmps.append(kv5(kvc, ts))
        sels.append(kv5(kvs, ts))
        wins.append(token_major(win_new))

    return (xp.reshape(bp, tp, d), xs.reshape(bs, ts, d), jnp.stack(hgp), jnp.stack(hgs), jnp.stack(cmpp),
            jnp.stack(cmps), jnp.stack(selp), jnp.stack(sels), jnp.stack(winp), jnp.stack(wins))
```

```python
import functools

import jax
import jax.numpy as jnp
import numpy as np
from jax import lax
from jax.experimental import pallas as pl
from jax.experimental.pallas import tpu as pltpu

F32 = jnp.float32
BF16 = jnp.bfloat16

HG_HEADS = 4
HG_DK = 128
HG_DV = 128
HG_W = HG_HEADS * HG_DV
HEAD_DIM = 64
NSA_HEADS = 8
NSA_KV_HEADS = 2
GQA_REP = NSA_HEADS // NSA_KV_HEADS
NSA_W = NSA_HEADS * HEAD_DIM
KV_W = NSA_KV_HEADS * 2 * HEAD_DIM
N_GATES = NSA_HEADS * 3
CMP_STRIDE = 16
CMP_RATIO = 2
CMP_BLOCK = CMP_STRIDE * CMP_RATIO
SEL_BLOCK = 64
N_SELECT = 16
WINDOW = 512
PAGE_SIZE = 128
ROPE_THETA = 500000.0
ROT_DIM = HEAD_DIM // 4
EPS = 1e-6
FORCED_SCORE = 1e6
NEG_BIG = -1e30
LB_FLOOR = 1e-30
REMOVED = -3e38
LOG2_E = 1.4426950408889634

LANES = 128
HG_CHUNK = 16
GATE_PAD = 2 * LANES
CMP_PAGES_PER_STEP = 64
SEL_KV_TILE = 1024
COL_QN = 4 * HG_W
COL_KVC = COL_QN + NSA_W
COL_KVS = COL_KVC + KV_W
COL_KVW = COL_KVS + KV_W
COL_GL = COL_KVW + KV_W
IN_W_PAD = COL_GL + GATE_PAD

VMEM_LIMIT = 56 * 1024 * 1024


def _cparams(sem):
    return pltpu.CompilerParams(dimension_semantics=sem, vmem_limit_bytes=VMEM_LIMIT)


def _rms(x, g):
    return x * lax.rsqrt(jnp.mean(x * x, axis=-1, keepdims=True) + EPS) * g


def _sigmoid(x):
    return 1.0 / (1.0 + jnp.exp(-x))


def _dot(a, b):
    return jnp.dot(a, b, preferred_element_type=F32)


def _dot_nt(a, b):
    return lax.dot_general(a, b, (((1,), (1,)), ((), ())), preferred_element_type=F32)


def _dot_tn(a, b):
    return lax.dot_general(a, b, (((0,), (0,)), ((), ())), preferred_element_type=F32)


def _div_pow2(x, n):
    assert n & (n - 1) == 0
    return x >> (n.bit_length() - 1)


def _split3(x):
    hi = x.astype(BF16)
    r1 = x - hi.astype(F32)
    mid = r1.astype(BF16)
    lo = (r1 - mid.astype(F32)).astype(BF16)
    return hi, mid, lo


def _rope128(x, c, s1, s2):
    return x * c + pltpu.roll(x, LANES - ROT_DIM // 2, 1) * s1 + pltpu.roll(x, ROT_DIM // 2, 1) * s2


def _inproj_body(kv_t, x_ref, g_ref, w_ref, cq_ref, s1q_ref, s2q_ref,
                 hq_ref, fl_ref, hi_ref, hg_ref, qn_ref, qr_ref, kvc_ref, kvs_ref, kvw_ref, gl_ref):
    def put_kv(ref, j, blk):
        if kv_t:
            ref[0, j] = blk.T
        else:
            ref[:, j * LANES:(j + 1) * LANES] = blk

    hn = _rms(x_ref[...], g_ref[...])
    z = _dot(hn.astype(BF16), w_ref[...])
    hq_ref[...] = z[:, 0:HG_W]
    fl_ref[...] = z[:, HG_W:2 * HG_W]
    hi_ref[...] = z[:, 2 * HG_W:3 * HG_W]
    hg_ref[...] = z[:, 3 * HG_W:4 * HG_W]
    scale = HEAD_DIM ** -0.5 * LOG2_E
    cq, s1q, s2q = cq_ref[...], s1q_ref[...], s2q_ref[...]
    for j in range(NSA_W // LANES):
        blk = z[:, COL_QN + j * LANES:COL_QN + (j + 1) * LANES]
        qn_ref[:, j * LANES:(j + 1) * LANES] = (blk * scale).astype(BF16)
        qr_ref[:, j * LANES:(j + 1) * LANES] = (_rope128(blk, cq, s1q, s2q) * scale).astype(BF16)
    k_half = lax.broadcasted_iota(jnp.int32, cq.shape, 1) < HEAD_DIM
    ck, s1k, s2k = jnp.where(k_half, cq, 1.0), jnp.where(k_half, s1q, 0.0), jnp.where(k_half, s2q, 0.0)
    for j in range(NSA_KV_HEADS):
        cols = lambda c0: z[:, c0 + j * LANES:c0 + (j + 1) * LANES]
        put_kv(kvc_ref, j, cols(COL_KVC))
        put_kv(kvs_ref, j, _rope128(cols(COL_KVS), ck, s1k, s2k))
        put_kv(kvw_ref, j, _rope128(cols(COL_KVW), ck, s1k, s2k))
    gl_ref[...] = z[:, COL_GL:COL_GL + GATE_PAD]


def _inproj(x2d, gain, w_bf, tabs, tm, kv_t):
    m, d = x2d.shape
    n_pos_blocks = tabs[0].shape[0] // tm
    row = lambda i: (i, 0)
    tab = lambda i: (i % n_pos_blocks, 0)
    const = lambda i: (0, 0)
    widths = (HG_W, HG_W, HG_W, HG_W, NSA_W, NSA_W, GATE_PAD)
    dtypes = (F32, F32, F32, F32, BF16, BF16, F32)
    specs = [pl.BlockSpec((tm, w), row) for w in widths]
    shapes = [jax.ShapeDtypeStruct((m, w), dt) for w, dt in zip(widths, dtypes)]
    if kv_t:
        t = tabs[0].shape[0]
        kv_spec = pl.BlockSpec((1, NSA_KV_HEADS, LANES, tm), lambda i: (i // n_pos_blocks, 0, 0, i % n_pos_blocks))
        kv_shape = jax.ShapeDtypeStruct((m // t, NSA_KV_HEADS, LANES, t), F32)
    else:
        kv_spec = pl.BlockSpec((tm, KV_W), row)
        kv_shape = jax.ShapeDtypeStruct((m, KV_W), F32)
    return pl.pallas_call(
        functools.partial(_inproj_body, kv_t),
        grid=(m // tm,),
        in_specs=[pl.BlockSpec((tm, d), row), pl.BlockSpec((1, d), const),
                  pl.BlockSpec((d, IN_W_PAD), const, pipeline_mode=pl.Buffered(1))]
        + [pl.BlockSpec((tm, LANES), tab)] * 3,
        out_specs=specs[:6] + [kv_spec] * 3 + specs[6:],
        out_shape=shapes[:6] + [kv_shape] * 3 + shapes[6:],
        compiler_params=_cparams(("arbitrary",)),
        name="inproj",
    )(x2d, gain, w_bf, *tabs)


def _rope_tables(pos):
    half = ROT_DIM // 2
    inv = ROPE_THETA ** (-2.0 * jnp.arange(half, dtype=F32) / ROT_DIM)
    ang = pos.astype(F32)[:, None] * inv[None, :]
    cos, sin = jnp.cos(ang), jnp.sin(ang)
    p = pos.shape[0]
    one = jnp.ones((p, HEAD_DIM - ROT_DIM), F32)
    zero = jnp.zeros((p, HEAD_DIM - ROT_DIM), F32)
    zh = jnp.zeros((p, half), F32)
    c64 = jnp.concatenate([cos, cos, one], axis=1)
    s1_64 = jnp.concatenate([-sin, zh, zero], axis=1)
    s2_64 = jnp.concatenate([zh, sin, zero], axis=1)
    return tuple(jnp.concatenate([t, t], axis=1) for t in (c64, s1_64, s2_64))


def _pad_w_in(w):
    d = w.shape[0]
    per_g = GQA_REP * 3
    pad = jnp.zeros((d, LANES - per_g), w.dtype)
    gates = [jnp.concatenate([w[:, COL_GL + g * per_g:COL_GL + (g + 1) * per_g], pad], axis=1)
             for g in range(NSA_KV_HEADS)]
    return jnp.concatenate([w[:, :COL_GL]] + gates, axis=1).astype(BF16)


def _lower_bound(lbl_ref, layer):
    rows = [lbl_ref[l:l + 1, :] for l in range(lbl_ref.shape[0])]
    mx = functools.reduce(jnp.maximum, rows)
    ex = [jnp.exp(r - mx) for r in rows]
    den = functools.reduce(lambda a, b: a + b, ex)
    sm = [e / den for e in ex]
    cum = sm[0]
    for l in range(1, layer + 1):
        cum = cum + sm[l]
    return cum - sm[0]


def _forget_gate(fl, lb):
    logsig = jnp.minimum(fl, 0.0) - jnp.log1p(jnp.exp(-jnp.abs(fl)))
    a = jnp.log(jnp.maximum(lb, LB_FLOOR))
    b = jnp.log1p(-lb) + logsig
    logf = jnp.maximum(a, b) + jnp.log1p(jnp.exp(-jnp.abs(a - b)))
    kh = (1.0 - lb) * _sigmoid(-fl)
    return logf, kh


def _readout(o, g, gain):
    on = o * lax.rsqrt(jnp.mean(o * o, axis=-1, keepdims=True) + EPS) * gain
    return on * (g * _sigmoid(g))


def _hgrn_prompt_body(layer, lbl_ref, q_ref, f_ref, i_ref, g_ref, gain_ref, o_ref, s_ref,
                      st_sc, k_sc, b_sc, o_sc):
    t = pl.program_id(1)
    ct = q_ref.shape[1]

    @pl.when(t == 0)
    def _():
        st_sc[...] = jnp.zeros_like(st_sc)

    lb = _lower_bound(lbl_ref, layer)
    logf, kh = _forget_gate(f_ref[0], lb)
    k_sc[...] = kh
    ri = lax.broadcasted_iota(jnp.int32, (ct, ct), 0)
    ci = lax.broadcasted_iota(jnp.int32, (ct, ct), 1)
    tri = jnp.where((ci <= ri) & (_div_pow2(ci, HG_CHUNK) == _div_pow2(ri, HG_CHUNK)), 1.0, 0.0).astype(BF16)
    hi, mid, lo = _split3(logf)
    b_sc[...] = _dot(tri, hi) + _dot(tri, mid) + _dot(tri, lo)

    row = lax.broadcasted_iota(jnp.int32, (HG_CHUNK, HG_DK), 0)

    def chunk(c, carry):
        r0 = pl.multiple_of(c * HG_CHUNK, HG_CHUNK)
        for h in range(HG_HEADS):
            hl = slice(h * HG_DK, (h + 1) * HG_DK)
            q = q_ref[0, pl.ds(r0, HG_CHUNK), hl]
            k = k_sc[pl.ds(r0, HG_CHUNK), hl]
            v = i_ref[0, pl.ds(r0, HG_CHUNK), hl]
            b = b_sc[pl.ds(r0, HG_CHUNK), hl]
            st = st_sc[h]
            o = _dot_nt((q * jnp.exp(b)).astype(BF16), st.astype(BF16))
            for s in range(HG_CHUNK):
                w = jnp.where(row >= s, jnp.exp(b - b[s:s + 1, :]), 0.0)
                d = jnp.sum(q * (k[s:s + 1, :] * w), axis=-1, keepdims=True)
                o = o + d * v[s:s + 1, :]
            o_sc[pl.ds(r0, HG_CHUNK), hl] = o
            blast = b[HG_CHUNK - 1:HG_CHUNK, :]
            kdec = k * jnp.exp(blast - b)
            st_sc[h] = st * jnp.exp(blast) + _dot_tn(v.astype(BF16), kdec.astype(BF16))
        return carry

    lax.fori_loop(0, ct // HG_CHUNK, chunk, 0)

    gain = gain_ref[...]
    for h in range(HG_HEADS):
        hl = slice(h * HG_DV, (h + 1) * HG_DV)
        o_ref[0, :, hl] = _readout(o_sc[:, hl], g_ref[0, :, hl], gain)

    @pl.when(t == pl.num_programs(1) - 1)
    def _():
        for h in range(HG_HEADS):
            s_ref[0, h] = st_sc[h].T


def _hgrn_prompt(layer, lbl, hq, fl, hi, hg, gain, ct):
    b, t, _ = hq.shape
    tile = pl.BlockSpec((1, ct, HG_W), lambda i, j: (i, j, 0))
    return pl.pallas_call(
        functools.partial(_hgrn_prompt_body, layer),
        grid=(b, t // ct),
        in_specs=[pl.BlockSpec(lbl.shape, lambda i, j: (0, 0)), tile, tile, tile, tile,
                  pl.BlockSpec((1, HG_DV), lambda i, j: (0, 0))],
        out_specs=[tile, pl.BlockSpec((1, HG_HEADS, HG_DK, HG_DV), lambda i, j: (i, 0, 0, 0))],
        out_shape=[jax.ShapeDtypeStruct((b, t, HG_W), F32),
                   jax.ShapeDtypeStruct((b, HG_HEADS, HG_DK, HG_DV), F32)],
        scratch_shapes=[pltpu.VMEM((HG_HEADS, HG_DV, HG_DK), F32), pltpu.VMEM((ct, HG_W), F32),
                        pltpu.VMEM((ct, HG_W), F32), pltpu.VMEM((ct, HG_W), F32)],
        compiler_params=_cparams(("arbitrary", "arbitrary")),
        name="hgrn_prompt",
    )(lbl, hq, fl, hi, hg, gain)


def _hgrn_sample_body(layer, lbl_ref, s0_ref, q_ref, f_ref, i_ref, g_ref, gain_ref, s_ref, o_ref):
    lb = _lower_bound(lbl_ref, layer)
    logf, kh = _forget_gate(f_ref[0], lb)
    q, v, g = q_ref[0], i_ref[0], g_ref[0]
    gain = gain_ref[...]

    def col(x):
        return jnp.broadcast_to(x, (HG_DK, HG_DK)).T

    for h in range(HG_HEADS):
        hl = slice(h * HG_DK, (h + 1) * HG_DK)
        s_new = col(jnp.exp(logf[:, hl])) * s0_ref[0, h] + col(kh[:, hl]) * v[:, hl]
        s_ref[0, h] = s_new
        o = jnp.sum(col(q[:, hl]) * s_new, axis=0, keepdims=True)
        o_ref[0, :, hl] = _readout(o, g[:, hl], gain)


def _hgrn_sample(layer, lbl, s0, hq, fl, hi, hg, gain):
    b = s0.shape[0]
    vec = pl.BlockSpec((1, 1, HG_W), lambda i: (i, 0, 0))
    st = pl.BlockSpec((1, HG_HEADS, HG_DK, HG_DV), lambda i: (i, 0, 0, 0))
    return pl.pallas_call(
        functools.partial(_hgrn_sample_body, layer),
        grid=(b,),
        in_specs=[pl.BlockSpec(lbl.shape, lambda i: (0, 0)), st, vec, vec, vec, vec,
                  pl.BlockSpec((1, HG_DV), lambda i: (0, 0))],
        out_specs=[st, vec],
        out_shape=[jax.ShapeDtypeStruct(s0.shape, F32), jax.ShapeDtypeStruct((b, 1, HG_W), F32)],
        compiler_params=_cparams(("arbitrary",)),
        name="hgrn_sample",
    )(lbl, s0, hq, fl, hi, hg, gain)


def _compress_body(n_pages, *refs):
    page_refs = refs[1:1 + n_pages]
    w_ref, pe_ref, y_ref, const_ref = refs[1 + n_pages:]
    rows = PAGE_SIZE // CMP_STRIDE
    tok = lax.broadcasted_iota(jnp.int32, (PAGE_SIZE, PAGE_SIZE), 0)
    col = lax.broadcasted_iota(jnp.int32, (PAGE_SIZE, PAGE_SIZE), 1)
    perm = jnp.where(tok == CMP_STRIDE * (col & (rows - 1)) + _div_pow2(col, rows), 1.0, 0.0).astype(BF16)
    n_split = 4 if n_pages % 4 == 0 else 1
    per = n_pages // n_split
    for g in range(NSA_KV_HEADS):
        for h in range(n_split):
            page_ids = range(h * per, (h + 1) * per)
            pages = jnp.concatenate([page_refs[p][g].astype(BF16) for p in page_ids], axis=0)
            permuted = _dot(pages, perm)
            pieces = [permuted[i * LANES:(i + 1) * LANES, :].T for i in range(per)]
            lhs = jnp.concatenate(
                [jnp.concatenate([pc[c * rows:(c + 1) * rows, :] for pc in pieces], axis=0)
                 for c in range(CMP_STRIDE)], axis=1)
            y_ref[0, h * per * rows:(h + 1) * per * rows, g * CMP_RATIO * LANES:(g + 1) * CMP_RATIO * LANES] = (
                _dot(lhs.astype(BF16), w_ref[...]))

    @pl.when((pl.program_id(0) == 0) & (pl.program_id(1) == 0))
    def _():
        tot = jnp.zeros((8, LANES), F32)
        for c in range(CMP_STRIDE):
            for m in range(CMP_RATIO):
                pe_row = jnp.broadcast_to(pe_ref[m, c:c + 1, :], (8, LANES))
                tot = tot + _dot(pe_row.astype(BF16), w_ref[c * LANES:(c + 1) * LANES, m * LANES:(m + 1) * LANES])
        const_ref[...] = tot


def _compress(pages, page_index, page_table, w_c, pe_c, n_pages):
    s, p = page_table.shape
    rows = PAGE_SIZE // CMP_STRIDE
    steps = p // n_pages

    def page_spec(k):
        return pl.BlockSpec((None,) * (pages.ndim - 3) + (NSA_KV_HEADS, LANES, PAGE_SIZE),
                            lambda i, j, pt: page_index(i, j * n_pages + k, pt))

    page_specs = [page_spec(k) for k in range(n_pages)]
    grid_spec = pltpu.PrefetchScalarGridSpec(
        num_scalar_prefetch=1,
        grid=(s, steps),
        in_specs=page_specs + [
            pl.BlockSpec(w_c.shape, lambda i, j, pt: (0, 0)),
            pl.BlockSpec(pe_c.shape, lambda i, j, pt: (0, 0, 0))],
        out_specs=[pl.BlockSpec((1, n_pages * rows, NSA_KV_HEADS * CMP_RATIO * LANES),
                                lambda i, j, pt: (i, j, 0)),
                   pl.BlockSpec((8, LANES), lambda i, j, pt: (0, 0))],
    )
    return pl.pallas_call(
        functools.partial(_compress_body, n_pages),
        grid_spec=grid_spec,
        out_shape=[jax.ShapeDtypeStruct((s, p * rows, NSA_KV_HEADS * CMP_RATIO * LANES), F32),
                   jax.ShapeDtypeStruct((8, LANES), F32)],
        compiler_params=_cparams(("arbitrary", "arbitrary")),
        name="compress",
    )(page_table.reshape(-1), *([pages] * n_pages), w_c, pe_c)


def _compress_weights(w_cmp, pe):
    w_r = w_cmp.reshape(2, CMP_RATIO, CMP_STRIDE, HEAD_DIM, HEAD_DIM)
    z = jnp.zeros((CMP_STRIDE, HEAD_DIM, HEAD_DIM), w_cmp.dtype)
    blocks = []
    for m in range(CMP_RATIO):
        top = jnp.concatenate([w_r[0, m], z], axis=2)
        bot = jnp.concatenate([z, w_r[1, m]], axis=2)
        blocks.append(jnp.concatenate([top, bot], axis=1))
    w_c = jnp.concatenate(blocks, axis=2).astype(BF16).reshape(CMP_STRIDE * LANES, CMP_RATIO * LANES)
    pe_r = pe.reshape(2, CMP_RATIO, CMP_STRIDE, HEAD_DIM)
    pe_c = jnp.concatenate([pe_r[0], pe_r[1]], axis=-1)
    return w_c, pe_c


def _cmp_from_y(y0, y1, const):
    body = y0[:-1, :] + y1[1:, :] + const
    return jnp.concatenate([body, jnp.zeros((1, LANES), F32)], axis=0)


def _kv_operands(kv):
    lane = lax.broadcasted_iota(jnp.int32, kv.shape, 1)
    v1 = jnp.where(lane < HEAD_DIM, pltpu.roll(kv, HEAD_DIM, 1), 1.0)
    return kv.astype(BF16), v1.astype(BF16)


def _stack_heads(qblk):
    q = qblk.astype(F32)
    lane = lax.broadcasted_iota(jnp.int32, (q.shape[0], LANES), 1)
    outs = []
    for r in range(GQA_REP):
        grp = q[:, (r // 2) * LANES:(r // 2 + 1) * LANES]
        if r % 2:
            grp = pltpu.roll(grp, HEAD_DIM, 1)
        outs.append(jnp.where(lane < HEAD_DIM, grp, 0.0))
    return jnp.concatenate(outs, axis=0).astype(BF16)


def _overlap_matrix(n_cmp_pad, n_sel_pad):
    n = lax.broadcasted_iota(jnp.int32, (n_cmp_pad, n_sel_pad), 0) * CMP_STRIDE
    j = lax.broadcasted_iota(jnp.int32, (n_cmp_pad, n_sel_pad), 1) * SEL_BLOCK
    return jnp.where((n < j + SEL_BLOCK) & (n + CMP_BLOCK > j), 1.0, 0.0).astype(BF16)


def _overlap_matrix_t(n_sel_pad, n_cmp_pad):
    j = lax.broadcasted_iota(jnp.int32, (n_sel_pad, n_cmp_pad), 0) * SEL_BLOCK
    n = lax.broadcasted_iota(jnp.int32, (n_sel_pad, n_cmp_pad), 1) * CMP_STRIDE
    return jnp.where((n < j + SEL_BLOCK) & (n + CMP_BLOCK > j), 1.0, 0.0).astype(BF16)


def _kv_operands_t(kvt):
    v1 = jnp.concatenate([kvt[HEAD_DIM:], jnp.ones((HEAD_DIM, kvt.shape[1]), F32)], axis=0)
    return kvt.astype(BF16), v1.astype(BF16)


def _topk_axis0(score, n_top):
    jf = lax.broadcasted_iota(jnp.int32, score.shape, 0).astype(F32)
    big = float(score.shape[0])
    sel = jnp.zeros(score.shape, F32)
    for _ in range(n_top):
        m = jnp.max(score, axis=0, keepdims=True)
        first = jnp.min(jnp.where(score == m, jf, big), axis=0, keepdims=True)
        pick = jf == first
        sel = jnp.where(pick & (m > 0.5 * NEG_BIG), 1.0, sel)
        score = jnp.where(pick, REMOVED, score)
    return sel


def _nsa_prompt_body(kt, qn_ref, qr_ref, gl_ref, y_ref, const_ref, kvs_ref, kvw_ref, o_ref,
                     kc_sc, vc_sc, ks_sc, vs_sc, kw_sc, vw_sc):
    qt = pl.program_id(2)
    tq = qn_ref.shape[1]
    t_len = kvs_ref.shape[3]
    n_cmp_pad = y_ref.shape[1]
    n_sel_pad = t_len // SEL_BLOCK
    rows = GQA_REP * tq

    @pl.when(qt == 0)
    def _():
        y = y_ref[0]
        kv_cmp = _cmp_from_y(y[:, 0:LANES], y[:, LANES:2 * LANES], const_ref[0:1, :])
        kc_sc[...], vc_sc[...] = _kv_operands(kv_cmp)
        kop, vs_sc[...] = _kv_operands_t(kvs_ref[0, 0])
        ks_sc[0:LANES, :] = kop
        j = lax.broadcasted_iota(jnp.int32, (n_sel_pad, t_len), 0)
        tok = lax.broadcasted_iota(jnp.int32, (n_sel_pad, t_len), 1)
        ks_sc[LANES:, :] = jnp.where(j == _div_pow2(tok & (kt - 1), SEL_BLOCK), 1.0, 0.0).astype(BF16)
        kw_sc[...], vw_sc[...] = _kv_operands_t(kvw_ref[0, 0])

    def cmp_and_select(qn_blk, tile):
        t0 = tile * tq
        pos_rows = t0 + lax.broadcasted_iota(jnp.int32, (tq, 1), 0)
        pos_rows4 = jnp.concatenate([pos_rows] * GQA_REP, axis=0)
        s = _dot_nt(_stack_heads(qn_blk), kc_sc[...])
        cmp_end = lax.broadcasted_iota(jnp.int32, (1, n_cmp_pad), 1) * CMP_STRIDE + (CMP_BLOCK - 1)
        vis = cmp_end <= pos_rows4
        s = jnp.where(vis, s, NEG_BIG)
        m = jnp.max(s, axis=-1, keepdims=True)
        p = jnp.where(vis, jnp.exp2(s - m), 0.0)
        p = p / jnp.maximum(jnp.sum(p, axis=-1, keepdims=True), 1e-30)
        acc = _dot(p.astype(BF16), vc_sc[...])
        psum = p[0:tq]
        for r in range(1, GQA_REP):
            psum = psum + p[r * tq:(r + 1) * tq]
        ov_t = _overlap_matrix_t(n_sel_pad, n_cmp_pad)
        hi, mid, lo = _split3(psum)
        imp_t = _dot_nt(ov_t, hi) + _dot_nt(ov_t, mid) + _dot_nt(ov_t, lo)
        blk = lax.broadcasted_iota(jnp.int32, (n_sel_pad, tq), 0)
        pos_l = t0 + lax.broadcasted_iota(jnp.int32, (1, tq), 1)
        cur = _div_pow2(pos_l, SEL_BLOCK)
        valid = blk * SEL_BLOCK <= pos_l
        forced = (blk == 0) | (blk == cur) | (blk == cur - 1)
        n_forced = 3
        score = jnp.where(valid & jnp.logical_not(forced), imp_t, jnp.where(valid, REMOVED, NEG_BIG))
        sel_t = _topk_axis0(score, min(N_SELECT, n_sel_pad) - n_forced)
        sel_t = jnp.where(valid & forced, 1.0, sel_t)
        return acc, jnp.where(sel_t.T > 0.5, 0.0, NEG_BIG)

    q0 = qt * tq
    pos = q0 + lax.broadcasted_iota(jnp.int32, (tq, 1), 0)
    pos4 = jnp.concatenate([pos] * GQA_REP, axis=0)
    qr = _stack_heads(qr_ref[0])

    wlen = WINDOW + tq
    w0 = pl.multiple_of(jnp.maximum(q0 - WINDOW, 0), tq)
    s_w = _dot(qr, kw_sc[:, pl.ds(w0, wlen)])
    dist = pos4 - (w0 + lax.broadcasted_iota(jnp.int32, (1, wlen), 1))
    ok_w = (dist >= 0) & (dist < WINDOW)
    s_w = jnp.where(ok_w, s_w, NEG_BIG)
    m_w = jnp.max(s_w, axis=-1, keepdims=True)
    p_w = jnp.where(ok_w, jnp.exp2(s_w - m_w), 0.0)
    acc_w = _dot_nt(p_w.astype(BF16), vw_sc[:, pl.ds(w0, wlen)])

    acc_c, sel_bias = cmp_and_select(qn_ref[0], qt)

    def sel_tile(k0, m_i, acc, causal):
        shift = (n_sel_pad - k0 // SEL_BLOCK) & (n_sel_pad - 1)
        bias = pltpu.roll(sel_bias, shift, 1).astype(BF16)
        lhs = jnp.concatenate([qr, jnp.concatenate([bias] * GQA_REP, axis=0)], axis=1)
        s_i = _dot(lhs, ks_sc[:, pl.ds(k0, kt)])
        if causal:
            tok = k0 + lax.broadcasted_iota(jnp.int32, (1, kt), 1)
            s_i = jnp.where(tok <= pos4, s_i, NEG_BIG)
        m_new = jnp.maximum(m_i, jnp.max(s_i, axis=-1, keepdims=True))
        p_i = jnp.exp2(s_i - m_new)
        acc = acc * jnp.exp2(m_i - m_new) + _dot_nt(p_i.astype(BF16), vs_sc[:, pl.ds(k0, kt)])
        return m_new, acc

    assert n_sel_pad & (n_sel_pad - 1) == 0 and kt // SEL_BLOCK <= n_sel_pad

    def past_tiles(i, c, n):
        for u in range(n):
            c = sel_tile(pl.multiple_of((i * n + u) * kt, kt), c[0], c[1], False)
        return c

    n_full = q0 // kt
    carry = (jnp.full((rows, 1), NEG_BIG, F32), jnp.zeros((rows, LANES), F32))
    carry = lax.fori_loop(0, n_full // 4, lambda i, c: past_tiles(i, c, 4), carry)
    carry = lax.fori_loop((n_full // 4) * 2, n_full // 2, lambda i, c: past_tiles(i, c, 2), carry)
    carry = lax.fori_loop((n_full // 2) * 2, n_full, lambda i, c: past_tiles(i, c, 1), carry)
    _, acc_s = sel_tile(pl.multiple_of(n_full * kt, kt), carry[0], carry[1], True)

    def norm(acc):
        return acc / jnp.maximum(acc[:, HEAD_DIM:HEAD_DIM + 1], 1e-30)

    o_c, o_s, o_w = acc_c, norm(acc_s), norm(acc_w)
    gate = _sigmoid(gl_ref[0])
    lane = lax.broadcasted_iota(jnp.int32, (tq, LANES), 1)
    heads = []
    for r in range(GQA_REP):
        rs = slice(r * tq, (r + 1) * tq)
        heads.append(gate[:, 3 * r:3 * r + 1] * o_c[rs] + gate[:, 3 * r + 1:3 * r + 2] * o_s[rs]
                     + gate[:, 3 * r + 2:3 * r + 3] * o_w[rs])
    for j in range(GQA_REP // 2):
        o_ref[0, :, j * LANES:(j + 1) * LANES] = jnp.where(
            lane < HEAD_DIM, heads[2 * j], pltpu.roll(heads[2 * j + 1], HEAD_DIM, 1))


def _nsa_prompt(qn, qr, gl, y, const, kvs_t, kvw_t, tq, kt):
    b, _, _, t = kvs_t.shape
    n_cmp_pad = y.shape[1]
    n_sel_pad = t // SEL_BLOCK
    gw = GQA_REP * HEAD_DIM
    qspec = pl.BlockSpec((1, tq, gw), lambda i, g, j: (i, j, g))
    res = pl.BlockSpec((1, 1, LANES, t), lambda i, g, j: (i, g, 0, 0))
    return pl.pallas_call(
        functools.partial(_nsa_prompt_body, kt),
        grid=(b, NSA_KV_HEADS, t // tq),
        in_specs=[qspec, qspec, pl.BlockSpec((1, tq, LANES), lambda i, g, j: (i, j, g)),
                  pl.BlockSpec((1, n_cmp_pad, CMP_RATIO * LANES), lambda i, g, j: (i, 0, g)),
                  pl.BlockSpec((8, LANES), lambda i, g, j: (0, 0)), res, res],
        out_specs=qspec,
        out_shape=jax.ShapeDtypeStruct((b, t, NSA_W), F32),
        scratch_shapes=[pltpu.VMEM((n_cmp_pad, LANES), BF16), pltpu.VMEM((n_cmp_pad, LANES), BF16),
                        pltpu.VMEM((LANES + n_sel_pad, t), BF16)] + [pltpu.VMEM((LANES, t), BF16)] * 3,
        compiler_params=_cparams(("arbitrary", "arbitrary", "arbitrary")),
        name="nsa_prompt",
    )(qn, qr, gl, y, const, kvs_t, kvw_t)


def _nsa_sample_cmp_body(past, q_ref, y_ref, const_ref, idx_ref, oc_ref):
    sb = q_ref.shape[0]
    n_cmp_pad = y_ref.shape[1]
    n_sel = -(-(past + 1) // SEL_BLOCK)
    n_sel_pad = idx_ref.shape[3]
    n_rows = sb * NSA_KV_HEADS
    ov = _overlap_matrix(n_cmp_pad, n_sel_pad)
    cmp_end = lax.broadcasted_iota(jnp.int32, (1, n_cmp_pad), 1) * CMP_STRIDE + (CMP_BLOCK - 1)
    vis = cmp_end <= past
    row = lax.broadcasted_iota(jnp.int32, (n_rows, n_cmp_pad), 0)
    psum_all = jnp.zeros((n_rows, n_cmp_pad), F32)
    for s_i in range(sb):
        y = y_ref[s_i]
        for g in range(NSA_KV_HEADS):
            c0 = g * CMP_RATIO * LANES
            kv_cmp = _cmp_from_y(y[:, c0:c0 + LANES], y[:, c0 + LANES:c0 + 2 * LANES], const_ref[0:1, :])
            kc, vc = _kv_operands(kv_cmp)
            s = _dot_nt(q_ref[s_i, g], kc)
            s = jnp.where(vis, s, NEG_BIG)
            m = jnp.max(s, axis=-1, keepdims=True)
            p = jnp.where(vis, jnp.exp2(s - m), 0.0)
            p = p / jnp.maximum(jnp.sum(p, axis=-1, keepdims=True), 1e-30)
            oc_ref[s_i, g] = _dot(p.astype(BF16), vc)
            psum = jnp.sum(p[0:GQA_REP], axis=0, keepdims=True)
            psum_all = jnp.where(row == s_i * NSA_KV_HEADS + g, psum, psum_all)
    hi, mid, lo = _split3(psum_all)
    imp = _dot(hi, ov) + _dot(mid, ov) + _dot(lo, ov)
    blk = lax.broadcasted_iota(jnp.int32, (n_rows, n_sel_pad), 1)
    cur = past // SEL_BLOCK
    valid = (blk * SEL_BLOCK <= past) & (blk < n_sel)
    forced = (blk == 0) | (blk == cur) | (blk == cur - 1)
    jf = blk.astype(F32)
    score = jnp.where(valid, jnp.where(forced, FORCED_SCORE, imp), NEG_BIG)
    idx = jnp.full((n_rows, n_sel_pad), -1.0, F32)
    for k in range(min(N_SELECT, n_sel)):
        mx = jnp.max(score, axis=-1, keepdims=True)
        first = jnp.min(jnp.where(score == mx, jf, float(n_sel_pad)), axis=-1, keepdims=True)
        idx = jnp.where((blk == k) & (mx > 0.5 * NEG_BIG), first, idx)
        score = jnp.where(jf == first, REMOVED, score)
    idx = idx.astype(jnp.int32)
    for s_i in range(sb):
        for g in range(NSA_KV_HEADS):
            r = s_i * NSA_KV_HEADS + g
            idx_ref[s_i, g] = jnp.broadcast_to(idx[r:r + 1, :], (8, n_sel_pad))


def _nsa_sample_cmp(past, q8, y, const, n_sel_pad, sb):
    b = q8.shape[0]
    n_cmp_pad = y.shape[1]
    blk4 = lambda w: pl.BlockSpec((sb, NSA_KV_HEADS, 8, w), lambda i: (i, 0, 0, 0))
    return pl.pallas_call(
        functools.partial(_nsa_sample_cmp_body, past),
        grid=(b // sb,),
        in_specs=[blk4(LANES), pl.BlockSpec((sb, n_cmp_pad, y.shape[2]), lambda i: (i, 0, 0)),
                  pl.BlockSpec((8, LANES), lambda i: (0, 0))],
        out_specs=[blk4(n_sel_pad), blk4(LANES)],
        out_shape=[jax.ShapeDtypeStruct((b, NSA_KV_HEADS, 8, n_sel_pad), jnp.int32),
                   jax.ShapeDtypeStruct((b, NSA_KV_HEADS, 8, LANES), F32)],
        compiler_params=_cparams(("arbitrary",)),
        name="nsa_sample_cmp",
    )(q8, y, const)


def _nsa_sample_sel_body(past, n_top, *refs):
    idx_ref, pt_ref = refs[0], refs[1]
    del pt_ref
    blk_refs = refs[2:2 + n_top]
    q_ref, new_s_ref, new_w_ref, win_ref, oc_ref, gl_ref, o_ref = refs[2 + n_top:]
    b, g = pl.program_id(0), pl.program_id(1)
    q = q_ref[0, 0]
    n_past_blocks = past // SEL_BLOCK

    def attend(pieces):
        ms = [jnp.max(jnp.where(ok, s, NEG_BIG), axis=-1, keepdims=True) for s, ok, _, _ in pieces]
        m = functools.reduce(jnp.maximum, ms)
        acc = jnp.zeros((8, LANES), F32)
        for s, ok, v, v_t in pieces:
            p = jnp.where(ok, jnp.exp2(s - m), 0.0).astype(BF16)
            acc = acc + (_dot_nt(p, v) if v_t else _dot(p, v))
        return acc

    def new_token(ref, enabled):
        kop, vop = _kv_operands(jnp.broadcast_to(ref[0, 0], (8, LANES)))
        first = lax.broadcasted_iota(jnp.int32, (8, 8), 1) == 0
        return _dot_nt(q, kop), first & enabled, vop, False

    per_page = PAGE_SIZE // SEL_BLOCK
    tok = lax.broadcasted_iota(jnp.int32, (8, n_top * PAGE_SIZE), 1)
    ok = tok < 0
    has_new = jnp.zeros((8, 8), jnp.int32)
    for k in range(n_top):
        j = idx_ref[(b * NSA_KV_HEADS + g) * n_top + k]
        lo = k * PAGE_SIZE + jnp.where((j >= 0) & (j < n_past_blocks), (j % per_page) * SEL_BLOCK, PAGE_SIZE)
        ok = ok | ((tok >= lo) & (tok < lo + SEL_BLOCK) & (tok < (k + 1) * PAGE_SIZE))
        has_new = has_new + jnp.where(j == n_past_blocks, 1, 0)
    kop, vop = _kv_operands_t(jnp.concatenate([blk_refs[k][...] for k in range(n_top)], axis=1))
    acc_s = attend([(_dot(q, kop), ok, vop, True), new_token(new_s_ref, has_new > 0)])

    win = win_ref[0]
    w_buf = win.shape[1]
    kw, vw = _kv_operands_t(win)
    dist = w_buf - lax.broadcasted_iota(jnp.int32, (8, w_buf), 1)
    ok_buf = (dist < WINDOW) & (past - dist >= 0)
    acc_w = attend([(_dot(q, kw), ok_buf, vw, True), new_token(new_w_ref, jnp.ones((8, 8), jnp.int32) > 0)])

    def norm(acc):
        return acc / jnp.maximum(acc[:, HEAD_DIM:HEAD_DIM + 1], 1e-30)

    gate = _sigmoid(gl_ref[0, 0])
    rsel = lax.broadcasted_iota(jnp.int32, (8, LANES), 0)
    gc = jnp.zeros((8, LANES), F32)
    gs = jnp.zeros((8, LANES), F32)
    gw = jnp.zeros((8, LANES), F32)
    for r in range(GQA_REP):
        gc = jnp.where(rsel == r, gate[:, 3 * r:3 * r + 1], gc)
        gs = jnp.where(rsel == r, gate[:, 3 * r + 1:3 * r + 2], gs)
        gw = jnp.where(rsel == r, gate[:, 3 * r + 2:3 * r + 3], gw)
    o_ref[0, 0] = gc * oc_ref[0, 0] + gs * norm(acc_s) + gw * norm(acc_w)


def _nsa_sample_sel(past, layer, idx, page_table, sel_pool, q8, new_s, new_w, win, oc, gl):
    b, n_past_pages = page_table.shape
    n_top = idx.shape[0] // (b * NSA_KV_HEADS)
    n_past_blocks = past // SEL_BLOCK
    per_page = PAGE_SIZE // SEL_BLOCK

    def blk_spec(k):
        def imap(i, g, idx_ref, pt_ref):
            j = jnp.clip(idx_ref[(i * NSA_KV_HEADS + g) * n_top + k], 0, n_past_blocks - 1)
            return (layer, pt_ref[i * n_past_pages + j // per_page], g, 0, 0)
        return pl.BlockSpec((None, None, None, LANES, PAGE_SIZE), imap)

    v4 = lambda w: pl.BlockSpec((1, 1, 8, w), lambda i, g, a, c: (i, g, 0, 0))
    v1 = pl.BlockSpec((1, 1, 1, LANES), lambda i, g, a, c: (i, g, 0, 0))
    grid_spec = pltpu.PrefetchScalarGridSpec(
        num_scalar_prefetch=2,
        grid=(b, NSA_KV_HEADS),
        in_specs=[blk_spec(k) for k in range(n_top)] + [
            v4(LANES), v1, v1,
            pl.BlockSpec((None, None, 1, LANES, win.shape[4]), lambda i, g, a, c: (layer, i, g, 0, 0)),
            v4(LANES), v1],
        out_specs=v4(LANES),
    )
    return pl.pallas_call(
        functools.partial(_nsa_sample_sel_body, past, n_top),
        grid_spec=grid_spec,
        out_shape=jax.ShapeDtypeStruct((b, NSA_KV_HEADS, 8, LANES), F32),
        compiler_params=_cparams(("arbitrary", "arbitrary")),
        name="nsa_sample_sel",
    )(idx, page_table.reshape(-1), *([sel_pool] * n_top), q8, new_s, new_w, win, oc, gl)


def _win_update_body(win_ref, new_ref, o_ref):
    w_buf = win_ref.shape[3]
    last = lax.broadcasted_iota(jnp.int32, (LANES, LANES), 1) == LANES - 1
    for g in range(NSA_KV_HEADS):
        shifted = pltpu.roll(win_ref[0, g], w_buf - 1, 1)
        new_col = jnp.broadcast_to(new_ref[0, g], (LANES, LANES)).T
        o_ref[0, g, :, 0:w_buf - LANES] = shifted[:, 0:w_buf - LANES]
        o_ref[0, g, :, w_buf - LANES:] = jnp.where(last, new_col, shifted[:, w_buf - LANES:])


def _win_update(layer, win, new):
    _, b, g, r, w_buf = win.shape
    return pl.pallas_call(
        _win_update_body,
        grid=(b,),
        in_specs=[pl.BlockSpec((None, 1, g, r, w_buf), lambda i: (layer, i, 0, 0, 0)),
                  pl.BlockSpec((1, g, 1, r), lambda i: (i, 0, 0, 0))],
        out_specs=pl.BlockSpec((1, g, r, w_buf), lambda i: (i, 0, 0, 0)),
        out_shape=jax.ShapeDtypeStruct((b, g, r, w_buf), F32),
        compiler_params=_cparams(("arbitrary",)),
        name="win_update",
    )(win, new)


def _finish_body(x_ref, ho_ref, no_ref, wo_ref, g1_ref, g2_ref, g3_ref, w1_ref, w2_ref, o_ref):
    wo = wo_ref[...]
    y = _dot(ho_ref[...].astype(BF16), wo[0:HG_W]) + _dot(no_ref[...].astype(BF16), wo[HG_W:])
    h = x_ref[...] + _rms(y, g1_ref[...])
    a = _dot(_rms(h, g2_ref[...]).astype(BF16), w1_ref[...])
    dff = a.shape[1] // 2
    u, v = a[:, :dff], a[:, dff:]
    act = (u * _sigmoid(u)) * v
    y2 = _dot(act.astype(BF16), w2_ref[...])
    o_ref[...] = h + _rms(y2, g3_ref[...])


def _finish(x2d, ho, no, wo, g1, g2, g3, w1, w2, tm):
    m, d = x2d.shape
    row = lambda w: pl.BlockSpec((tm, w), lambda i: (i, 0))
    full = lambda a: pl.BlockSpec(a.shape, lambda i: (0, 0), pipeline_mode=pl.Buffered(1))
    return pl.pallas_call(
        _finish_body,
        grid=(m // tm,),
        in_specs=[row(d), row(HG_W), row(NSA_W), full(wo), full(g1), full(g2), full(g3), full(w1), full(w2)],
        out_specs=row(d),
        out_shape=jax.ShapeDtypeStruct((m, d), F32),
        compiler_params=_cparams(("arbitrary",)),
        name="finish",
    )(x2d, ho, no, wo, g1, g2, g3, w1, w2)


def _tile(n, pref):
    while n % pref:
        pref //= 2
    return pref


def kernel(x_prompt, x_sample, state_hgrn, cache_cmp_kv, cache_sel_kv, cache_win_kv, page_table,
           norm_mix_pre, norm_mix_post, norm_ffn_pre, norm_ffn_post, w_in, w_out, hg_lb_logits,
           hg_out_norm, cmp_pe, w_cmp, w_ffn_in, w_ffn_out):
    bp, tp, d = x_prompt.shape
    bs, ts, _ = x_sample.shape
    assert ts == 1
    depth = w_in.shape[0]
    n_pages = page_table.shape[1]
    past = n_pages * PAGE_SIZE
    assert tp % PAGE_SIZE == 0 and tp >= WINDOW + PAGE_SIZE

    def token_minor(cache):
        return cache.transpose(0, 1, 3, 4, 5, 2).reshape(cache.shape[0], cache.shape[1], NSA_KV_HEADS,
                                                        2 * HEAD_DIM, cache.shape[2])

    def token_major(a):
        return a.reshape(a.shape[0], NSA_KV_HEADS, 2, HEAD_DIM, a.shape[3]).transpose(0, 4, 1, 2, 3)

    tm_p = _tile(tp, 512)
    tabs_p = _rope_tables(jnp.arange(tp))
    tabs_s = _rope_tables(jnp.full((bs * ts,), past, jnp.int32))
    cmp_pool, sel_pool, win_pool = token_minor(cache_cmp_kv), token_minor(cache_sel_kv), token_minor(cache_win_kv)
    prompt_pages = jnp.zeros((bp, tp // PAGE_SIZE), jnp.int32)
    n_sel_s = -(-(past + ts) // SEL_BLOCK)
    n_sel_pad_s = -(-n_sel_s // LANES) * LANES
    n_top_s = min(N_SELECT, n_sel_s)

    xp = x_prompt.reshape(bp * tp, d)
    xs = x_sample.reshape(bs * ts, d)
    hgp, hgs, cmpp, cmps, selp, sels, winp, wins = [], [], [], [], [], [], [], []
    for l in range(depth):
        w_in_bf = _pad_w_in(w_in[l])
        w_out_bf = w_out[l].astype(BF16)
        w1_bf = w_ffn_in[l].astype(BF16)
        w2_bf = w_ffn_out[l].astype(BF16)
        g_pre, g_post = norm_mix_pre[l][None, :], norm_mix_post[l][None, :]
        g_fpre, g_fpost = norm_ffn_pre[l][None, :], norm_ffn_post[l][None, :]
        gain = hg_out_norm[l][None, :]
        w_c, pe_c = _compress_weights(w_cmp[l], cmp_pe[l])

        hq, fl, hi, hg, qn, qr, kvc_t, kvs_t, kvw_t, gl = _inproj(xp, g_pre, w_in_bf, tabs_p, tm_p, True)
        r3 = lambda a: a.reshape(bp, tp, a.shape[-1])
        ho, s_fin = _hgrn_prompt(l, hg_lb_logits, r3(hq), r3(fl), r3(hi), r3(hg), gain, _tile(tp, 256))
        y, const = _compress(kvc_t, lambda i, pg, pt: (i, 0, 0, pg), prompt_pages, w_c, pe_c,
                             _tile(tp // PAGE_SIZE, CMP_PAGES_PER_STEP))
        no = _nsa_prompt(r3(qn), r3(qr), r3(gl), y, const, kvs_t, kvw_t, PAGE_SIZE, _tile(tp, SEL_KV_TILE))
        xp = _finish(xp, ho.reshape(bp * tp, HG_W), no.reshape(bp * tp, NSA_W), w_out_bf, g_post, g_fpre,
                     g_fpost, w1_bf, w2_bf, tm_p)
        hgp.append(s_fin)
        cmpp.append(token_major(kvc_t))
        selp.append(token_major(kvs_t))
        winp.append(token_major(kvw_t[..., tp - min(WINDOW, tp):]))

        hq, fl, hi, hg, qn, qr, kvc, kvs, kvw, gl = _inproj(xs, g_pre, w_in_bf, tabs_s, bs * ts, False)
        v3 = lambda a: a.reshape(bs, 1, a.shape[-1])
        s_new, ho = _hgrn_sample(l, hg_lb_logits, state_hgrn[l], v3(hq), v3(fl), v3(hi), v3(hg), gain)
        y, const = _compress(cmp_pool, lambda i, pg, pt, l=l: (l, pt[i * n_pages + pg], 0, 0, 0),
                             page_table, w_c, pe_c, _tile(n_pages, CMP_PAGES_PER_STEP))

        def q8(q):
            q4 = q.reshape(bs, NSA_KV_HEADS, GQA_REP, HEAD_DIM)
            return jnp.pad(q4, ((0, 0), (0, 0), (0, 8 - GQA_REP), (0, LANES - HEAD_DIM)))

        idx, oc = _nsa_sample_cmp(past, q8(qn), y, const, n_sel_pad_s, _tile(bs, 4))
        g4 = lambda a: a.reshape(bs, NSA_KV_HEADS, 1, LANES)
        o8 = _nsa_sample_sel(past, l, idx[:, :, 0, :n_top_s].reshape(-1), page_table, sel_pool, q8(qr),
                             g4(kvs), g4(kvw), win_pool, oc, g4(gl))
        no = o8[:, :, :GQA_REP, :HEAD_DIM].reshape(bs * ts, NSA_W)
        win_new = _win_update(l, win_pool, g4(kvw))
        xs = _finish(xs, ho.reshape(bs * ts, HG_W), no, w_out_bf, g_post, g_fpre, g_fpost, w1_bf, w2_bf,
                     bs * ts)
        kv5 = lambda a, t: a.reshape(-1, t, NSA_KV_HEADS, 2, HEAD_DIM)
        hgs.append(s_new)
        cmps.append(kv5(kvc, ts))
        sels.append(kv5(kvs, ts))
        wins.append(token_major(win_new))

    return (xp.reshape(bp, tp, d), xs.reshape(bs, ts, d), jnp.stack(hgp), jnp.stack(hgs), jnp.stack(cmpp),
            jnp.stack(cmps), jnp.stack(selp), jnp.stack(sels), jnp.stack(winp), jnp.stack(wins))
```

```python
import functools

import jax
import jax.numpy as jnp
import numpy as np
from jax import lax
from jax.experimental import pallas as pl
from jax.experimental.pallas import tpu as pltpu

F32 = jnp.float32
BF16 = jnp.bfloat16

HG_HEADS = 4
HG_DK = 128
HG_DV = 128
HG_W = HG_HEADS * HG_DV
HEAD_DIM = 64
NSA_HEADS = 8
NSA_KV_HEADS = 2
GQA_REP = NSA_HEADS // NSA_KV_HEADS
NSA_W = NSA_HEADS * HEAD_DIM
KV_W = NSA_KV_HEADS * 2 * HEAD_DIM
N_GATES = NSA_HEADS * 3
CMP_STRIDE = 16
CMP_RATIO = 2
CMP_BLOCK = CMP_STRIDE * CMP_RATIO
SEL_BLOCK = 64
N_SELECT = 16
WINDOW = 512
PAGE_SIZE = 128
ROPE_THETA = 500000.0
ROT_DIM = HEAD_DIM // 4
EPS = 1e-6
FORCED_SCORE = 1e6
NEG_BIG = -1e30
LB_FLOOR = 1e-30
REMOVED = -3e38
LOG2_E = 1.4426950408889634

LANES = 128
HG_CHUNK = 16
GATE_PAD = 2 * LANES
CMP_CHAIN_PAGES = 16
CMP_PAGES_PER_STEP = 128
SEL_KV_TILE = 1024
COL_QN = 4 * HG_W
COL_KVC = COL_QN + NSA_W
COL_KVS = COL_KVC + KV_W
COL_KVW = COL_KVS + KV_W
COL_GL = COL_KVW + KV_W
IN_W_PAD = COL_GL + GATE_PAD

VMEM_LIMIT = 56 * 1024 * 1024


def _cparams(sem):
    return pltpu.CompilerParams(dimension_semantics=sem, vmem_limit_bytes=VMEM_LIMIT)


def _rms(x, g):
    return x * lax.rsqrt(jnp.mean(x * x, axis=-1, keepdims=True) + EPS) * g


def _sigmoid(x):
    return 1.0 / (1.0 + jnp.exp(-x))


def _dot(a, b):
    return jnp.dot(a, b, preferred_element_type=F32)


def _dot_nt(a, b):
    return lax.dot_general(a, b, (((1,), (1,)), ((), ())), preferred_element_type=F32)


def _dot_tn(a, b):
    return lax.dot_general(a, b, (((0,), (0,)), ((), ())), preferred_element_type=F32)


def _div_pow2(x, n):
    assert n & (n - 1) == 0
    return x >> (n.bit_length() - 1)


def _split3(x):
    hi = x.astype(BF16)
    r1 = x - hi.astype(F32)
    mid = r1.astype(BF16)
    lo = (r1 - mid.astype(F32)).astype(BF16)
    return hi, mid, lo


def _rope128(x, c, s1, s2):
    return x * c + pltpu.roll(x, LANES - ROT_DIM // 2, 1) * s1 + pltpu.roll(x, ROT_DIM // 2, 1) * s2


def _inproj_body(kv_t, x_ref, g_ref, w_ref, cq_ref, s1q_ref, s2q_ref,
                 hq_ref, fl_ref, hi_ref, hg_ref, qn_ref, qr_ref, kvc_ref, kvs_ref, kvw_ref, gl_ref):
    def put_kv(ref, j, blk):
        if kv_t:
            ref[0, j] = blk.T
        else:
            ref[:, j * LANES:(j + 1) * LANES] = blk

    hn = _rms(x_ref[...], g_ref[...])
    z = _dot(hn.astype(BF16), w_ref[...])
    hq_ref[...] = z[:, 0:HG_W]
    fl_ref[...] = z[:, HG_W:2 * HG_W]
    hi_ref[...] = z[:, 2 * HG_W:3 * HG_W]
    hg_ref[...] = z[:, 3 * HG_W:4 * HG_W]
    scale = HEAD_DIM ** -0.5 * LOG2_E
    cq, s1q, s2q = cq_ref[...], s1q_ref[...], s2q_ref[...]
    for j in range(NSA_W // LANES):
        blk = z[:, COL_QN + j * LANES:COL_QN + (j + 1) * LANES]
        qn_ref[:, j * LANES:(j + 1) * LANES] = (blk * scale).astype(BF16)
        qr_ref[:, j * LANES:(j + 1) * LANES] = (_rope128(blk, cq, s1q, s2q) * scale).astype(BF16)
    k_half = lax.broadcasted_iota(jnp.int32, cq.shape, 1) < HEAD_DIM
    ck, s1k, s2k = jnp.where(k_half, cq, 1.0), jnp.where(k_half, s1q, 0.0), jnp.where(k_half, s2q, 0.0)
    for j in range(NSA_KV_HEADS):
        cols = lambda c0: z[:, c0 + j * LANES:c0 + (j + 1) * LANES]
        put_kv(kvc_ref, j, cols(COL_KVC))
        put_kv(kvs_ref, j, _rope128(cols(COL_KVS), ck, s1k, s2k))
        put_kv(kvw_ref, j, _rope128(cols(COL_KVW), ck, s1k, s2k))
    gl_ref[...] = z[:, COL_GL:COL_GL + GATE_PAD]


def _inproj(x2d, gain, w_bf, tabs, tm, kv_t):
    m, d = x2d.shape
    n_pos_blocks = tabs[0].shape[0] // tm
    row = lambda i: (i, 0)
    tab = lambda i: (i % n_pos_blocks, 0)
    const = lambda i: (0, 0)
    widths = (HG_W, HG_W, HG_W, HG_W, NSA_W, NSA_W, GATE_PAD)
    dtypes = (F32, F32, F32, F32, BF16, BF16, F32)
    specs = [pl.BlockSpec((tm, w), row) for w in widths]
    shapes = [jax.ShapeDtypeStruct((m, w), dt) for w, dt in zip(widths, dtypes)]
    if kv_t:
        t = tabs[0].shape[0]
        kv_spec = pl.BlockSpec((1, NSA_KV_HEADS, LANES, tm), lambda i: (i // n_pos_blocks, 0, 0, i % n_pos_blocks))
        kv_shape = jax.ShapeDtypeStruct((m // t, NSA_KV_HEADS, LANES, t), F32)
    else:
        kv_spec = pl.BlockSpec((tm, KV_W), row)
        kv_shape = jax.ShapeDtypeStruct((m, KV_W), F32)
    return pl.pallas_call(
        functools.partial(_inproj_body, kv_t),
        grid=(m // tm,),
        in_specs=[pl.BlockSpec((tm, d), row), pl.BlockSpec((1, d), const),
                  pl.BlockSpec((d, IN_W_PAD), const, pipeline_mode=pl.Buffered(1))]
        + [pl.BlockSpec((tm, LANES), tab)] * 3,
        out_specs=specs[:6] + [kv_spec] * 3 + specs[6:],
        out_shape=shapes[:6] + [kv_shape] * 3 + shapes[6:],
        compiler_params=_cparams(("arbitrary",)),
        name="inproj",
    )(x2d, gain, w_bf, *tabs)


def _rope_tables(pos):
    half = ROT_DIM // 2
    inv = ROPE_THETA ** (-2.0 * jnp.arange(half, dtype=F32) / ROT_DIM)
    ang = pos.astype(F32)[:, None] * inv[None, :]
    cos, sin = jnp.cos(ang), jnp.sin(ang)
    p = pos.shape[0]
    one = jnp.ones((p, HEAD_DIM - ROT_DIM), F32)
    zero = jnp.zeros((p, HEAD_DIM - ROT_DIM), F32)
    zh = jnp.zeros((p, half), F32)
    c64 = jnp.concatenate([cos, cos, one], axis=1)
    s1_64 = jnp.concatenate([-sin, zh, zero], axis=1)
    s2_64 = jnp.concatenate([zh, sin, zero], axis=1)
    return tuple(jnp.concatenate([t, t], axis=1) for t in (c64, s1_64, s2_64))


def _pad_w_in(w):
    d = w.shape[0]
    per_g = GQA_REP * 3
    pad = jnp.zeros((d, LANES - per_g), w.dtype)
    gates = [jnp.concatenate([w[:, COL_GL + g * per_g:COL_GL + (g + 1) * per_g], pad], axis=1)
             for g in range(NSA_KV_HEADS)]
    return jnp.concatenate([w[:, :COL_GL]] + gates, axis=1).astype(BF16)


def _lower_bound(lbl_ref, layer):
    rows = [lbl_ref[l:l + 1, :] for l in range(lbl_ref.shape[0])]
    mx = functools.reduce(jnp.maximum, rows)
    ex = [jnp.exp(r - mx) for r in rows]
    den = functools.reduce(lambda a, b: a + b, ex)
    sm = [e / den for e in ex]
    cum = sm[0]
    for l in range(1, layer + 1):
        cum = cum + sm[l]
    return cum - sm[0]


def _forget_gate(fl, lb):
    logsig = jnp.minimum(fl, 0.0) - jnp.log1p(jnp.exp(-jnp.abs(fl)))
    a = jnp.log(jnp.maximum(lb, LB_FLOOR))
    b = jnp.log1p(-lb) + logsig
    logf = jnp.maximum(a, b) + jnp.log1p(jnp.exp(-jnp.abs(a - b)))
    kh = (1.0 - lb) * _sigmoid(-fl)
    return logf, kh


def _readout(o, g, gain):
    on = o * lax.rsqrt(jnp.mean(o * o, axis=-1, keepdims=True) + EPS) * gain
    return on * (g * _sigmoid(g))


def _hgrn_prompt_body(layer, lbl_ref, q_ref, f_ref, i_ref, g_ref, gain_ref, o_ref, s_ref,
                      st_sc, k_sc, b_sc, o_sc):
    t = pl.program_id(1)
    ct = q_ref.shape[1]

    @pl.when(t == 0)
    def _():
        st_sc[...] = jnp.zeros_like(st_sc)

    lb = _lower_bound(lbl_ref, layer)
    logf, kh = _forget_gate(f_ref[0], lb)
    k_sc[...] = kh
    ri = lax.broadcasted_iota(jnp.int32, (ct, ct), 0)
    ci = lax.broadcasted_iota(jnp.int32, (ct, ct), 1)
    tri = jnp.where((ci <= ri) & (_div_pow2(ci, HG_CHUNK) == _div_pow2(ri, HG_CHUNK)), 1.0, 0.0).astype(BF16)
    hi, mid, lo = _split3(logf)
    b_sc[...] = _dot(tri, hi) + _dot(tri, mid) + _dot(tri, lo)

    row = lax.broadcasted_iota(jnp.int32, (HG_CHUNK, HG_DK), 0)

    def chunk(c, carry):
        r0 = pl.multiple_of(c * HG_CHUNK, HG_CHUNK)
        for h in range(HG_HEADS):
            hl = slice(h * HG_DK, (h + 1) * HG_DK)
            q = q_ref[0, pl.ds(r0, HG_CHUNK), hl]
            k = k_sc[pl.ds(r0, HG_CHUNK), hl]
            v = i_ref[0, pl.ds(r0, HG_CHUNK), hl]
            b = b_sc[pl.ds(r0, HG_CHUNK), hl]
            st = st_sc[h]
            o = _dot_nt((q * jnp.exp(b)).astype(BF16), st.astype(BF16))
            for s in range(HG_CHUNK):
                w = jnp.where(row >= s, jnp.exp(b - b[s:s + 1, :]), 0.0)
                d = jnp.sum(q * (k[s:s + 1, :] * w), axis=-1, keepdims=True)
                o = o + d * v[s:s + 1, :]
            o_sc[pl.ds(r0, HG_CHUNK), hl] = o
            blast = b[HG_CHUNK - 1:HG_CHUNK, :]
            kdec = k * jnp.exp(blast - b)
            st_sc[h] = st * jnp.exp(blast) + _dot_tn(v.astype(BF16), kdec.astype(BF16))
        return carry

    lax.fori_loop(0, ct // HG_CHUNK, chunk, 0)

    gain = gain_ref[...]
    for h in range(HG_HEADS):
        hl = slice(h * HG_DV, (h + 1) * HG_DV)
        o_ref[0, :, hl] = _readout(o_sc[:, hl], g_ref[0, :, hl], gain)

    @pl.when(t == pl.num_programs(1) - 1)
    def _():
        for h in range(HG_HEADS):
            s_ref[0, h] = st_sc[h].T


def _hgrn_prompt(layer, lbl, hq, fl, hi, hg, gain, ct):
    b, t, _ = hq.shape
    tile = pl.BlockSpec((1, ct, HG_W), lambda i, j: (i, j, 0))
    return pl.pallas_call(
        functools.partial(_hgrn_prompt_body, layer),
        grid=(b, t // ct),
        in_specs=[pl.BlockSpec(lbl.shape, lambda i, j: (0, 0)), tile, tile, tile, tile,
                  pl.BlockSpec((1, HG_DV), lambda i, j: (0, 0))],
        out_specs=[tile, pl.BlockSpec((1, HG_HEADS, HG_DK, HG_DV), lambda i, j: (i, 0, 0, 0))],
        out_shape=[jax.ShapeDtypeStruct((b, t, HG_W), F32),
                   jax.ShapeDtypeStruct((b, HG_HEADS, HG_DK, HG_DV), F32)],
        scratch_shapes=[pltpu.VMEM((HG_HEADS, HG_DV, HG_DK), F32), pltpu.VMEM((ct, HG_W), F32),
                        pltpu.VMEM((ct, HG_W), F32), pltpu.VMEM((ct, HG_W), F32)],
        compiler_params=_cparams(("arbitrary", "arbitrary")),
        name="hgrn_prompt",
    )(lbl, hq, fl, hi, hg, gain)


def _hgrn_sample_body(layer, lbl_ref, s0_ref, q_ref, f_ref, i_ref, g_ref, gain_ref, s_ref, o_ref):
    lb = _lower_bound(lbl_ref, layer)
    logf, kh = _forget_gate(f_ref[0], lb)
    q, v, g = q_ref[0], i_ref[0], g_ref[0]
    gain = gain_ref[...]

    def col(x):
        return jnp.broadcast_to(x, (HG_DK, HG_DK)).T

    for h in range(HG_HEADS):
        hl = slice(h * HG_DK, (h + 1) * HG_DK)
        s_new = col(jnp.exp(logf[:, hl])) * s0_ref[0, h] + col(kh[:, hl]) * v[:, hl]
        s_ref[0, h] = s_new
        o = jnp.sum(col(q[:, hl]) * s_new, axis=0, keepdims=True)
        o_ref[0, :, hl] = _readout(o, g[:, hl], gain)


def _hgrn_sample(layer, lbl, s0, hq, fl, hi, hg, gain):
    b = s0.shape[0]
    vec = pl.BlockSpec((1, 1, HG_W), lambda i: (i, 0, 0))
    st = pl.BlockSpec((1, HG_HEADS, HG_DK, HG_DV), lambda i: (i, 0, 0, 0))
    return pl.pallas_call(
        functools.partial(_hgrn_sample_body, layer),
        grid=(b,),
        in_specs=[pl.BlockSpec(lbl.shape, lambda i: (0, 0)), st, vec, vec, vec, vec,
                  pl.BlockSpec((1, HG_DV), lambda i: (0, 0))],
        out_specs=[st, vec],
        out_shape=[jax.ShapeDtypeStruct(s0.shape, F32), jax.ShapeDtypeStruct((b, 1, HG_W), F32)],
        compiler_params=_cparams(("arbitrary",)),
        name="hgrn_sample",
    )(lbl, s0, hq, fl, hi, hg, gain)


def _compress_body(n_pages, *refs):
    page_refs = refs[1:1 + n_pages]
    w_ref, pe_ref, y_ref, const_ref = refs[1 + n_pages:]
    rows = PAGE_SIZE // CMP_STRIDE
    tok = lax.broadcasted_iota(jnp.int32, (PAGE_SIZE, PAGE_SIZE), 0)
    col = lax.broadcasted_iota(jnp.int32, (PAGE_SIZE, PAGE_SIZE), 1)
    perm = jnp.where(tok == CMP_STRIDE * (col & (rows - 1)) + _div_pow2(col, rows), 1.0, 0.0).astype(BF16)
    n_split = n_pages // CMP_CHAIN_PAGES if n_pages % CMP_CHAIN_PAGES == 0 else 1
    per = n_pages // n_split
    for g in range(NSA_KV_HEADS):
        for h in range(n_split):
            page_ids = range(h * per, (h + 1) * per)
            pages = jnp.concatenate([page_refs[p][g].astype(BF16) for p in page_ids], axis=0)
            permuted = _dot(pages, perm)
            pieces = [permuted[i * LANES:(i + 1) * LANES, :].T for i in range(per)]
            lhs = jnp.concatenate(
                [jnp.concatenate([pc[c * rows:(c + 1) * rows, :] for pc in pieces], axis=0)
                 for c in range(CMP_STRIDE)], axis=1)
            y_ref[0, h * per * rows:(h + 1) * per * rows, g * CMP_RATIO * LANES:(g + 1) * CMP_RATIO * LANES] = (
                _dot(lhs.astype(BF16), w_ref[...]))

    @pl.when((pl.program_id(0) == 0) & (pl.program_id(1) == 0))
    def _():
        tot = jnp.zeros((8, LANES), F32)
        for c in range(CMP_STRIDE):
            for m in range(CMP_RATIO):
                pe_row = jnp.broadcast_to(pe_ref[m, c:c + 1, :], (8, LANES))
                tot = tot + _dot(pe_row.astype(BF16), w_ref[c * LANES:(c + 1) * LANES, m * LANES:(m + 1) * LANES])
        const_ref[...] = tot


def _compress(pages, page_index, page_table, w_c, pe_c, n_pages):
    s, p = page_table.shape
    rows = PAGE_SIZE // CMP_STRIDE
    steps = p // n_pages

    def page_spec(k):
        return pl.BlockSpec((None,) * (pages.ndim - 3) + (NSA_KV_HEADS, LANES, PAGE_SIZE),
                            lambda i, j, pt: page_index(i, j * n_pages + k, pt))

    page_specs = [page_spec(k) for k in range(n_pages)]
    grid_spec = pltpu.PrefetchScalarGridSpec(
        num_scalar_prefetch=1,
        grid=(s, steps),
        in_specs=page_specs + [
            pl.BlockSpec(w_c.shape, lambda i, j, pt: (0, 0)),
            pl.BlockSpec(pe_c.shape, lambda i, j, pt: (0, 0, 0))],
        out_specs=[pl.BlockSpec((1, n_pages * rows, NSA_KV_HEADS * CMP_RATIO * LANES),
                                lambda i, j, pt: (i, j, 0)),
                   pl.BlockSpec((8, LANES), lambda i, j, pt: (0, 0))],
    )
    return pl.pallas_call(
        functools.partial(_compress_body, n_pages),
        grid_spec=grid_spec,
        out_shape=[jax.ShapeDtypeStruct((s, p * rows, NSA_KV_HEADS * CMP_RATIO * LANES), F32),
                   jax.ShapeDtypeStruct((8, LANES), F32)],
        compiler_params=_cparams(("arbitrary", "arbitrary")),
        name="compress",
    )(page_table.reshape(-1), *([pages] * n_pages), w_c, pe_c)


def _compress_weights(w_cmp, pe):
    w_r = w_cmp.reshape(2, CMP_RATIO, CMP_STRIDE, HEAD_DIM, HEAD_DIM)
    z = jnp.zeros((CMP_STRIDE, HEAD_DIM, HEAD_DIM), w_cmp.dtype)
    blocks = []
    for m in range(CMP_RATIO):
        top = jnp.concatenate([w_r[0, m], z], axis=2)
        bot = jnp.concatenate([z, w_r[1, m]], axis=2)
        blocks.append(jnp.concatenate([top, bot], axis=1))
    w_c = jnp.concatenate(blocks, axis=2).astype(BF16).reshape(CMP_STRIDE * LANES, CMP_RATIO * LANES)
    pe_r = pe.reshape(2, CMP_RATIO, CMP_STRIDE, HEAD_DIM)
    pe_c = jnp.concatenate([pe_r[0], pe_r[1]], axis=-1)
    return w_c, pe_c


def _cmp_from_y(y0, y1, const):
    body = y0[:-1, :] + y1[1:, :] + const
    return jnp.concatenate([body, jnp.zeros((1, LANES), F32)], axis=0)


def _kv_operands(kv):
    lane = lax.broadcasted_iota(jnp.int32, kv.shape, 1)
    v1 = jnp.where(lane < HEAD_DIM, pltpu.roll(kv, HEAD_DIM, 1), 1.0)
    return kv.astype(BF16), v1.astype(BF16)


def _stack_heads(qblk):
    q = qblk.astype(F32)
    lane = lax.broadcasted_iota(jnp.int32, (q.shape[0], LANES), 1)
    outs = []
    for r in range(GQA_REP):
        grp = q[:, (r // 2) * LANES:(r // 2 + 1) * LANES]
        if r % 2:
            grp = pltpu.roll(grp, HEAD_DIM, 1)
        outs.append(jnp.where(lane < HEAD_DIM, grp, 0.0))
    return jnp.concatenate(outs, axis=0).astype(BF16)


def _overlap_matrix(n_cmp_pad, n_sel_pad):
    n = lax.broadcasted_iota(jnp.int32, (n_cmp_pad, n_sel_pad), 0) * CMP_STRIDE
    j = lax.broadcasted_iota(jnp.int32, (n_cmp_pad, n_sel_pad), 1) * SEL_BLOCK
    return jnp.where((n < j + SEL_BLOCK) & (n + CMP_BLOCK > j), 1.0, 0.0).astype(BF16)


def _overlap_matrix_t(n_sel_pad, n_cmp_pad):
    j = lax.broadcasted_iota(jnp.int32, (n_sel_pad, n_cmp_pad), 0) * SEL_BLOCK
    n = lax.broadcasted_iota(jnp.int32, (n_sel_pad, n_cmp_pad), 1) * CMP_STRIDE
    return jnp.where((n < j + SEL_BLOCK) & (n + CMP_BLOCK > j), 1.0, 0.0).astype(BF16)


def _kv_operands_t(kvt):
    v1 = jnp.concatenate([kvt[HEAD_DIM:], jnp.ones((HEAD_DIM, kvt.shape[1]), F32)], axis=0)
    return kvt.astype(BF16), v1.astype(BF16)


def _topk_axis0(score, n_top):
    jf = lax.broadcasted_iota(jnp.int32, score.shape, 0).astype(F32)
    big = float(score.shape[0])
    sel = jnp.zeros(score.shape, F32)
    for _ in range(n_top):
        m = jnp.max(score, axis=0, keepdims=True)
        first = jnp.min(jnp.where(score == m, jf, big), axis=0, keepdims=True)
        pick = jf == first
        sel = jnp.where(pick & (m > 0.5 * NEG_BIG), 1.0, sel)
        score = jnp.where(pick, REMOVED, score)
    return sel


def _nsa_prompt_body(kt, qn_ref, qr_ref, gl_ref, y_ref, const_ref, kvs_ref, kvw_ref, o_ref,
                     kc_sc, vc_sc, ks_sc, vs_sc, kw_sc, vw_sc):
    qt = pl.program_id(2)
    tq = qn_ref.shape[1]
    t_len = kvs_ref.shape[3]
    n_cmp_pad = y_ref.shape[1]
    n_sel_pad = t_len // SEL_BLOCK
    rows = GQA_REP * tq

    @pl.when(qt == 0)
    def _():
        y = y_ref[0]
        kv_cmp = _cmp_from_y(y[:, 0:LANES], y[:, LANES:2 * LANES], const_ref[0:1, :])
        kc_sc[...], vc_sc[...] = _kv_operands(kv_cmp)
        kop, vs_sc[...] = _kv_operands_t(kvs_ref[0, 0])
        ks_sc[0:LANES, :] = kop
        j = lax.broadcasted_iota(jnp.int32, (n_sel_pad, t_len), 0)
        tok = lax.broadcasted_iota(jnp.int32, (n_sel_pad, t_len), 1)
        ks_sc[LANES:, :] = jnp.where(j == _div_pow2(tok & (kt - 1), SEL_BLOCK), 1.0, 0.0).astype(BF16)
        kw_sc[...], vw_sc[...] = _kv_operands_t(kvw_ref[0, 0])

    def cmp_and_select(qn_blk, tile):
        t0 = tile * tq
        pos_rows = t0 + lax.broadcasted_iota(jnp.int32, (tq, 1), 0)
        pos_rows4 = jnp.concatenate([pos_rows] * GQA_REP, axis=0)
        s = _dot_nt(_stack_heads(qn_blk), kc_sc[...])
        cmp_end = lax.broadcasted_iota(jnp.int32, (1, n_cmp_pad), 1) * CMP_STRIDE + (CMP_BLOCK - 1)
        vis = cmp_end <= pos_rows4
        s = jnp.where(vis, s, NEG_BIG)
        m = jnp.max(s, axis=-1, keepdims=True)
        p = jnp.where(vis, jnp.exp2(s - m), 0.0)
        p = p / jnp.maximum(jnp.sum(p, axis=-1, keepdims=True), 1e-30)
        acc = _dot(p.astype(BF16), vc_sc[...])
        psum = p[0:tq]
        for r in range(1, GQA_REP):
            psum = psum + p[r * tq:(r + 1) * tq]
        ov_t = _overlap_matrix_t(n_sel_pad, n_cmp_pad)
        hi, mid, lo = _split3(psum)
        imp_t = _dot_nt(ov_t, hi) + _dot_nt(ov_t, mid) + _dot_nt(ov_t, lo)
        blk = lax.broadcasted_iota(jnp.int32, (n_sel_pad, tq), 0)
        pos_l = t0 + lax.broadcasted_iota(jnp.int32, (1, tq), 1)
        cur = _div_pow2(pos_l, SEL_BLOCK)
        valid = blk * SEL_BLOCK <= pos_l
        forced = (blk == 0) | (blk == cur) | (blk == cur - 1)
        n_forced = 3
        score = jnp.where(valid & jnp.logical_not(forced), imp_t, jnp.where(valid, REMOVED, NEG_BIG))
        sel_t = _topk_axis0(score, min(N_SELECT, n_sel_pad) - n_forced)
        sel_t = jnp.where(valid & forced, 1.0, sel_t)
        return acc, jnp.where(sel_t.T > 0.5, 0.0, NEG_BIG)

    q0 = qt * tq
    pos = q0 + lax.broadcasted_iota(jnp.int32, (tq, 1), 0)
    pos4 = jnp.concatenate([pos] * GQA_REP, axis=0)
    qr = _stack_heads(qr_ref[0])

    wlen = WINDOW + tq
    w0 = pl.multiple_of(jnp.maximum(q0 - WINDOW, 0), tq)
    s_w = _dot(qr, kw_sc[:, pl.ds(w0, wlen)])
    dist = pos4 - (w0 + lax.broadcasted_iota(jnp.int32, (1, wlen), 1))
    ok_w = (dist >= 0) & (dist < WINDOW)
    s_w = jnp.where(ok_w, s_w, NEG_BIG)
    m_w = jnp.max(s_w, axis=-1, keepdims=True)
    p_w = jnp.where(ok_w, jnp.exp2(s_w - m_w), 0.0)
    acc_w = _dot_nt(p_w.astype(BF16), vw_sc[:, pl.ds(w0, wlen)])

    acc_c, sel_bias = cmp_and_select(qn_ref[0], qt)

    def sel_tile(k0, m_i, acc, causal):
        shift = (n_sel_pad - k0 // SEL_BLOCK) & (n_sel_pad - 1)
        bias = pltpu.roll(sel_bias, shift, 1).astype(BF16)
        lhs = jnp.concatenate([qr, jnp.concatenate([bias] * GQA_REP, axis=0)], axis=1)
        s_i = _dot(lhs, ks_sc[:, pl.ds(k0, kt)])
        if causal:
            tok = k0 + lax.broadcasted_iota(jnp.int32, (1, kt), 1)
            s_i = jnp.where(tok <= pos4, s_i, NEG_BIG)
        m_new = jnp.maximum(m_i, jnp.max(s_i, axis=-1, keepdims=True))
        p_i = jnp.exp2(s_i - m_new)
        acc = acc * jnp.exp2(m_i - m_new) + _dot_nt(p_i.astype(BF16), vs_sc[:, pl.ds(k0, kt)])
        return m_new, acc

    assert n_sel_pad & (n_sel_pad - 1) == 0 and kt // SEL_BLOCK <= n_sel_pad

    def past_tiles(i, c, n):
        for u in range(n):
            c = sel_tile(pl.multiple_of((i * n + u) * kt, kt), c[0], c[1], False)
        return c

    n_full = q0 // kt
    carry = (jnp.full((rows, 1), NEG_BIG, F32), jnp.zeros((rows, LANES), F32))
    carry = lax.fori_loop(0, n_full // 4, lambda i, c: past_tiles(i, c, 4), carry)
    carry = lax.fori_loop((n_full // 4) * 2, n_full // 2, lambda i, c: past_tiles(i, c, 2), carry)
    carry = lax.fori_loop((n_full // 2) * 2, n_full, lambda i, c: past_tiles(i, c, 1), carry)
    _, acc_s = sel_tile(pl.multiple_of(n_full * kt, kt), carry[0], carry[1], True)

    def norm(acc):
        return acc / jnp.maximum(acc[:, HEAD_DIM:HEAD_DIM + 1], 1e-30)

    o_c, o_s, o_w = acc_c, norm(acc_s), norm(acc_w)
    gate = _sigmoid(gl_ref[0])
    lane = lax.broadcasted_iota(jnp.int32, (tq, LANES), 1)
    heads = []
    for r in range(GQA_REP):
        rs = slice(r * tq, (r + 1) * tq)
        heads.append(gate[:, 3 * r:3 * r + 1] * o_c[rs] + gate[:, 3 * r + 1:3 * r + 2] * o_s[rs]
                     + gate[:, 3 * r + 2:3 * r + 3] * o_w[rs])
    for j in range(GQA_REP // 2):
        o_ref[0, :, j * LANES:(j + 1) * LANES] = jnp.where(
            lane < HEAD_DIM, heads[2 * j], pltpu.roll(heads[2 * j + 1], HEAD_DIM, 1))


def _nsa_prompt(qn, qr, gl, y, const, kvs_t, kvw_t, tq, kt):
    b, _, _, t = kvs_t.shape
    n_cmp_pad = y.shape[1]
    n_sel_pad = t // SEL_BLOCK
    gw = GQA_REP * HEAD_DIM
    qspec = pl.BlockSpec((1, tq, gw), lambda i, g, j: (i, j, g))
    res = pl.BlockSpec((1, 1, LANES, t), lambda i, g, j: (i, g, 0, 0))
    return pl.pallas_call(
        functools.partial(_nsa_prompt_body, kt),
        grid=(b, NSA_KV_HEADS, t // tq),
        in_specs=[qspec, qspec, pl.BlockSpec((1, tq, LANES), lambda i, g, j: (i, j, g)),
                  pl.BlockSpec((1, n_cmp_pad, CMP_RATIO * LANES), lambda i, g, j: (i, 0, g)),
                  pl.BlockSpec((8, LANES), lambda i, g, j: (0, 0)), res, res],
        out_specs=qspec,
        out_shape=jax.ShapeDtypeStruct((b, t, NSA_W), F32),
        scratch_shapes=[pltpu.VMEM((n_cmp_pad, LANES), BF16), pltpu.VMEM((n_cmp_pad, LANES), BF16),
                        pltpu.VMEM((LANES + n_sel_pad, t), BF16)] + [pltpu.VMEM((LANES, t), BF16)] * 3,
        compiler_params=_cparams(("arbitrary", "arbitrary", "arbitrary")),
        name="nsa_prompt",
    )(qn, qr, gl, y, const, kvs_t, kvw_t)


def _nsa_sample_cmp_body(past, q_ref, y_ref, const_ref, idx_ref, oc_ref):
    sb = q_ref.shape[0]
    n_cmp_pad = y_ref.shape[1]
    n_sel = -(-(past + 1) // SEL_BLOCK)
    n_sel_pad = idx_ref.shape[3]
    n_rows = sb * NSA_KV_HEADS
    ov = _overlap_matrix(n_cmp_pad, n_sel_pad)
    cmp_end = lax.broadcasted_iota(jnp.int32, (1, n_cmp_pad), 1) * CMP_STRIDE + (CMP_BLOCK - 1)
    vis = cmp_end <= past
    row = lax.broadcasted_iota(jnp.int32, (n_rows, n_cmp_pad), 0)
    psum_all = jnp.zeros((n_rows, n_cmp_pad), F32)
    for s_i in range(sb):
        y = y_ref[s_i]
        for g in range(NSA_KV_HEADS):
            c0 = g * CMP_RATIO * LANES
            kv_cmp = _cmp_from_y(y[:, c0:c0 + LANES], y[:, c0 + LANES:c0 + 2 * LANES], const_ref[0:1, :])
            kc, vc = _kv_operands(kv_cmp)
            s = _dot_nt(q_ref[s_i, g], kc)
            s = jnp.where(vis, s, NEG_BIG)
            m = jnp.max(s, axis=-1, keepdims=True)
            p = jnp.where(vis, jnp.exp2(s - m), 0.0)
            p = p / jnp.maximum(jnp.sum(p, axis=-1, keepdims=True), 1e-30)
            oc_ref[s_i, g] = _dot(p.astype(BF16), vc)
            psum = jnp.sum(p[0:GQA_REP], axis=0, keepdims=True)
            psum_all = jnp.where(row == s_i * NSA_KV_HEADS + g, psum, psum_all)
    hi, mid, lo = _split3(psum_all)
    imp = _dot(hi, ov) + _dot(mid, ov) + _dot(lo, ov)
    blk = lax.broadcasted_iota(jnp.int32, (n_rows, n_sel_pad), 1)
    cur = past // SEL_BLOCK
    valid = (blk * SEL_BLOCK <= past) & (blk < n_sel)
    forced = (blk == 0) | (blk == cur) | (blk == cur - 1)
    jf = blk.astype(F32)
    score = jnp.where(valid, jnp.where(forced, FORCED_SCORE, imp), NEG_BIG)
    idx = jnp.full((n_rows, n_sel_pad), -1.0, F32)
    for k in range(min(N_SELECT, n_sel)):
        mx = jnp.max(score, axis=-1, keepdims=True)
        first = jnp.min(jnp.where(score == mx, jf, float(n_sel_pad)), axis=-1, keepdims=True)
        idx = jnp.where((blk == k) & (mx > 0.5 * NEG_BIG), first, idx)
        score = jnp.where(jf == first, REMOVED, score)
    idx = idx.astype(jnp.int32)
    for s_i in range(sb):
        for g in range(NSA_KV_HEADS):
            r = s_i * NSA_KV_HEADS + g
            idx_ref[s_i, g] = jnp.broadcast_to(idx[r:r + 1, :], (8, n_sel_pad))


def _nsa_sample_cmp(past, q8, y, const, n_sel_pad, sb):
    b = q8.shape[0]
    n_cmp_pad = y.shape[1]
    blk4 = lambda w: pl.BlockSpec((sb, NSA_KV_HEADS, 8, w), lambda i: (i, 0, 0, 0))
    return pl.pallas_call(
        functools.partial(_nsa_sample_cmp_body, past),
        grid=(b // sb,),
        in_specs=[blk4(LANES), pl.BlockSpec((sb, n_cmp_pad, y.shape[2]), lambda i: (i, 0, 0)),
                  pl.BlockSpec((8, LANES), lambda i: (0, 0))],
        out_specs=[blk4(n_sel_pad), blk4(LANES)],
        out_shape=[jax.ShapeDtypeStruct((b, NSA_KV_HEADS, 8, n_sel_pad), jnp.int32),
                   jax.ShapeDtypeStruct((b, NSA_KV_HEADS, 8, LANES), F32)],
        compiler_params=_cparams(("arbitrary",)),
        name="nsa_sample_cmp",
    )(q8, y, const)


def _nsa_sample_sel_body(past, n_top, *refs):
    idx_ref, pt_ref = refs[0], refs[1]
    del pt_ref
    blk_refs = refs[2:2 + n_top]
    q_ref, new_s_ref, new_w_ref, win_ref, oc_ref, gl_ref, o_ref = refs[2 + n_top:]
    b, g = pl.program_id(0), pl.program_id(1)
    q = q_ref[0, 0]
    n_past_blocks = past // SEL_BLOCK

    def attend(pieces):
        ms = [jnp.max(jnp.where(ok, s, NEG_BIG), axis=-1, keepdims=True) for s, ok, _, _ in pieces]
        m = functools.reduce(jnp.maximum, ms)
        acc = jnp.zeros((8, LANES), F32)
        for s, ok, v, v_t in pieces:
            p = jnp.where(ok, jnp.exp2(s - m), 0.0).astype(BF16)
            acc = acc + (_dot_nt(p, v) if v_t else _dot(p, v))
        return acc

    def new_token(ref, enabled):
        kop, vop = _kv_operands(jnp.broadcast_to(ref[0, 0], (8, LANES)))
        first = lax.broadcasted_iota(jnp.int32, (8, 8), 1) == 0
        return _dot_nt(q, kop), first & enabled, vop, False

    per_page = PAGE_SIZE // SEL_BLOCK
    tok = lax.broadcasted_iota(jnp.int32, (8, n_top * PAGE_SIZE), 1)
    ok = tok < 0
    has_new = jnp.zeros((8, 8), jnp.int32)
    for k in range(n_top):
        j = idx_ref[(b * NSA_KV_HEADS + g) * n_top + k]
        lo = k * PAGE_SIZE + jnp.where((j >= 0) & (j < n_past_blocks), (j % per_page) * SEL_BLOCK, PAGE_SIZE)
        ok = ok | ((tok >= lo) & (tok < lo + SEL_BLOCK) & (tok < (k + 1) * PAGE_SIZE))
        has_new = has_new + jnp.where(j == n_past_blocks, 1, 0)
    kop, vop = _kv_operands_t(jnp.concatenate([blk_refs[k][...] for k in range(n_top)], axis=1))
    acc_s = attend([(_dot(q, kop), ok, vop, True), new_token(new_s_ref, has_new > 0)])

    win = win_ref[0]
    w_buf = win.shape[1]
    kw, vw = _kv_operands_t(win)
    dist = w_buf - lax.broadcasted_iota(jnp.int32, (8, w_buf), 1)
    ok_buf = (dist < WINDOW) & (past - dist >= 0)
    acc_w = attend([(_dot(q, kw), ok_buf, vw, True), new_token(new_w_ref, jnp.ones((8, 8), jnp.int32) > 0)])

    def norm(acc):
        return acc / jnp.maximum(acc[:, HEAD_DIM:HEAD_DIM + 1], 1e-30)

    gate = _sigmoid(gl_ref[0, 0])
    rsel = lax.broadcasted_iota(jnp.int32, (8, LANES), 0)
    gc = jnp.zeros((8, LANES), F32)
    gs = jnp.zeros((8, LANES), F32)
    gw = jnp.zeros((8, LANES), F32)
    for r in range(GQA_REP):
        gc = jnp.where(rsel == r, gate[:, 3 * r:3 * r + 1], gc)
        gs = jnp.where(rsel == r, gate[:, 3 * r + 1:3 * r + 2], gs)
        gw = jnp.where(rsel == r, gate[:, 3 * r + 2:3 * r + 3], gw)
    o_ref[0, 0] = gc * oc_ref[0, 0] + gs * norm(acc_s) + gw * norm(acc_w)


def _nsa_sample_sel(past, layer, idx, page_table, sel_pool, q8, new_s, new_w, win, oc, gl):
    b, n_past_pages = page_table.shape
    n_top = idx.shape[0] // (b * NSA_KV_HEADS)
    n_past_blocks = past // SEL_BLOCK
    per_page = PAGE_SIZE // SEL_BLOCK

    def blk_spec(k):
        def imap(i, g, idx_ref, pt_ref):
            j = jnp.clip(idx_ref[(i * NSA_KV_HEADS + g) * n_top + k], 0, n_past_blocks - 1)
            return (layer, pt_ref[i * n_past_pages + j // per_page], g, 0, 0)
        return pl.BlockSpec((None, None, None, LANES, PAGE_SIZE), imap)

    v4 = lambda w: pl.BlockSpec((1, 1, 8, w), lambda i, g, a, c: (i, g, 0, 0))
    v1 = pl.BlockSpec((1, 1, 1, LANES), lambda i, g, a, c: (i, g, 0, 0))
    grid_spec = pltpu.PrefetchScalarGridSpec(
        num_scalar_prefetch=2,
        grid=(b, NSA_KV_HEADS),
        in_specs=[blk_spec(k) for k in range(n_top)] + [
            v4(LANES), v1, v1,
            pl.BlockSpec((None, None, 1, LANES, win.shape[4]), lambda i, g, a, c: (layer, i, g, 0, 0)),
            v4(LANES), v1],
        out_specs=v4(LANES),
    )
    return pl.pallas_call(
        functools.partial(_nsa_sample_sel_body, past, n_top),
        grid_spec=grid_spec,
        out_shape=jax.ShapeDtypeStruct((b, NSA_KV_HEADS, 8, LANES), F32),
        compiler_params=_cparams(("arbitrary", "arbitrary")),
        name="nsa_sample_sel",
    )(idx, page_table.reshape(-1), *([sel_pool] * n_top), q8, new_s, new_w, win, oc, gl)


def _win_update_body(win_ref, new_ref, o_ref):
    w_buf = win_ref.shape[3]
    last = lax.broadcasted_iota(jnp.int32, (LANES, LANES), 1) == LANES - 1
    for g in range(NSA_KV_HEADS):
        shifted = pltpu.roll(win_ref[0, g], w_buf - 1, 1)
        new_col = jnp.broadcast_to(new_ref[0, g], (LANES, LANES)).T
        o_ref[0, g, :, 0:w_buf - LANES] = shifted[:, 0:w_buf - LANES]
        o_ref[0, g, :, w_buf - LANES:] = jnp.where(last, new_col, shifted[:, w_buf - LANES:])


def _win_update(layer, win, new):
    _, b, g, r, w_buf = win.shape
    return pl.pallas_call(
        _win_update_body,
        grid=(b,),
        in_specs=[pl.BlockSpec((None, 1, g, r, w_buf), lambda i: (layer, i, 0, 0, 0)),
                  pl.BlockSpec((1, g, 1, r), lambda i: (i, 0, 0, 0))],
        out_specs=pl.BlockSpec((1, g, r, w_buf), lambda i: (i, 0, 0, 0)),
        out_shape=jax.ShapeDtypeStruct((b, g, r, w_buf), F32),
        compiler_params=_cparams(("arbitrary",)),
        name="win_update",
    )(win, new)


def _finish_body(x_ref, ho_ref, no_ref, wo_ref, g1_ref, g2_ref, g3_ref, w1_ref, w2_ref, o_ref):
    wo = wo_ref[...]
    y = _dot(ho_ref[...].astype(BF16), wo[0:HG_W]) + _dot(no_ref[...].astype(BF16), wo[HG_W:])
    h = x_ref[...] + _rms(y, g1_ref[...])
    a = _dot(_rms(h, g2_ref[...]).astype(BF16), w1_ref[...])
    dff = a.shape[1] // 2
    u, v = a[:, :dff], a[:, dff:]
    act = (u * _sigmoid(u)) * v
    y2 = _dot(act.astype(BF16), w2_ref[...])
    o_ref[...] = h + _rms(y2, g3_ref[...])


def _finish(x2d, ho, no, wo, g1, g2, g3, w1, w2, tm):
    m, d = x2d.shape
    row = lambda w: pl.BlockSpec((tm, w), lambda i: (i, 0))
    full = lambda a: pl.BlockSpec(a.shape, lambda i: (0, 0), pipeline_mode=pl.Buffered(1))
    return pl.pallas_call(
        _finish_body,
        grid=(m // tm,),
        in_specs=[row(d), row(HG_W), row(NSA_W), full(wo), full(g1), full(g2), full(g3), full(w1), full(w2)],
        out_specs=row(d),
        out_shape=jax.ShapeDtypeStruct((m, d), F32),
        compiler_params=_cparams(("arbitrary",)),
        name="finish",
    )(x2d, ho, no, wo, g1, g2, g3, w1, w2)


def _tile(n, pref):
    while n % pref:
        pref //= 2
    return pref


def kernel(x_prompt, x_sample, state_hgrn, cache_cmp_kv, cache_sel_kv, cache_win_kv, page_table,
           norm_mix_pre, norm_mix_post, norm_ffn_pre, norm_ffn_post, w_in, w_out, hg_lb_logits,
           hg_out_norm, cmp_pe, w_cmp, w_ffn_in, w_ffn_out):
    bp, tp, d = x_prompt.shape
    bs, ts, _ = x_sample.shape
    assert ts == 1
    depth = w_in.shape[0]
    n_pages = page_table.shape[1]
    past = n_pages * PAGE_SIZE
    assert tp % PAGE_SIZE == 0 and tp >= WINDOW + PAGE_SIZE

    def token_minor(cache):
        return cache.transpose(0, 1, 3, 4, 5, 2).reshape(cache.shape[0], cache.shape[1], NSA_KV_HEADS,
                                                        2 * HEAD_DIM, cache.shape[2])

    def token_major(a):
        return a.reshape(a.shape[0], NSA_KV_HEADS, 2, HEAD_DIM, a.shape[3]).transpose(0, 4, 1, 2, 3)

    tm_p = _tile(tp, 512)
    tabs_p = _rope_tables(jnp.arange(tp))
    tabs_s = _rope_tables(jnp.full((bs * ts,), past, jnp.int32))
    cmp_pool, sel_pool, win_pool = token_minor(cache_cmp_kv), token_minor(cache_sel_kv), token_minor(cache_win_kv)
    prompt_pages = jnp.zeros((bp, tp // PAGE_SIZE), jnp.int32)
    n_sel_s = -(-(past + ts) // SEL_BLOCK)
    n_sel_pad_s = -(-n_sel_s // LANES) * LANES
    n_top_s = min(N_SELECT, n_sel_s)

    xp = x_prompt.reshape(bp * tp, d)
    xs = x_sample.reshape(bs * ts, d)
    hgp, hgs, cmpp, cmps, selp, sels, winp, wins = [], [], [], [], [], [], [], []
    for l in range(depth):
        w_in_bf = _pad_w_in(w_in[l])
        w_out_bf = w_out[l].astype(BF16)
        w1_bf = w_ffn_in[l].astype(BF16)
        w2_bf = w_ffn_out[l].astype(BF16)
        g_pre, g_post = norm_mix_pre[l][None, :], norm_mix_post[l][None, :]
        g_fpre, g_fpost = norm_ffn_pre[l][None, :], norm_ffn_post[l][None, :]
        gain = hg_out_norm[l][None, :]
        w_c, pe_c = _compress_weights(w_cmp[l], cmp_pe[l])

        hq, fl, hi, hg, qn, qr, kvc_t, kvs_t, kvw_t, gl = _inproj(xp, g_pre, w_in_bf, tabs_p, tm_p, True)
        r3 = lambda a: a.reshape(bp, tp, a.shape[-1])
        ho, s_fin = _hgrn_prompt(l, hg_lb_logits, r3(hq), r3(fl), r3(hi), r3(hg), gain, _tile(tp, 256))
        y, const = _compress(kvc_t, lambda i, pg, pt: (i, 0, 0, pg), prompt_pages, w_c, pe_c,
                             _tile(tp // PAGE_SIZE, CMP_PAGES_PER_STEP))
        no = _nsa_prompt(r3(qn), r3(qr), r3(gl), y, const, kvs_t, kvw_t, PAGE_SIZE, _tile(tp, SEL_KV_TILE))
        xp = _finish(xp, ho.reshape(bp * tp, HG_W), no.reshape(bp * tp, NSA_W), w_out_bf, g_post, g_fpre,
                     g_fpost, w1_bf, w2_bf, tm_p)
        hgp.append(s_fin)
        cmpp.append(token_major(kvc_t))
        selp.append(token_major(kvs_t))
        winp.append(token_major(kvw_t[..., tp - min(WINDOW, tp):]))

        hq, fl, hi, hg, qn, qr, kvc, kvs, kvw, gl = _inproj(xs, g_pre, w_in_bf, tabs_s, bs * ts, False)
        v3 = lambda a: a.reshape(bs, 1, a.shape[-1])
        s_new, ho = _hgrn_sample(l, hg_lb_logits, state_hgrn[l], v3(hq), v3(fl), v3(hi), v3(hg), gain)
        y, const = _compress(cmp_pool, lambda i, pg, pt, l=l: (l, pt[i * n_pages + pg], 0, 0, 0),
                             page_table, w_c, pe_c, _tile(n_pages, CMP_PAGES_PER_STEP))

        def q8(q):
            q4 = q.reshape(bs, NSA_KV_HEADS, GQA_REP, HEAD_DIM)
            return jnp.pad(q4, ((0, 0), (0, 0), (0, 8 - GQA_REP), (0, LANES - HEAD_DIM)))

        idx, oc = _nsa_sample_cmp(past, q8(qn), y, const, n_sel_pad_s, _tile(bs, 4))
        g4 = lambda a: a.reshape(bs, NSA_KV_HEADS, 1, LANES)
        o8 = _nsa_sample_sel(past, l, idx[:, :, 0, :n_top_s].reshape(-1), page_table, sel_pool, q8(qr),
                             g4(kvs), g4(kvw), win_pool, oc, g4(gl))
        no = o8[:, :, :GQA_REP, :HEAD_DIM].reshape(bs * ts, NSA_W)
        win_new = _win_update(l, win_pool, g4(kvw))
        xs = _finish(xs, ho.reshape(bs * ts, HG_W), no, w_out_bf, g_post, g_fpre, g_fpost, w1_bf, w2_bf,
                     bs * ts)
        kv5 = lambda a, t: a.reshape(-1, t, NSA_KV_HEADS, 2, HEAD_DIM)
        hgs.append(s_new)
        cmps.append(kv5(kvc, ts))
        sels.append(kv5(kvs, ts))
        wins.append(token_major(win_new))

    return (xp.reshape(bp, tp, d), xs.reshape(bs, ts, d), jnp.stack(hgp), jnp.stack(hgs), jnp.stack(cmpp),
            jnp.stack(cmps), jnp.stack(selp), jnp.stack(sels), jnp.stack(winp), jnp.stack(wins))
```

```python
import functools

import jax
import jax.numpy as jnp
import numpy as np
from jax import lax
from jax.experimental import pallas as pl
from jax.experimental.pallas import tpu as pltpu

F32 = jnp.float32
BF16 = jnp.bfloat16

HG_HEADS = 4
HG_DK = 128
HG_DV = 128
HG_W = HG_HEADS * HG_DV
HEAD_DIM = 64
NSA_HEADS = 8
NSA_KV_HEADS = 2
GQA_REP = NSA_HEADS // NSA_KV_HEADS
NSA_W = NSA_HEADS * HEAD_DIM
KV_W = NSA_KV_HEADS * 2 * HEAD_DIM
N_GATES = NSA_HEADS * 3
CMP_STRIDE = 16
CMP_RATIO = 2
CMP_BLOCK = CMP_STRIDE * CMP_RATIO
SEL_BLOCK = 64
N_SELECT = 16
WINDOW = 512
PAGE_SIZE = 128
ROPE_THETA = 500000.0
ROT_DIM = HEAD_DIM // 4
EPS = 1e-6
FORCED_SCORE = 1e6
NEG_BIG = -1e30
LB_FLOOR = 1e-30
REMOVED = -3e38
LOG2_E = 1.4426950408889634

LANES = 128
HG_CHUNK = 16
GATE_PAD = 2 * LANES
CMP_CHAIN_PAGES = 16
CMP_PAGES_PER_STEP = 128
SEL_KV_TILE = 1024
COL_QN = 4 * HG_W
COL_KVC = COL_QN + NSA_W
COL_KVS = COL_KVC + KV_W
COL_KVW = COL_KVS + KV_W
COL_GL = COL_KVW + KV_W
IN_W_PAD = COL_GL + GATE_PAD

VMEM_LIMIT = 56 * 1024 * 1024


def _cparams(sem):
    return pltpu.CompilerParams(dimension_semantics=sem, vmem_limit_bytes=VMEM_LIMIT)


def _rms(x, g):
    return x * lax.rsqrt(jnp.mean(x * x, axis=-1, keepdims=True) + EPS) * g


def _sigmoid(x):
    return 1.0 / (1.0 + jnp.exp(-x))


def _dot(a, b):
    return jnp.dot(a, b, preferred_element_type=F32)


def _dot_nt(a, b):
    return lax.dot_general(a, b, (((1,), (1,)), ((), ())), preferred_element_type=F32)


def _dot_tn(a, b):
    return lax.dot_general(a, b, (((0,), (0,)), ((), ())), preferred_element_type=F32)


def _div_pow2(x, n):
    assert n & (n - 1) == 0
    return x >> (n.bit_length() - 1)


def _split3(x):
    hi = x.astype(BF16)
    r1 = x - hi.astype(F32)
    mid = r1.astype(BF16)
    lo = (r1 - mid.astype(F32)).astype(BF16)
    return hi, mid, lo


def _rope128(x, c, s1, s2):
    return x * c + pltpu.roll(x, LANES - ROT_DIM // 2, 1) * s1 + pltpu.roll(x, ROT_DIM // 2, 1) * s2


def _inproj_body(kv_t, x_ref, g_ref, w_ref, cq_ref, s1q_ref, s2q_ref,
                 hq_ref, fl_ref, hi_ref, hg_ref, qn_ref, qr_ref, kvc_ref, kvs_ref, kvw_ref, gl_ref):
    def put_kv(ref, j, blk):
        if kv_t:
            ref[0, j] = blk.T
        else:
            ref[:, j * LANES:(j + 1) * LANES] = blk

    hn = _rms(x_ref[...], g_ref[...])
    z = _dot(hn.astype(BF16), w_ref[...])
    hq_ref[...] = z[:, 0:HG_W]
    fl_ref[...] = z[:, HG_W:2 * HG_W]
    hi_ref[...] = z[:, 2 * HG_W:3 * HG_W]
    hg_ref[...] = z[:, 3 * HG_W:4 * HG_W]
    scale = HEAD_DIM ** -0.5 * LOG2_E
    cq, s1q, s2q = cq_ref[...], s1q_ref[...], s2q_ref[...]
    for j in range(NSA_W // LANES):
        blk = z[:, COL_QN + j * LANES:COL_QN + (j + 1) * LANES]
        qn_ref[:, j * LANES:(j + 1) * LANES] = (blk * scale).astype(BF16)
        qr_ref[:, j * LANES:(j + 1) * LANES] = (_rope128(blk, cq, s1q, s2q) * scale).astype(BF16)
    k_half = lax.broadcasted_iota(jnp.int32, cq.shape, 1) < HEAD_DIM
    ck, s1k, s2k = jnp.where(k_half, cq, 1.0), jnp.where(k_half, s1q, 0.0), jnp.where(k_half, s2q, 0.0)
    for j in range(NSA_KV_HEADS):
        cols = lambda c0: z[:, c0 + j * LANES:c0 + (j + 1) * LANES]
        put_kv(kvc_ref, j, cols(COL_KVC))
        put_kv(kvs_ref, j, _rope128(cols(COL_KVS), ck, s1k, s2k))
        put_kv(kvw_ref, j, _rope128(cols(COL_KVW), ck, s1k, s2k))
    gl_ref[...] = z[:, COL_GL:COL_GL + GATE_PAD]


def _inproj(x2d, gain, w_bf, tabs, tm, kv_t):
    m, d = x2d.shape
    n_pos_blocks = tabs[0].shape[0] // tm
    row = lambda i: (i, 0)
    tab = lambda i: (i % n_pos_blocks, 0)
    const = lambda i: (0, 0)
    widths = (HG_W, HG_W, HG_W, HG_W, NSA_W, NSA_W, GATE_PAD)
    dtypes = (F32, F32, F32, F32, BF16, BF16, F32)
    specs = [pl.BlockSpec((tm, w), row) for w in widths]
    shapes = [jax.ShapeDtypeStruct((m, w), dt) for w, dt in zip(widths, dtypes)]
    if kv_t:
        t = tabs[0].shape[0]
        kv_spec = pl.BlockSpec((1, NSA_KV_HEADS, LANES, tm), lambda i: (i // n_pos_blocks, 0, 0, i % n_pos_blocks))
        kv_shape = jax.ShapeDtypeStruct((m // t, NSA_KV_HEADS, LANES, t), F32)
    else:
        kv_spec = pl.BlockSpec((tm, KV_W), row)
        kv_shape = jax.ShapeDtypeStruct((m, KV_W), F32)
    return pl.pallas_call(
        functools.partial(_inproj_body, kv_t),
        grid=(m // tm,),
        in_specs=[pl.BlockSpec((tm, d), row), pl.BlockSpec((1, d), const),
                  pl.BlockSpec((d, IN_W_PAD), const, pipeline_mode=pl.Buffered(1))]
        + [pl.BlockSpec((tm, LANES), tab)] * 3,
        out_specs=specs[:6] + [kv_spec] * 3 + specs[6:],
        out_shape=shapes[:6] + [kv_shape] * 3 + shapes[6:],
        compiler_params=_cparams(("arbitrary",)),
        name="inproj",
    )(x2d, gain, w_bf, *tabs)


def _rope_tables(pos):
    half = ROT_DIM // 2
    inv = ROPE_THETA ** (-2.0 * jnp.arange(half, dtype=F32) / ROT_DIM)
    ang = pos.astype(F32)[:, None] * inv[None, :]
    cos, sin = jnp.cos(ang), jnp.sin(ang)
    p = pos.shape[0]
    one = jnp.ones((p, HEAD_DIM - ROT_DIM), F32)
    zero = jnp.zeros((p, HEAD_DIM - ROT_DIM), F32)
    zh = jnp.zeros((p, half), F32)
    c64 = jnp.concatenate([cos, cos, one], axis=1)
    s1_64 = jnp.concatenate([-sin, zh, zero], axis=1)
    s2_64 = jnp.concatenate([zh, sin, zero], axis=1)
    return tuple(jnp.concatenate([t, t], axis=1) for t in (c64, s1_64, s2_64))


def _pad_w_in(w):
    d = w.shape[0]
    per_g = GQA_REP * 3
    pad = jnp.zeros((d, LANES - per_g), w.dtype)
    gates = [jnp.concatenate([w[:, COL_GL + g * per_g:COL_GL + (g + 1) * per_g], pad], axis=1)
             for g in range(NSA_KV_HEADS)]
    return jnp.concatenate([w[:, :COL_GL]] + gates, axis=1).astype(BF16)


def _lower_bound(lbl_ref, layer):
    rows = [lbl_ref[l:l + 1, :] for l in range(lbl_ref.shape[0])]
    mx = functools.reduce(jnp.maximum, rows)
    ex = [jnp.exp(r - mx) for r in rows]
    den = functools.reduce(lambda a, b: a + b, ex)
    sm = [e / den for e in ex]
    cum = sm[0]
    for l in range(1, layer + 1):
        cum = cum + sm[l]
    return cum - sm[0]


def _forget_gate(fl, lb):
    logsig = jnp.minimum(fl, 0.0) - jnp.log1p(jnp.exp(-jnp.abs(fl)))
    a = jnp.log(jnp.maximum(lb, LB_FLOOR))
    b = jnp.log1p(-lb) + logsig
    logf = jnp.maximum(a, b) + jnp.log1p(jnp.exp(-jnp.abs(a - b)))
    kh = (1.0 - lb) * _sigmoid(-fl)
    return logf, kh


def _readout(o, g, gain):
    on = o * lax.rsqrt(jnp.mean(o * o, axis=-1, keepdims=True) + EPS) * gain
    return on * (g * _sigmoid(g))


def _hgrn_prompt_body(layer, lbl_ref, q_ref, f_ref, i_ref, g_ref, gain_ref, o_ref, s_ref,
                      st_sc, k_sc, b_sc, o_sc):
    t = pl.program_id(1)
    ct = q_ref.shape[1]

    @pl.when(t == 0)
    def _():
        st_sc[...] = jnp.zeros_like(st_sc)

    lb = _lower_bound(lbl_ref, layer)
    logf, kh = _forget_gate(f_ref[0], lb)
    k_sc[...] = kh
    ri = lax.broadcasted_iota(jnp.int32, (ct, ct), 0)
    ci = lax.broadcasted_iota(jnp.int32, (ct, ct), 1)
    tri = jnp.where((ci <= ri) & (_div_pow2(ci, HG_CHUNK) == _div_pow2(ri, HG_CHUNK)), 1.0, 0.0).astype(BF16)
    hi, mid, lo = _split3(logf)
    b_sc[...] = _dot(tri, hi) + _dot(tri, mid) + _dot(tri, lo)

    row = lax.broadcasted_iota(jnp.int32, (HG_CHUNK, HG_DK), 0)

    def chunk(c, carry):
        r0 = pl.multiple_of(c * HG_CHUNK, HG_CHUNK)
        for h in range(HG_HEADS):
            hl = slice(h * HG_DK, (h + 1) * HG_DK)
            q = q_ref[0, pl.ds(r0, HG_CHUNK), hl]
            k = k_sc[pl.ds(r0, HG_CHUNK), hl]
            v = i_ref[0, pl.ds(r0, HG_CHUNK), hl]
            b = b_sc[pl.ds(r0, HG_CHUNK), hl]
            st = st_sc[h]
            o = _dot_nt((q * jnp.exp(b)).astype(BF16), st.astype(BF16))
            for s in range(HG_CHUNK):
                w = jnp.where(row >= s, jnp.exp(b - b[s:s + 1, :]), 0.0)
                d = jnp.sum(q * (k[s:s + 1, :] * w), axis=-1, keepdims=True)
                o = o + d * v[s:s + 1, :]
            o_sc[pl.ds(r0, HG_CHUNK), hl] = o
            blast = b[HG_CHUNK - 1:HG_CHUNK, :]
            kdec = k * jnp.exp(blast - b)
            st_sc[h] = st * jnp.exp(blast) + _dot_tn(v.astype(BF16), kdec.astype(BF16))
        return carry

    lax.fori_loop(0, ct // HG_CHUNK, chunk, 0)

    gain = gain_ref[...]
    for h in range(HG_HEADS):
        hl = slice(h * HG_DV, (h + 1) * HG_DV)
        o_ref[0, :, hl] = _readout(o_sc[:, hl], g_ref[0, :, hl], gain)

    @pl.when(t == pl.num_programs(1) - 1)
    def _():
        for h in range(HG_HEADS):
            s_ref[0, h] = st_sc[h].T


def _hgrn_prompt(layer, lbl, hq, fl, hi, hg, gain, ct):
    b, t, _ = hq.shape
    tile = pl.BlockSpec((1, ct, HG_W), lambda i, j: (i, j, 0))
    return pl.pallas_call(
        functools.partial(_hgrn_prompt_body, layer),
        grid=(b, t // ct),
        in_specs=[pl.BlockSpec(lbl.shape, lambda i, j: (0, 0)), tile, tile, tile, tile,
                  pl.BlockSpec((1, HG_DV), lambda i, j: (0, 0))],
        out_specs=[tile, pl.BlockSpec((1, HG_HEADS, HG_DK, HG_DV), lambda i, j: (i, 0, 0, 0))],
        out_shape=[jax.ShapeDtypeStruct((b, t, HG_W), F32),
                   jax.ShapeDtypeStruct((b, HG_HEADS, HG_DK, HG_DV), F32)],
        scratch_shapes=[pltpu.VMEM((HG_HEADS, HG_DV, HG_DK), F32), pltpu.VMEM((ct, HG_W), F32),
                        pltpu.VMEM((ct, HG_W), F32), pltpu.VMEM((ct, HG_W), F32)],
        compiler_params=_cparams(("arbitrary", "arbitrary")),
        name="hgrn_prompt",
    )(lbl, hq, fl, hi, hg, gain)


def _hgrn_sample_body(layer, lbl_ref, s0_ref, q_ref, f_ref, i_ref, g_ref, gain_ref, s_ref, o_ref):
    lb = _lower_bound(lbl_ref, layer)
    logf, kh = _forget_gate(f_ref[0], lb)
    q, v, g = q_ref[0], i_ref[0], g_ref[0]
    gain = gain_ref[...]

    def col(x):
        return jnp.broadcast_to(x, (HG_DK, HG_DK)).T

    for h in range(HG_HEADS):
        hl = slice(h * HG_DK, (h + 1) * HG_DK)
        s_new = col(jnp.exp(logf[:, hl])) * s0_ref[0, h] + col(kh[:, hl]) * v[:, hl]
        s_ref[0, h] = s_new
        o = jnp.sum(col(q[:, hl]) * s_new, axis=0, keepdims=True)
        o_ref[0, :, hl] = _readout(o, g[:, hl], gain)


def _hgrn_sample(layer, lbl, s0, hq, fl, hi, hg, gain):
    b = s0.shape[0]
    vec = pl.BlockSpec((1, 1, HG_W), lambda i: (i, 0, 0))
    st = pl.BlockSpec((1, HG_HEADS, HG_DK, HG_DV), lambda i: (i, 0, 0, 0))
    return pl.pallas_call(
        functools.partial(_hgrn_sample_body, layer),
        grid=(b,),
        in_specs=[pl.BlockSpec(lbl.shape, lambda i: (0, 0)), st, vec, vec, vec, vec,
                  pl.BlockSpec((1, HG_DV), lambda i: (0, 0))],
        out_specs=[st, vec],
        out_shape=[jax.ShapeDtypeStruct(s0.shape, F32), jax.ShapeDtypeStruct((b, 1, HG_W), F32)],
        compiler_params=_cparams(("arbitrary",)),
        name="hgrn_sample",
    )(lbl, s0, hq, fl, hi, hg, gain)


def _compress_body(n_pages, *refs):
    page_refs = refs[1:1 + n_pages]
    w_ref, pe_ref, y_ref, const_ref = refs[1 + n_pages:]
    rows = PAGE_SIZE // CMP_STRIDE
    tok = lax.broadcasted_iota(jnp.int32, (PAGE_SIZE, PAGE_SIZE), 0)
    col = lax.broadcasted_iota(jnp.int32, (PAGE_SIZE, PAGE_SIZE), 1)
    perm = jnp.where(tok == CMP_STRIDE * (col & (rows - 1)) + _div_pow2(col, rows), 1.0, 0.0).astype(BF16)
    n_split = n_pages // CMP_CHAIN_PAGES if n_pages % CMP_CHAIN_PAGES == 0 else 1
    per = n_pages // n_split
    for g in range(NSA_KV_HEADS):
        for h in range(n_split):
            page_ids = range(h * per, (h + 1) * per)
            pages = jnp.concatenate([page_refs[p][g].astype(BF16) for p in page_ids], axis=0)
            permuted = _dot(pages, perm)
            pieces = [permuted[i * LANES:(i + 1) * LANES, :].T for i in range(per)]
            lhs = jnp.concatenate(
                [jnp.concatenate([pc[c * rows:(c + 1) * rows, :] for pc in pieces], axis=0)
                 for c in range(CMP_STRIDE)], axis=1)
            y_ref[0, h * per * rows:(h + 1) * per * rows, g * CMP_RATIO * LANES:(g + 1) * CMP_RATIO * LANES] = (
                _dot(lhs.astype(BF16), w_ref[...]))

    @pl.when((pl.program_id(0) == 0) & (pl.program_id(1) == 0))
    def _():
        tot = jnp.zeros((8, LANES), F32)
        for c in range(CMP_STRIDE):
            for m in range(CMP_RATIO):
                pe_row = jnp.broadcast_to(pe_ref[m, c:c + 1, :], (8, LANES))
                tot = tot + _dot(pe_row.astype(BF16), w_ref[c * LANES:(c + 1) * LANES, m * LANES:(m + 1) * LANES])
        const_ref[...] = tot


def _compress(pages, page_index, page_table, w_c, pe_c, n_pages):
    s, p = page_table.shape
    rows = PAGE_SIZE // CMP_STRIDE
    steps = p // n_pages

    def page_spec(k):
        return pl.BlockSpec((None,) * (pages.ndim - 3) + (NSA_KV_HEADS, LANES, PAGE_SIZE),
                            lambda i, j, pt: page_index(i, j * n_pages + k, pt))

    page_specs = [page_spec(k) for k in range(n_pages)]
    grid_spec = pltpu.PrefetchScalarGridSpec(
        num_scalar_prefetch=1,
        grid=(s, steps),
        in_specs=page_specs + [
            pl.BlockSpec(w_c.shape, lambda i, j, pt: (0, 0)),
            pl.BlockSpec(pe_c.shape, lambda i, j, pt: (0, 0, 0))],
        out_specs=[pl.BlockSpec((1, n_pages * rows, NSA_KV_HEADS * CMP_RATIO * LANES),
                                lambda i, j, pt: (i, j, 0)),
                   pl.BlockSpec((8, LANES), lambda i, j, pt: (0, 0))],
    )
    return pl.pallas_call(
        functools.partial(_compress_body, n_pages),
        grid_spec=grid_spec,
        out_shape=[jax.ShapeDtypeStruct((s, p * rows, NSA_KV_HEADS * CMP_RATIO * LANES), F32),
                   jax.ShapeDtypeStruct((8, LANES), F32)],
        compiler_params=_cparams(("arbitrary", "arbitrary")),
        name="compress",
    )(page_table.reshape(-1), *([pages] * n_pages), w_c, pe_c)


def _compress_weights(w_cmp, pe):
    w_r = w_cmp.reshape(2, CMP_RATIO, CMP_STRIDE, HEAD_DIM, HEAD_DIM)
    z = jnp.zeros((CMP_STRIDE, HEAD_DIM, HEAD_DIM), w_cmp.dtype)
    blocks = []
    for m in range(CMP_RATIO):
        top = jnp.concatenate([w_r[0, m], z], axis=2)
        bot = jnp.concatenate([z, w_r[1, m]], axis=2)
        blocks.append(jnp.concatenate([top, bot], axis=1))
    w_c = jnp.concatenate(blocks, axis=2).astype(BF16).reshape(CMP_STRIDE * LANES, CMP_RATIO * LANES)
    pe_r = pe.reshape(2, CMP_RATIO, CMP_STRIDE, HEAD_DIM)
    pe_c = jnp.concatenate([pe_r[0], pe_r[1]], axis=-1)
    return w_c, pe_c


def _cmp_from_y(y0, y1, const):
    body = y0[:-1, :] + y1[1:, :] + const
    return jnp.concatenate([body, jnp.zeros((1, LANES), F32)], axis=0)


def _kv_operands(kv):
    lane = lax.broadcasted_iota(jnp.int32, kv.shape, 1)
    v1 = jnp.where(lane < HEAD_DIM, pltpu.roll(kv, HEAD_DIM, 1), 1.0)
    return kv.astype(BF16), v1.astype(BF16)


def _stack_heads(qblk):
    q = qblk.astype(F32)
    lane = lax.broadcasted_iota(jnp.int32, (q.shape[0], LANES), 1)
    outs = []
    for r in range(GQA_REP):
        grp = q[:, (r // 2) * LANES:(r // 2 + 1) * LANES]
        if r % 2:
            grp = pltpu.roll(grp, HEAD_DIM, 1)
        outs.append(jnp.where(lane < HEAD_DIM, grp, 0.0))
    return jnp.concatenate(outs, axis=0).astype(BF16)


def _overlap_matrix(n_cmp_pad, n_sel_pad):
    n = lax.broadcasted_iota(jnp.int32, (n_cmp_pad, n_sel_pad), 0) * CMP_STRIDE
    j = lax.broadcasted_iota(jnp.int32, (n_cmp_pad, n_sel_pad), 1) * SEL_BLOCK
    return jnp.where((n < j + SEL_BLOCK) & (n + CMP_BLOCK > j), 1.0, 0.0).astype(BF16)


def _overlap_matrix_t(n_sel_pad, n_cmp_pad):
    j = lax.broadcasted_iota(jnp.int32, (n_sel_pad, n_cmp_pad), 0) * SEL_BLOCK
    n = lax.broadcasted_iota(jnp.int32, (n_sel_pad, n_cmp_pad), 1) * CMP_STRIDE
    return jnp.where((n < j + SEL_BLOCK) & (n + CMP_BLOCK > j), 1.0, 0.0).astype(BF16)


def _kv_operands_t(kvt):
    v1 = jnp.concatenate([kvt[HEAD_DIM:], jnp.ones((HEAD_DIM, kvt.shape[1]), F32)], axis=0)
    return kvt.astype(BF16), v1.astype(BF16)


def _topk_axis0(score, n_top):
    jf = lax.broadcasted_iota(jnp.int32, score.shape, 0).astype(F32)
    big = float(score.shape[0])
    sel = jnp.zeros(score.shape, F32)
    for _ in range(n_top):
        m = jnp.max(score, axis=0, keepdims=True)
        first = jnp.min(jnp.where(score == m, jf, big), axis=0, keepdims=True)
        pick = jf == first
        sel = jnp.where(pick & (m > 0.5 * NEG_BIG), 1.0, sel)
        score = jnp.where(pick, REMOVED, score)
    return sel


def _nsa_prompt_body(kt, qn_ref, qr_ref, gl_ref, y_ref, const_ref, kvs_ref, kvw_ref, o_ref,
                     kc_sc, vc_sc, ks_sc, vs_sc, kw_sc, vw_sc):
    qt = pl.program_id(2)
    tq = qn_ref.shape[1]
    t_len = kvs_ref.shape[3]
    n_cmp_pad = y_ref.shape[1]
    n_sel_pad = t_len // SEL_BLOCK
    rows = GQA_REP * tq

    @pl.when(qt == 0)
    def _():
        y = y_ref[0]
        kv_cmp = _cmp_from_y(y[:, 0:LANES], y[:, LANES:2 * LANES], const_ref[0:1, :])
        kc_sc[...], vc_sc[...] = _kv_operands(kv_cmp)
        kop, vs_sc[...] = _kv_operands_t(kvs_ref[0, 0])
        ks_sc[0:LANES, :] = kop
        j = lax.broadcasted_iota(jnp.int32, (n_sel_pad, t_len), 0)
        tok = lax.broadcasted_iota(jnp.int32, (n_sel_pad, t_len), 1)
        ks_sc[LANES:, :] = jnp.where(j == _div_pow2(tok & (kt - 1), SEL_BLOCK), 1.0, 0.0).astype(BF16)
        kw_sc[...], vw_sc[...] = _kv_operands_t(kvw_ref[0, 0])

    def cmp_and_select(qn_blk, tile):
        t0 = tile * tq
        pos_rows = t0 + lax.broadcasted_iota(jnp.int32, (tq, 1), 0)
        pos_rows4 = jnp.concatenate([pos_rows] * GQA_REP, axis=0)
        s = _dot_nt(_stack_heads(qn_blk), kc_sc[...])
        cmp_end = lax.broadcasted_iota(jnp.int32, (1, n_cmp_pad), 1) * CMP_STRIDE + (CMP_BLOCK - 1)
        vis = cmp_end <= pos_rows4
        s = jnp.where(vis, s, NEG_BIG)
        m = jnp.max(s, axis=-1, keepdims=True)
        p = jnp.where(vis, jnp.exp2(s - m), 0.0)
        p = p / jnp.maximum(jnp.sum(p, axis=-1, keepdims=True), 1e-30)
        acc = _dot(p.astype(BF16), vc_sc[...])
        psum = p[0:tq]
        for r in range(1, GQA_REP):
            psum = psum + p[r * tq:(r + 1) * tq]
        ov_t = _overlap_matrix_t(n_sel_pad, n_cmp_pad)
        hi, mid, lo = _split3(psum)
        imp_t = _dot_nt(ov_t, hi) + _dot_nt(ov_t, mid) + _dot_nt(ov_t, lo)
        blk = lax.broadcasted_iota(jnp.int32, (n_sel_pad, tq), 0)
        pos_l = t0 + lax.broadcasted_iota(jnp.int32, (1, tq), 1)
        cur = _div_pow2(pos_l, SEL_BLOCK)
        valid = blk * SEL_BLOCK <= pos_l
        forced = (blk == 0) | (blk == cur) | (blk == cur - 1)
        n_forced = 3
        score = jnp.where(valid & jnp.logical_not(forced), imp_t, jnp.where(valid, REMOVED, NEG_BIG))
        sel_t = _topk_axis0(score, min(N_SELECT, n_sel_pad) - n_forced)
        sel_t = jnp.where(valid & forced, 1.0, sel_t)
        return acc, jnp.where(sel_t.T > 0.5, 0.0, NEG_BIG)

    q0 = qt * tq
    pos = q0 + lax.broadcasted_iota(jnp.int32, (tq, 1), 0)
    pos4 = jnp.concatenate([pos] * GQA_REP, axis=0)
    qr = _stack_heads(qr_ref[0])

    wlen = WINDOW + tq
    w0 = pl.multiple_of(jnp.maximum(q0 - WINDOW, 0), tq)
    s_w = _dot(qr, kw_sc[:, pl.ds(w0, wlen)])
    dist = pos4 - (w0 + lax.broadcasted_iota(jnp.int32, (1, wlen), 1))
    ok_w = (dist >= 0) & (dist < WINDOW)
    s_w = jnp.where(ok_w, s_w, NEG_BIG)
    m_w = jnp.max(s_w, axis=-1, keepdims=True)
    p_w = jnp.where(ok_w, jnp.exp2(s_w - m_w), 0.0)
    acc_w = _dot_nt(p_w.astype(BF16), vw_sc[:, pl.ds(w0, wlen)])

    acc_c, sel_bias = cmp_and_select(qn_ref[0], qt)

    def sel_tile(k0, m_i, acc, causal):
        shift = (n_sel_pad - k0 // SEL_BLOCK) & (n_sel_pad - 1)
        bias = pltpu.roll(sel_bias, shift, 1).astype(BF16)
        lhs = jnp.concatenate([qr, jnp.concatenate([bias] * GQA_REP, axis=0)], axis=1)
        s_i = _dot(lhs, ks_sc[:, pl.ds(k0, kt)])
        if causal:
            tok = k0 + lax.broadcasted_iota(jnp.int32, (1, kt), 1)
            s_i = jnp.where(tok <= pos4, s_i, NEG_BIG)
        m_new = jnp.maximum(m_i, jnp.max(s_i, axis=-1, keepdims=True))
        p_i = jnp.exp2(s_i - m_new)
        acc = acc * jnp.exp2(m_i - m_new) + _dot_nt(p_i.astype(BF16), vs_sc[:, pl.ds(k0, kt)])
        return m_new, acc

    assert n_sel_pad & (n_sel_pad - 1) == 0 and kt // SEL_BLOCK <= n_sel_pad

    def past_tiles(i, c, n):
        for u in range(n):
            c = sel_tile(pl.multiple_of((i * n + u) * kt, kt), c[0], c[1], False)
        return c

    n_full = q0 // kt
    carry = (jnp.full((rows, 1), NEG_BIG, F32), jnp.zeros((rows, LANES), F32))
    carry = lax.fori_loop(0, n_full // 4, lambda i, c: past_tiles(i, c, 4), carry)
    carry = lax.fori_loop((n_full // 4) * 2, n_full // 2, lambda i, c: past_tiles(i, c, 2), carry)
    carry = lax.fori_loop((n_full // 2) * 2, n_full, lambda i, c: past_tiles(i, c, 1), carry)
    _, acc_s = sel_tile(pl.multiple_of(n_full * kt, kt), carry[0], carry[1], True)

    def norm(acc):
        return acc / jnp.maximum(acc[:, HEAD_DIM:HEAD_DIM + 1], 1e-30)

    o_c, o_s, o_w = acc_c, norm(acc_s), norm(acc_w)
    gate = _sigmoid(gl_ref[0])
    lane = lax.broadcasted_iota(jnp.int32, (tq, LANES), 1)
    heads = []
    for r in range(GQA_REP):
        rs = slice(r * tq, (r + 1) * tq)
        heads.append(gate[:, 3 * r:3 * r + 1] * o_c[rs] + gate[:, 3 * r + 1:3 * r + 2] * o_s[rs]
                     + gate[:, 3 * r + 2:3 * r + 3] * o_w[rs])
    for j in range(GQA_REP // 2):
        o_ref[0, :, j * LANES:(j + 1) * LANES] = jnp.where(
            lane < HEAD_DIM, heads[2 * j], pltpu.roll(heads[2 * j + 1], HEAD_DIM, 1))


def _nsa_prompt(qn, qr, gl, y, const, kvs_t, kvw_t, tq, kt):
    b, _, _, t = kvs_t.shape
    n_cmp_pad = y.shape[1]
    n_sel_pad = t // SEL_BLOCK
    gw = GQA_REP * HEAD_DIM
    qspec = pl.BlockSpec((1, tq, gw), lambda i, g, j: (i, j, g))
    res = pl.BlockSpec((1, 1, LANES, t), lambda i, g, j: (i, g, 0, 0))
    return pl.pallas_call(
        functools.partial(_nsa_prompt_body, kt),
        grid=(b, NSA_KV_HEADS, t // tq),
        in_specs=[qspec, qspec, pl.BlockSpec((1, tq, LANES), lambda i, g, j: (i, j, g)),
                  pl.BlockSpec((1, n_cmp_pad, CMP_RATIO * LANES), lambda i, g, j: (i, 0, g)),
                  pl.BlockSpec((8, LANES), lambda i, g, j: (0, 0)), res, res],
        out_specs=qspec,
        out_shape=jax.ShapeDtypeStruct((b, t, NSA_W), F32),
        scratch_shapes=[pltpu.VMEM((n_cmp_pad, LANES), BF16), pltpu.VMEM((n_cmp_pad, LANES), BF16),
                        pltpu.VMEM((LANES + n_sel_pad, t), BF16)] + [pltpu.VMEM((LANES, t), BF16)] * 3,
        compiler_params=_cparams(("arbitrary", "arbitrary", "arbitrary")),
        name="nsa_prompt",
    )(qn, qr, gl, y, const, kvs_t, kvw_t)


def _nsa_sample_cmp_body(past, q_ref, y_ref, const_ref, idx_ref, oc_ref):
    sb = q_ref.shape[0]
    n_cmp_pad = y_ref.shape[1]
    n_sel = -(-(past + 1) // SEL_BLOCK)
    n_sel_pad = idx_ref.shape[3]
    n_rows = sb * NSA_KV_HEADS
    ov = _overlap_matrix(n_cmp_pad, n_sel_pad)
    cmp_end = lax.broadcasted_iota(jnp.int32, (1, n_cmp_pad), 1) * CMP_STRIDE + (CMP_BLOCK - 1)
    vis = cmp_end <= past
    row = lax.broadcasted_iota(jnp.int32, (n_rows, n_cmp_pad), 0)
    psum_all = jnp.zeros((n_rows, n_cmp_pad), F32)
    for s_i in range(sb):
        y = y_ref[s_i]
        for g in range(NSA_KV_HEADS):
            c0 = g * CMP_RATIO * LANES
            kv_cmp = _cmp_from_y(y[:, c0:c0 + LANES], y[:, c0 + LANES:c0 + 2 * LANES], const_ref[0:1, :])
            kc, vc = _kv_operands(kv_cmp)
            s = _dot_nt(q_ref[s_i, g], kc)
            s = jnp.where(vis, s, NEG_BIG)
            m = jnp.max(s, axis=-1, keepdims=True)
            p = jnp.where(vis, jnp.exp2(s - m), 0.0)
            p = p / jnp.maximum(jnp.sum(p, axis=-1, keepdims=True), 1e-30)
            oc_ref[s_i, g] = _dot(p.astype(BF16), vc)
            psum = jnp.sum(p[0:GQA_REP], axis=0, keepdims=True)
            psum_all = jnp.where(row == s_i * NSA_KV_HEADS + g, psum, psum_all)
    hi, mid, lo = _split3(psum_all)
    imp = _dot(hi, ov) + _dot(mid, ov) + _dot(lo, ov)
    blk = lax.broadcasted_iota(jnp.int32, (n_rows, n_sel_pad), 1)
    cur = past // SEL_BLOCK
    valid = (blk * SEL_BLOCK <= past) & (blk < n_sel)
    forced = (blk == 0) | (blk == cur) | (blk == cur - 1)
    jf = blk.astype(F32)
    score = jnp.where(valid, jnp.where(forced, FORCED_SCORE, imp), NEG_BIG)
    idx = jnp.full((n_rows, n_sel_pad), -1.0, F32)
    for k in range(min(N_SELECT, n_sel)):
        mx = jnp.max(score, axis=-1, keepdims=True)
        first = jnp.min(jnp.where(score == mx, jf, float(n_sel_pad)), axis=-1, keepdims=True)
        idx = jnp.where((blk == k) & (mx > 0.5 * NEG_BIG), first, idx)
        score = jnp.where(jf == first, REMOVED, score)
    idx = idx.astype(jnp.int32)
    for s_i in range(sb):
        for g in range(NSA_KV_HEADS):
            r = s_i * NSA_KV_HEADS + g
            idx_ref[s_i, g] = jnp.broadcast_to(idx[r:r + 1, :], (8, n_sel_pad))


def _nsa_sample_cmp(past, q8, y, const, n_sel_pad, sb):
    b = q8.shape[0]
    n_cmp_pad = y.shape[1]
    blk4 = lambda w: pl.BlockSpec((sb, NSA_KV_HEADS, 8, w), lambda i: (i, 0, 0, 0))
    return pl.pallas_call(
        functools.partial(_nsa_sample_cmp_body, past),
        grid=(b // sb,),
        in_specs=[blk4(LANES), pl.BlockSpec((sb, n_cmp_pad, y.shape[2]), lambda i: (i, 0, 0)),
                  pl.BlockSpec((8, LANES), lambda i: (0, 0))],
        out_specs=[blk4(n_sel_pad), blk4(LANES)],
        out_shape=[jax.ShapeDtypeStruct((b, NSA_KV_HEADS, 8, n_sel_pad), jnp.int32),
                   jax.ShapeDtypeStruct((b, NSA_KV_HEADS, 8, LANES), F32)],
        compiler_params=_cparams(("arbitrary",)),
        name="nsa_sample_cmp",
    )(q8, y, const)


def _nsa_sample_sel_body(past, n_top, *refs):
    idx_ref, pt_ref = refs[0], refs[1]
    del pt_ref
    blk_refs = refs[2:2 + n_top]
    q_ref, new_s_ref, new_w_ref, win_ref, oc_ref, gl_ref, o_ref, win_out_ref = refs[2 + n_top:]
    b, g = pl.program_id(0), pl.program_id(1)
    q = q_ref[0, 0]
    n_past_blocks = past // SEL_BLOCK

    def attend(pieces):
        ms = [jnp.max(jnp.where(ok, s, NEG_BIG), axis=-1, keepdims=True) for s, ok, _, _ in pieces]
        m = functools.reduce(jnp.maximum, ms)
        acc = jnp.zeros((8, LANES), F32)
        for s, ok, v, v_t in pieces:
            p = jnp.where(ok, jnp.exp2(s - m), 0.0).astype(BF16)
            acc = acc + (_dot_nt(p, v) if v_t else _dot(p, v))
        return acc

    def new_token(ref, enabled):
        kop, vop = _kv_operands(jnp.broadcast_to(ref[0, 0], (8, LANES)))
        first = lax.broadcasted_iota(jnp.int32, (8, 8), 1) == 0
        return _dot_nt(q, kop), first & enabled, vop, False

    per_page = PAGE_SIZE // SEL_BLOCK
    tok = lax.broadcasted_iota(jnp.int32, (8, n_top * PAGE_SIZE), 1)
    ok = tok < 0
    has_new = jnp.zeros((8, 8), jnp.int32)
    for k in range(n_top):
        j = idx_ref[(b * NSA_KV_HEADS + g) * n_top + k]
        lo = k * PAGE_SIZE + jnp.where((j >= 0) & (j < n_past_blocks), (j % per_page) * SEL_BLOCK, PAGE_SIZE)
        ok = ok | ((tok >= lo) & (tok < lo + SEL_BLOCK) & (tok < (k + 1) * PAGE_SIZE))
        has_new = has_new + jnp.where(j == n_past_blocks, 1, 0)
    kop, vop = _kv_operands_t(jnp.concatenate([blk_refs[k][...] for k in range(n_top)], axis=1))
    acc_s = attend([(_dot(q, kop), ok, vop, True), new_token(new_s_ref, has_new > 0)])

    win = win_ref[0]
    w_buf = win.shape[1]
    kw, vw = _kv_operands_t(win)
    dist = w_buf - lax.broadcasted_iota(jnp.int32, (8, w_buf), 1)
    ok_buf = (dist < WINDOW) & (past - dist >= 0)
    acc_w = attend([(_dot(q, kw), ok_buf, vw, True), new_token(new_w_ref, jnp.ones((8, 8), jnp.int32) > 0)])

    def norm(acc):
        return acc / jnp.maximum(acc[:, HEAD_DIM:HEAD_DIM + 1], 1e-30)

    gate = _sigmoid(gl_ref[0, 0])
    rsel = lax.broadcasted_iota(jnp.int32, (8, LANES), 0)
    gc = jnp.zeros((8, LANES), F32)
    gs = jnp.zeros((8, LANES), F32)
    gw = jnp.zeros((8, LANES), F32)
    for r in range(GQA_REP):
        gc = jnp.where(rsel == r, gate[:, 3 * r:3 * r + 1], gc)
        gs = jnp.where(rsel == r, gate[:, 3 * r + 1:3 * r + 2], gs)
        gw = jnp.where(rsel == r, gate[:, 3 * r + 2:3 * r + 3], gw)
    o_ref[0, 0] = gc * oc_ref[0, 0] + gs * norm(acc_s) + gw * norm(acc_w)

    shifted = pltpu.roll(win, w_buf - 1, 1)
    new_col = jnp.broadcast_to(new_w_ref[0, 0], (LANES, LANES)).T
    last = lax.broadcasted_iota(jnp.int32, (LANES, LANES), 1) == LANES - 1
    win_out_ref[0, 0, :, 0:w_buf - LANES] = shifted[:, 0:w_buf - LANES]
    win_out_ref[0, 0, :, w_buf - LANES:] = jnp.where(last, new_col, shifted[:, w_buf - LANES:])


def _nsa_sample_sel(past, layer, idx, page_table, sel_pool, q8, new_s, new_w, win, oc, gl):
    b, n_past_pages = page_table.shape
    n_top = idx.shape[0] // (b * NSA_KV_HEADS)
    n_past_blocks = past // SEL_BLOCK
    per_page = PAGE_SIZE // SEL_BLOCK

    def blk_spec(k):
        def imap(i, g, idx_ref, pt_ref):
            j = jnp.clip(idx_ref[(i * NSA_KV_HEADS + g) * n_top + k], 0, n_past_blocks - 1)
            return (layer, pt_ref[i * n_past_pages + j // per_page], g, 0, 0)
        return pl.BlockSpec((None, None, None, LANES, PAGE_SIZE), imap)

    v4 = lambda w: pl.BlockSpec((1, 1, 8, w), lambda i, g, a, c: (i, g, 0, 0))
    v1 = pl.BlockSpec((1, 1, 1, LANES), lambda i, g, a, c: (i, g, 0, 0))
    grid_spec = pltpu.PrefetchScalarGridSpec(
        num_scalar_prefetch=2,
        grid=(b, NSA_KV_HEADS),
        in_specs=[blk_spec(k) for k in range(n_top)] + [
            v4(LANES), v1, v1,
            pl.BlockSpec((None, None, 1, LANES, win.shape[4]), lambda i, g, a, c: (layer, i, g, 0, 0)),
            v4(LANES), v1],
        out_specs=[v4(LANES), pl.BlockSpec((1, 1, LANES, win.shape[4]), lambda i, g, a, c: (i, g, 0, 0))],
    )
    return pl.pallas_call(
        functools.partial(_nsa_sample_sel_body, past, n_top),
        grid_spec=grid_spec,
        out_shape=[jax.ShapeDtypeStruct((b, NSA_KV_HEADS, 8, LANES), F32),
                   jax.ShapeDtypeStruct((b, NSA_KV_HEADS, LANES, win.shape[4]), F32)],
        compiler_params=_cparams(("arbitrary", "arbitrary")),
        name="nsa_sample_sel",
    )(idx, page_table.reshape(-1), *([sel_pool] * n_top), q8, new_s, new_w, win, oc, gl)


def _finish_body(x_ref, ho_ref, no_ref, wo_ref, g1_ref, g2_ref, g3_ref, w1_ref, w2_ref, o_ref):
    wo = wo_ref[...]
    y = _dot(ho_ref[...].astype(BF16), wo[0:HG_W]) + _dot(no_ref[...].astype(BF16), wo[HG_W:])
    h = x_ref[...] + _rms(y, g1_ref[...])
    a = _dot(_rms(h, g2_ref[...]).astype(BF16), w1_ref[...])
    dff = a.shape[1] // 2
    u, v = a[:, :dff], a[:, dff:]
    act = (u * _sigmoid(u)) * v
    y2 = _dot(act.astype(BF16), w2_ref[...])
    o_ref[...] = h + _rms(y2, g3_ref[...])


def _finish(x2d, ho, no, wo, g1, g2, g3, w1, w2, tm):
    m, d = x2d.shape
    row = lambda w: pl.BlockSpec((tm, w), lambda i: (i, 0))
    full = lambda a: pl.BlockSpec(a.shape, lambda i: (0, 0), pipeline_mode=pl.Buffered(1))
    return pl.pallas_call(
        _finish_body,
        grid=(m // tm,),
        in_specs=[row(d), row(HG_W), row(NSA_W), full(wo), full(g1), full(g2), full(g3), full(w1), full(w2)],
        out_specs=row(d),
        out_shape=jax.ShapeDtypeStruct((m, d), F32),
        compiler_params=_cparams(("arbitrary",)),
        name="finish",
    )(x2d, ho, no, wo, g1, g2, g3, w1, w2)


def _tile(n, pref):
    while n % pref:
        pref //= 2
    return pref


def kernel(x_prompt, x_sample, state_hgrn, cache_cmp_kv, cache_sel_kv, cache_win_kv, page_table,
           norm_mix_pre, norm_mix_post, norm_ffn_pre, norm_ffn_post, w_in, w_out, hg_lb_logits,
           hg_out_norm, cmp_pe, w_cmp, w_ffn_in, w_ffn_out):
    bp, tp, d = x_prompt.shape
    bs, ts, _ = x_sample.shape
    assert ts == 1
    depth = w_in.shape[0]
    n_pages = page_table.shape[1]
    past = n_pages * PAGE_SIZE
    assert tp % PAGE_SIZE == 0 and tp >= WINDOW + PAGE_SIZE

    def token_minor(cache):
        return cache.transpose(0, 1, 3, 4, 5, 2).reshape(cache.shape[0], cache.shape[1], NSA_KV_HEADS,
                                                        2 * HEAD_DIM, cache.shape[2])

    def token_major(a):
        return a.reshape(a.shape[0], NSA_KV_HEADS, 2, HEAD_DIM, a.shape[3]).transpose(0, 4, 1, 2, 3)

    tm_p = _tile(tp, 512)
    tabs_p = _rope_tables(jnp.arange(tp))
    tabs_s = _rope_tables(jnp.full((bs * ts,), past, jnp.int32))
    cmp_pool, sel_pool, win_pool = token_minor(cache_cmp_kv), token_minor(cache_sel_kv), token_minor(cache_win_kv)
    prompt_pages = jnp.zeros((bp, tp // PAGE_SIZE), jnp.int32)
    n_sel_s = -(-(past + ts) // SEL_BLOCK)
    n_sel_pad_s = -(-n_sel_s // LANES) * LANES
    n_top_s = min(N_SELECT, n_sel_s)

    xp = x_prompt.reshape(bp * tp, d)
    xs = x_sample.reshape(bs * ts, d)
    hgp, hgs, cmpp, cmps, selp, sels, winp, wins = [], [], [], [], [], [], [], []
    for l in range(depth):
        w_in_bf = _pad_w_in(w_in[l])
        w_out_bf = w_out[l].astype(BF16)
        w1_bf = w_ffn_in[l].astype(BF16)
        w2_bf = w_ffn_out[l].astype(BF16)
        g_pre, g_post = norm_mix_pre[l][None, :], norm_mix_post[l][None, :]
        g_fpre, g_fpost = norm_ffn_pre[l][None, :], norm_ffn_post[l][None, :]
        gain = hg_out_norm[l][None, :]
        w_c, pe_c = _compress_weights(w_cmp[l], cmp_pe[l])

        hq, fl, hi, hg, qn, qr, kvc_t, kvs_t, kvw_t, gl = _inproj(xp, g_pre, w_in_bf, tabs_p, tm_p, True)
        r3 = lambda a: a.reshape(bp, tp, a.shape[-1])
        ho, s_fin = _hgrn_prompt(l, hg_lb_logits, r3(hq), r3(fl), r3(hi), r3(hg), gain, _tile(tp, 256))
        y, const = _compress(kvc_t, lambda i, pg, pt: (i, 0, 0, pg), prompt_pages, w_c, pe_c,
                             _tile(tp // PAGE_SIZE, CMP_PAGES_PER_STEP))
        no = _nsa_prompt(r3(qn), r3(qr), r3(gl), y, const, kvs_t, kvw_t, PAGE_SIZE, _tile(tp, SEL_KV_TILE))
        xp = _finish(xp, ho.reshape(bp * tp, HG_W), no.reshape(bp * tp, NSA_W), w_out_bf, g_post, g_fpre,
                     g_fpost, w1_bf, w2_bf, tm_p)
        hgp.append(s_fin)
        cmpp.append(token_major(kvc_t))
        selp.append(token_major(kvs_t))
        winp.append(token_major(kvw_t[..., tp - min(WINDOW, tp):]))

        hq, fl, hi, hg, qn, qr, kvc, kvs, kvw, gl = _inproj(xs, g_pre, w_in_bf, tabs_s, bs * ts, False)
        v3 = lambda a: a.reshape(bs, 1, a.shape[-1])
        s_new, ho = _hgrn_sample(l, hg_lb_logits, state_hgrn[l], v3(hq), v3(fl), v3(hi), v3(hg), gain)
        y, const = _compress(cmp_pool, lambda i, pg, pt, l=l: (l, pt[i * n_pages + pg], 0, 0, 0),
                             page_table, w_c, pe_c, _tile(n_pages, CMP_PAGES_PER_STEP))

        def q8(q):
            q4 = q.reshape(bs, NSA_KV_HEADS, GQA_REP, HEAD_DIM)
            return jnp.pad(q4, ((0, 0), (0, 0), (0, 8 - GQA_REP), (0, LANES - HEAD_DIM)))

        idx, oc = _nsa_sample_cmp(past, q8(qn), y, const, n_sel_pad_s, _tile(bs, 4))
        g4 = lambda a: a.reshape(bs, NSA_KV_HEADS, 1, LANES)
        o8, win_new = _nsa_sample_sel(past, l, idx[:, :, 0, :n_top_s].reshape(-1), page_table, sel_pool, q8(qr),
                                      g4(kvs), g4(kvw), win_pool, oc, g4(gl))
        no = o8[:, :, :GQA_REP, :HEAD_DIM].reshape(bs * ts, NSA_W)
        xs = _finish(xs, ho.reshape(bs * ts, HG_W), no, w_out_bf, g_post, g_fpre, g_fpost, w1_bf, w2_bf,
                     bs * ts)
        kv5 = lambda a, t: a.reshape(-1, t, NSA_KV_HEADS, 2, HEAD_DIM)
        hgs.append(s_new)
        cmps.append(kv5(kvc, ts))
        sels.append(kv5(kvs, ts))
        wins.append(token_major(win_new))

    return (xp.reshape(bp, tp, d), xs.reshape(bs, ts, d), jnp.stack(hgp), jnp.stack(hgs), jnp.stack(cmpp),
            jnp.stack(cmps), jnp.stack(selp), jnp.stack(sels), jnp.stack(winp), jnp.stack(wins))
```
